```python
import functools
import jax, jax.numpy as jnp
from jax import lax
import numpy as np

D_MODEL = 4096
BATCH = 4
SEQ = 2048
DEPTH = 2
DEC_BATCH = 8
DEC_SEQ = 1
PAST_LEN = 16384
PAGE_SIZE = 128

N_EVEN = (DEPTH + 1) // 2
N_ODD = DEPTH // 2
A_WIDTH = D_MODEL // 2
A_HEADS = 8
A_DV = A_WIDTH // A_HEADS
A_DQK = A_DV // 2
A_CHUNK = 64
B_WIDTH = D_MODEL // 2
B_HEADS = 16
B_DH = B_WIDTH // B_HEADS
B_KV_HEADS = 4
IDX_HEADS = 16
IDX_DIM = 64
TOPK_MAX = 256
Q_BLOCK = 128
C_WIDTH = D_MODEL // 2
C_GROUPS = 8
C_CHUNK = 128
D_WIDTH = D_MODEL // 2
D_CONV = 31
D_FF = 11008
FF_CONV = 3
EPS = 1e-6
EVEN_SPLITS = (A_HEADS * A_DQK, A_HEADS * A_DQK, A_WIDTH, A_WIDTH, A_HEADS, A_HEADS,
               B_WIDTH, B_KV_HEADS * B_DH, B_KV_HEADS * B_DH, IDX_HEADS * IDX_DIM, IDX_HEADS, IDX_DIM)
IN_EVEN = sum(EVEN_SPLITS)
ODD_SPLITS = (C_WIDTH, C_WIDTH, D_WIDTH, D_WIDTH)
IN_ODD = sum(ODD_SPLITS)

kernel_name = 'hybrid_mlstm_dsa_gmlp_conformer_step'


def rmsnorm(x, g):
    xf = x.astype(jnp.float32)
    y = xf * lax.rsqrt(jnp.mean(xf * xf, axis=-1, keepdims=True) + EPS)
    return (y * g.astype(jnp.float32)).astype(x.dtype)


def layernorm(x, g, b):
    xf = x.astype(jnp.float32)
    xc = xf - jnp.mean(xf, axis=-1, keepdims=True)
    y = xc * lax.rsqrt(jnp.mean(xc * xc, axis=-1, keepdims=True) + EPS)
    return (y * g.astype(jnp.float32) + b.astype(jnp.float32)).astype(x.dtype)


def split_cols(x, sizes):
    return jnp.split(x, np.cumsum(sizes)[:-1].tolist(), axis=-1)


def causal_dwconv(x, buf, w, b):
    xp = jnp.concatenate([buf.astype(x.dtype), x], axis=1)
    y = lax.conv_general_dilated(xp, w[:, None, :].astype(x.dtype), (1,), 'VALID',
                                 dimension_numbers=('NWC', 'WIO', 'NWC'),
                                 feature_group_count=x.shape[-1])
    return y + b.astype(x.dtype), xp[:, -(w.shape[0] - 1):]


def mlstm_chunkwise(q, k, v, ig, lf, C0, n0, m0):
    B, H, T, _ = q.shape
    L = A_CHUNK if T % A_CHUNK == 0 else T
    nc = T // L

    def chunks(a):
        return jnp.moveaxis(a.reshape(B, H, nc, L, *a.shape[3:]), 2, 0)

    causal = jnp.tril(jnp.ones((L, L), dtype=bool))

    def step(carry, xs):
        C, n, m = carry
        qc, kc, vc, ic, fc = xs
        b = jnp.cumsum(fc, axis=-1)
        log_intra = jnp.where(causal, b[..., :, None] - b[..., None, :] + ic[..., None, :], -jnp.inf)
        log_inter = b + m[..., None]
        m_t = jnp.maximum(log_inter, jnp.max(log_intra, axis=-1))
        w_intra = jnp.exp(log_intra - m_t[..., None])
        w_inter = jnp.exp(log_inter - m_t)
        s = jnp.einsum('bhtd,bhsd->bhts', qc, kc) * w_intra
        num = jnp.einsum('bhts,bhsv->bhtv', s, vc) + w_inter[..., None] * jnp.einsum('bhvd,bhtd->bhtv', C, qc)
        den = jnp.sum(s, axis=-1) + w_inter * jnp.einsum('bhd,bhtd->bht', n, qc)
        h = num / jnp.maximum(jnp.abs(den), jnp.exp(-m_t))[..., None]
        m_new = m_t[..., -1]
        w_end = jnp.exp(b[..., -1:] - b + ic - m_new[..., None])
        decay = jnp.exp(b[..., -1] + m - m_new)
        C_new = decay[..., None, None] * C + jnp.einsum('bhs,bhsv,bhsd->bhvd', w_end, vc, kc)
        n_new = decay[..., None] * n + jnp.einsum('bhs,bhsd->bhd', w_end, kc)
        return (C_new, n_new, m_new), h

    (C, n, m), hs = lax.scan(step, (C0, n0, m0),
                             (chunks(q), chunks(k), chunks(v), chunks(ig), chunks(lf)))
    h = jnp.moveaxis(hs, 0, 2).reshape(B, H, T, -1)
    return h, (C, n, m)


def mlstm_mixer(q, k, v, o_pre, i_pre, f_pre, b_gate, g_head, C0, n0, m0):
    B, T, _ = q.shape
    f32 = jnp.float32

    def heads(a, d):
        return a.reshape(B, T, A_HEADS, d).transpose(0, 2, 1, 3).astype(f32)

    qh = heads(q, A_DQK) * (A_DQK ** -0.5)
    kh = heads(k, A_DQK)
    vh = heads(v, A_DV)
    ig = (i_pre.astype(f32) + b_gate[:A_HEADS].astype(f32)).transpose(0, 2, 1)
    lf = jax.nn.log_sigmoid(f_pre.astype(f32) + b_gate[A_HEADS:].astype(f32)).transpose(0, 2, 1)
    h, (C, n, m) = mlstm_chunkwise(qh, kh, vh, ig, lf, C0.astype(f32), n0.astype(f32), m0.astype(f32))
    h = h.transpose(0, 2, 1, 3)
    h = h * lax.rsqrt(jnp.mean(h * h, axis=-1, keepdims=True) + EPS) * g_head.astype(f32)
    out = jax.nn.sigmoid(o_pre.astype(f32)) * h.reshape(B, T, A_WIDTH)
    dt = q.dtype
    return out.astype(dt), (C.astype(dt), n.astype(dt), m.astype(dt))


def indexer_scores(qi, wi, ki):
    f32 = jnp.float32
    s = jnp.einsum('bthd,bsd->bths', qi.astype(f32), ki.astype(f32)) * (IDX_DIM ** -0.5)
    return jnp.einsum('bths,bth->bts', jax.nn.relu(s), wi.astype(f32)) * (IDX_HEADS ** -0.5)


def attend_selected(q, kg, vg, valid):
    B, T = q.shape[:2]
    qg = q.reshape(B, T, B_KV_HEADS, B_HEADS // B_KV_HEADS, B_DH)
    s = jnp.einsum('btngd,btknd->btngk', qg, kg).astype(jnp.float32) * (B_DH ** -0.5)
    s = jnp.where(valid[:, :, None, None, :], s, -jnp.inf)
    p = jax.nn.softmax(s, axis=-1).astype(vg.dtype)
    o = jnp.einsum('btngk,btknd->btngd', p, vg)
    return o.reshape(B, T, B_HEADS * B_DH)


def dsa_prompt(q, k, v, qi, wi, ki):
    B, S = q.shape[:2]
    topk = min(TOPK_MAX, S // 4)
    nb = S // Q_BLOCK

    def blockify(a):
        return a.reshape(B, nb, Q_BLOCK, *a.shape[2:]).swapaxes(0, 1)

    key_pos = jnp.arange(S)

    def body(xs):
        qb, qib, wib, pos = xs
        sc = indexer_scores(qib, wib, ki)
        sc = jnp.where((key_pos[None, :] <= pos[:, None])[None], sc, -jnp.inf)
        top, idx = lax.top_k(sc, topk)
        valid = jnp.isfinite(top)
        kg = jax.vmap(lambda kb, ib: kb[ib])(k, idx)
        vg = jax.vmap(lambda vb, ib: vb[ib])(v, idx)
        return attend_selected(qb, kg, vg, valid)

    pos_blocks = jnp.arange(S, dtype=jnp.int32).reshape(nb, Q_BLOCK)
    out = lax.map(body, (blockify(q), blockify(qi), blockify(wi), pos_blocks))
    return out.swapaxes(0, 1).reshape(B, S, B_HEADS * B_DH)


def dsa_sample(q, k_new, v_new, qi, wi, ki_new, cache_k, cache_v, cache_ki, page_table):
    B, T = q.shape[:2]
    past = page_table.shape[1] * PAGE_SIZE
    L = past + T
    topk = min(TOPK_MAX, L // 4)
    ki_past = cache_ki[page_table].reshape(B, past, IDX_DIM)
    ki_all = jnp.concatenate([ki_past, ki_new.astype(ki_past.dtype)], axis=1)
    sc = indexer_scores(qi, wi, ki_all)
    visible = jnp.arange(L)[None, :] <= (past + jnp.arange(T))[:, None]
    sc = jnp.where(visible[None], sc, -jnp.inf)
    top, idx = lax.top_k(sc, topk)
    valid = jnp.isfinite(top)
    in_past = idx < past
    pidx = jnp.minimum(idx, past - 1)
    phys = jax.vmap(lambda pt, ib: pt[ib])(page_table, pidx // PAGE_SIZE)
    off = pidx % PAGE_SIZE
    nidx = jnp.clip(idx - past, 0, T - 1)
    kg = jnp.where(in_past[..., None, None], cache_k[phys, off],
                   jax.vmap(lambda kb, ib: kb[ib])(k_new, nidx).astype(cache_k.dtype))
    vg = jnp.where(in_past[..., None, None], cache_v[phys, off],
                   jax.vmap(lambda vb, ib: vb[ib])(v_new, nidx).astype(cache_v.dtype))
    return attend_selected(q, kg, vg, valid)


def even_mixer(xn, w_in, b_gate, g_head, w_out, C0, n0, m0, attend):
    B, T, _ = xn.shape
    aq, ak, av, ao, ai, af, bq, bk, bv, iq, iw, ik = split_cols(xn @ w_in, EVEN_SPLITS)
    h_a, mstate = mlstm_mixer(aq, ak, av, ao, ai, af, b_gate, g_head, C0, n0, m0)
    q = bq.reshape(B, T, B_HEADS, B_DH)
    k = bk.reshape(B, T, B_KV_HEADS, B_DH)
    v = bv.reshape(B, T, B_KV_HEADS, B_DH)
    qi = iq.reshape(B, T, IDX_HEADS, IDX_DIM)
    h_b = attend(q, k, v, qi, iw, ik)
    y = jnp.concatenate([h_a, h_b.astype(h_a.dtype)], axis=-1) @ w_out
    return y, (k, v, ik), mstate


def spatial_gate(v, w_sp, b_sp):
    B, T, _ = v.shape
    Lc = C_CHUNK if T % C_CHUNK == 0 else T
    nc = T // Lc
    wm = jnp.tril(w_sp[:, :Lc, :Lc])
    vr = v.reshape(B, nc, Lc, C_GROUPS, C_WIDTH // C_GROUPS)
    out = jnp.einsum('gts,bcsgd->bctgd', wm.astype(v.dtype), vr) + b_sp[:, :Lc].T.astype(v.dtype)[None, None, :, :, None]
    return out.reshape(B, T, C_WIDTH)


def odd_mixer(xn, w_in, g_v, b_v, w_sp, b_sp, w_dc, b_dc, g_dn, b_dn, w_out, conv_buf):
    u, v, a, gte = split_cols(xn @ w_in, ODD_SPLITS)
    vn = layernorm(v, g_v, b_v)
    c_out = u * spatial_gate(vn, w_sp, b_sp)
    glu = a * jax.nn.sigmoid(gte)
    dc, new_buf = causal_dwconv(glu, conv_buf, w_dc, b_dc)
    d_out = jax.nn.silu(layernorm(dc, g_dn, b_dn))
    y = jnp.concatenate([c_out, d_out], axis=-1) @ w_out
    return y, vn, new_buf


def conv_ffn(xn, w_up, w_fc, b_fc, w_down, buf):
    hup = xn @ w_up
    hc, new_buf = causal_dwconv(hup, buf, w_fc, b_fc)
    g, val = jnp.split(hc, 2, axis=-1)
    return (jax.nn.silu(g) * val) @ w_down, new_buf


def setup_inputs(seed: int = 0) -> dict:
    key = jax.random.key(seed)
    ks = iter(jax.random.split(key, 48))

    def nrm(shape, scale=1.0):
        return jax.random.normal(next(ks), shape, jnp.float32) * scale

    n_pages = PAST_LEN // PAGE_SIZE
    n_phys = (DEC_BATCH * n_pages * 5) // 4
    perm = jax.random.permutation(next(ks), n_phys)
    page_table = perm[:DEC_BATCH * n_pages].reshape(DEC_BATCH, n_pages).astype(jnp.int32)
    b_gate = jnp.concatenate([nrm((N_EVEN, A_HEADS), 0.1), 3.0 + nrm((N_EVEN, A_HEADS), 0.1)], axis=-1)
    return {
        'x_prompt': nrm((BATCH, SEQ, D_MODEL)),
        'x_sample': nrm((DEC_BATCH, DEC_SEQ, D_MODEL)),
        'cache_k': nrm((N_EVEN, n_phys, PAGE_SIZE, B_KV_HEADS, B_DH)),
        'cache_v': nrm((N_EVEN, n_phys, PAGE_SIZE, B_KV_HEADS, B_DH)),
        'cache_kidx': nrm((N_EVEN, n_phys, PAGE_SIZE, IDX_DIM)),
        'state_mlstm_C': nrm((N_EVEN, DEC_BATCH, A_HEADS, A_DV, A_DQK)),
        'state_mlstm_n': nrm((N_EVEN, DEC_BATCH, A_HEADS, A_DQK)),
        'state_mlstm_m': nrm((N_EVEN, DEC_BATCH, A_HEADS)),
        'state_conv_d': nrm((N_ODD, DEC_BATCH, D_CONV - 1, D_WIDTH)),
        'state_ffn_conv': nrm((DEPTH, DEC_BATCH, FF_CONV - 1, 2 * D_FF)),
        'page_table': page_table,
        'w_norm_mix': 1.0 + nrm((DEPTH, D_MODEL), 0.05),
        'w_norm_ffn': 1.0 + nrm((DEPTH, D_MODEL), 0.05),
        'w_in_even': nrm((N_EVEN, D_MODEL, IN_EVEN), D_MODEL ** -0.5),
        'b_gate_even': b_gate,
        'g_head_even': 1.0 + nrm((N_EVEN, A_HEADS, A_DV), 0.05),
        'w_out_even': nrm((N_EVEN, A_WIDTH + B_WIDTH, D_MODEL), (A_WIDTH + B_WIDTH) ** -0.5),
        'w_in_odd': nrm((N_ODD, D_MODEL, IN_ODD), D_MODEL ** -0.5),
        'g_vnorm': 1.0 + nrm((N_ODD, C_WIDTH), 0.05),
        'b_vnorm': nrm((N_ODD, C_WIDTH), 0.02),
        'w_spatial': nrm((N_ODD, C_GROUPS, C_CHUNK, C_CHUNK), C_CHUNK ** -0.5),
        'b_spatial': nrm((N_ODD, C_GROUPS, C_CHUNK), 0.1),
        'w_dconv': nrm((N_ODD, D_CONV, D_WIDTH), D_CONV ** -0.5),
        'b_dconv': nrm((N_ODD, D_WIDTH), 0.02),
        'g_dnorm': 1.0 + nrm((N_ODD, D_WIDTH), 0.05),
        'b_dnorm': nrm((N_ODD, D_WIDTH), 0.02),
        'w_out_odd': nrm((N_ODD, C_WIDTH + D_WIDTH, D_MODEL), (C_WIDTH + D_WIDTH) ** -0.5),
        'w_up': nrm((DEPTH, D_MODEL, 2 * D_FF), D_MODEL ** -0.5),
        'w_fconv': nrm((DEPTH, FF_CONV, 2 * D_FF), FF_CONV ** -0.5),
        'b_fconv': nrm((DEPTH, 2 * D_FF), 0.02),
        'w_down': nrm((DEPTH, D_FF, D_MODEL), D_FF ** -0.5),
        'w_norm_final': 1.0 + nrm((D_MODEL,), 0.05),
    }


def reference(x_prompt, x_sample, cache_k, cache_v, cache_kidx, state_mlstm_C, state_mlstm_n,
              state_mlstm_m, state_conv_d, state_ffn_conv, page_table, w_norm_mix, w_norm_ffn,
              w_in_even, b_gate_even, g_head_even, w_out_even, w_in_odd, g_vnorm, b_vnorm,
              w_spatial, b_spatial, w_dconv, b_dconv, g_dnorm, b_dnorm, w_out_odd,
              w_up, w_fconv, b_fconv, w_down, w_norm_final):
    hp, hs = x_prompt, x_sample
    bp = x_prompt.shape[0]
    kp_l, vp_l, kip_l, ks_l, vs_l, kis_l = [], [], [], [], [], []
    Cp_l, np_l, mp_l, Cs_l, ns_l, ms_l = [], [], [], [], [], []
    gv_s_l, cdp_l, cds_l, ffp_l, ffs_l = [], [], [], [], []
    for l in range(DEPTH):
        j = l // 2
        xpn = rmsnorm(hp, w_norm_mix[l])
        xsn = rmsnorm(hs, w_norm_mix[l])
        if l % 2 == 0:
            yp, (k_p, v_p, ki_p), (C_p, n_p, m_p) = even_mixer(
                xpn, w_in_even[j], b_gate_even[j], g_head_even[j], w_out_even[j],
                jnp.zeros((bp, A_HEADS, A_DV, A_DQK), jnp.float32),
                jnp.zeros((bp, A_HEADS, A_DQK), jnp.float32),
                jnp.zeros((bp, A_HEADS), jnp.float32), dsa_prompt)
            att_s = functools.partial(dsa_sample, cache_k=cache_k[j], cache_v=cache_v[j],
                                      cache_ki=cache_kidx[j], page_table=page_table)
            ys, (k_s, v_s, ki_s), (C_s, n_s, m_s) = even_mixer(
                xsn, w_in_even[j], b_gate_even[j], g_head_even[j], w_out_even[j],
                state_mlstm_C[j], state_mlstm_n[j], state_mlstm_m[j], att_s)
            kp_l.append(k_p); vp_l.append(v_p); kip_l.append(ki_p)
            ks_l.append(k_s); vs_l.append(v_s); kis_l.append(ki_s)
            Cp_l.append(C_p); np_l.append(n_p); mp_l.append(m_p)
            Cs_l.append(C_s); ns_l.append(n_s); ms_l.append(m_s)
        else:
            yp, _, cd_p = odd_mixer(xpn, w_in_odd[j], g_vnorm[j], b_vnorm[j], w_spatial[j], b_spatial[j],
                                    w_dconv[j], b_dconv[j], g_dnorm[j], b_dnorm[j], w_out_odd[j],
                                    jnp.zeros((bp, D_CONV - 1, D_WIDTH), hp.dtype))
            ys, gv_s, cd_s = odd_mixer(xsn, w_in_odd[j], g_vnorm[j], b_vnorm[j], w_spatial[j], b_spatial[j],
                                       w_dconv[j], b_dconv[j], g_dnorm[j], b_dnorm[j], w_out_odd[j],
                                       state_conv_d[j])
            gv_s_l.append(gv_s); cdp_l.append(cd_p); cds_l.append(cd_s)
        hp = hp + yp
        hs = hs + ys
        fp, ff_p = conv_ffn(rmsnorm(hp, w_norm_ffn[l]), w_up[l], w_fconv[l], b_fconv[l], w_down[l],
                            jnp.zeros((bp, FF_CONV - 1, 2 * D_FF), hp.dtype))
        fs, ff_s = conv_ffn(rmsnorm(hs, w_norm_ffn[l]), w_up[l], w_fconv[l], b_fconv[l], w_down[l],
                            state_ffn_conv[l])
        hp = hp + fp
        hs = hs + fs
        ffp_l.append(ff_p); ffs_l.append(ff_s)
    y_prompt = rmsnorm(hp, w_norm_final)
    y_sample = rmsnorm(hs, w_norm_final)
    return (y_prompt, y_sample,
            jnp.stack(kp_l), jnp.stack(vp_l), jnp.stack(kip_l),
            jnp.stack(ks_l), jnp.stack(vs_l), jnp.stack(kis_l),
            jnp.stack(Cp_l), jnp.stack(np_l), jnp.stack(mp_l),
            jnp.stack(Cs_l), jnp.stack(ns_l), jnp.stack(ms_l),
            jnp.stack(gv_s_l),
            jnp.stack(cdp_l), jnp.stack(cds_l),
            jnp.stack(ffp_l), jnp.stack(ffs_l))
```

```python
import functools
import math

import jax
import jax.numpy as jnp
from jax import lax
from jax.experimental import pallas as pl
from jax.experimental.pallas import tpu as pltpu

F32 = jnp.float32
BF16 = jnp.bfloat16
I32 = jnp.int32
EPS = 1e-6
NEG_INF = float("-inf")
INT_MIN = -(2 ** 31)

A_HEADS = 8
A_DQK = 128
A_DV = 256
B_HEADS = 16
B_KV_HEADS = 4
B_DH = 128
IDX_HEADS = 16
IDX_DIM = 64
TOPK_MAX = 256
PAGE = 128
C_GROUPS = 8
C_CHUNK = 128
D_CONV = 31
FF_CONV = 3
LANE = 128
VMEM_LIMIT_BYTES = 48 * 1024 * 1024

_AQ, _AK, _AV, _AO = 0, 1024, 2048, 4096
_BQ, _BK, _BV, _IQ = 6144, 8192, 8704, 9216
_N_MAIN = 10240
_S_AI, _S_AF, _S_IW, _S_IK = 0, 8, 16, 32


def _params(sem):
    return pltpu.CompilerParams(dimension_semantics=sem, vmem_limit_bytes=VMEM_LIMIT_BYTES)


def _dot(a, b):
    return jnp.dot(a, b, preferred_element_type=F32)


def _dot_nt(a, b):
    return lax.dot_general(a, b, (((1,), (1,)), ((), ())), preferred_element_type=F32)


def _sigmoid(x):
    return 1.0 / (1.0 + jnp.exp(-x))


def _col(row):
    n = row.shape[1]
    r = lax.broadcasted_iota(I32, (n, n), 0)
    c = lax.broadcasted_iota(I32, (n, n), 1)
    return jnp.sum(jnp.where(r == c, row, 0.0), axis=1, keepdims=True)


def _rms_kernel(x_ref, g_ref, o_ref):
    x = x_ref[...]
    y = x * lax.rsqrt(jnp.mean(x * x, axis=-1, keepdims=True) + EPS)
    o_ref[...] = (y * g_ref[...]).astype(o_ref.dtype)


def _rmsnorm(x, g, out_dtype):
    m, d = x.shape
    tm = min(m, 256)
    return pl.pallas_call(
        _rms_kernel,
        grid=(m // tm,),
        in_specs=[pl.BlockSpec((tm, d), lambda i: (i, 0)), pl.BlockSpec((1, d), lambda i: (0, 0))],
        out_specs=pl.BlockSpec((tm, d), lambda i: (i, 0)),
        out_shape=jax.ShapeDtypeStruct((m, d), out_dtype),
        compiler_params=_params(("parallel",)),
        name="rmsnorm",
    )(x, g.reshape(1, d))


def _mm_kernel(*refs, nk, has_res):
    if has_res:
        x_ref, w_ref, r_ref, o_ref = refs[:4]
        scratch = refs[4:]
    else:
        x_ref, w_ref, o_ref = refs[:3]
        r_ref = None
        scratch = refs[3:]
    if nk == 1:
        acc = _dot(x_ref[...], w_ref[...])
        if has_res:
            acc = acc + r_ref[...]
        o_ref[...] = acc.astype(o_ref.dtype)
        return
    acc_ref, = scratch
    k = pl.program_id(2)

    @pl.when(k == 0)
    def _():
        acc_ref[...] = jnp.zeros_like(acc_ref)

    acc_ref[...] += _dot(x_ref[...], w_ref[...])

    @pl.when(k == nk - 1)
    def _():
        acc = acc_ref[...]
        if has_res:
            acc = acc + r_ref[...]
        o_ref[...] = acc.astype(o_ref.dtype)


def _matmul(x, w, res=None, out_dtype=F32, tm=1024, tn=512, tk=None):
    m, kdim = x.shape
    n = w.shape[1]
    tm = min(tm, m)
    tn = min(tn, n)
    tk = kdim if tk is None else tk
    assert m % tm == 0 and n % tn == 0 and kdim % tk == 0
    nk = kdim // tk
    in_specs = [pl.BlockSpec((tm, tk), lambda i, j, k: (i, k)),
                pl.BlockSpec((tk, tn), lambda i, j, k: (k, j))]
    args = [x, w]
    if res is not None:
        in_specs.append(pl.BlockSpec((tm, tn), lambda i, j, k: (i, j)))
        args.append(res)
    return pl.pallas_call(
        functools.partial(_mm_kernel, nk=nk, has_res=res is not None),
        grid=(m // tm, n // tn, nk),
        in_specs=in_specs,
        out_specs=pl.BlockSpec((tm, tn), lambda i, j, k: (i, j)),
        out_shape=jax.ShapeDtypeStruct((m, n), out_dtype),
        scratch_shapes=[pltpu.VMEM((tm, tn), F32)] if nk > 1 else [],
        compiler_params=_params(("parallel", "parallel", "arbitrary")),
        name="matmul",
    )(*args)


def _shift_rows(x, s):
    row = lax.broadcasted_iota(I32, x.shape, 0)
    return jnp.where(row >= s, pltpu.roll(x, s, 0), 0.0)


def _ffn_act_kernel(g_ref, v_ref, wg_ref, wv_ref, bg_ref, bv_ref, o_ref):
    def conv(x, w_ref, b_ref):
        w = w_ref[...]
        y = w[FF_CONV - 1:FF_CONV, :] * x + b_ref[...]
        for s in range(1, FF_CONV):
            y = y + w[FF_CONV - 1 - s:FF_CONV - s, :] * _shift_rows(x, s)
        return y

    g = conv(g_ref[0], wg_ref, bg_ref)
    v = conv(v_ref[0], wv_ref, bv_ref)
    o_ref[0] = (g * _sigmoid(g) * v).astype(o_ref.dtype)


def _ffn_act(hup, w_fc, b_fc):
    b, t, f2 = hup.shape
    f = f2 // 2
    tf = 256
    nf = f // tf
    return pl.pallas_call(
        _ffn_act_kernel,
        grid=(b, nf),
        in_specs=[pl.BlockSpec((1, t, tf), lambda i, j: (i, 0, j)),
                  pl.BlockSpec((1, t, tf), lambda i, j: (i, 0, j + nf)),
                  pl.BlockSpec((FF_CONV, tf), lambda i, j: (0, j)),
                  pl.BlockSpec((FF_CONV, tf), lambda i, j: (0, j + nf)),
                  pl.BlockSpec((1, tf), lambda i, j: (0, j)),
                  pl.BlockSpec((1, tf), lambda i, j: (0, j + nf))],
        out_specs=pl.BlockSpec((1, t, tf), lambda i, j: (i, 0, j)),
        out_shape=jax.ShapeDtypeStruct((b, t, f), BF16),
        compiler_params=_params(("parallel", "parallel")),
        name="ffn_act",
    )(hup, hup, w_fc, w_fc, b_fc.reshape(1, f2), b_fc.reshape(1, f2))


def _ffn_act_step_kernel(h_ref, buf_ref, w_ref, b_ref, o_ref, *, f):
    w = w_ref[...]
    hc = w[FF_CONV - 1:FF_CONV, :] * h_ref[...] + b_ref[...]
    for s in range(FF_CONV - 1):
        hc = hc + w[s:s + 1, :] * buf_ref[s]
    g = hc[:, :f]
    o_ref[...] = g * _sigmoid(g) * hc[:, f:]


def _ffn_act_step(hup, buf, w_fc, b_fc):
    b, f2 = hup.shape
    return pl.pallas_call(
        functools.partial(_ffn_act_step_kernel, f=f2 // 2),
        out_shape=jax.ShapeDtypeStruct((b, f2 // 2), F32),
        compiler_params=pltpu.CompilerParams(vmem_limit_bytes=VMEM_LIMIT_BYTES),
        name="ffn_act_step",
    )(hup, buf, w_fc, b_fc.reshape(1, f2))


def _scan_lanes(x, op, fill):
    n = x.shape[1]
    lane = lax.broadcasted_iota(I32, x.shape, 1)
    d = 1
    while d < n:
        x = op(x, jnp.where(lane >= d, pltpu.roll(x, d, 1), fill))
        d *= 2
    return x


def _log_sigmoid(x):
    return jnp.minimum(x, 0.0) - jnp.log(1.0 + jnp.exp(-jnp.abs(x)))


def _gate_kernel(i_ref, f_ref, bi_ref, bf_ref, a_ref, g_ref, m_ref):
    ig = i_ref[...] + bi_ref[...]
    lf = _log_sigmoid(f_ref[...] + bf_ref[...])
    bsum = _scan_lanes(lf, jnp.add, 0.0)
    a = ig - bsum
    g = jnp.maximum(_scan_lanes(a, jnp.maximum, NEG_INF), 0.0)
    a_ref[...] = a
    g_ref[...] = g
    m_ref[...] = bsum + g


def _mlstm_gates(i_t, f_t, b_i, b_f):
    shp = jax.ShapeDtypeStruct(i_t.shape, F32)
    return pl.pallas_call(
        _gate_kernel, out_shape=(shp, shp, shp),
        compiler_params=pltpu.CompilerParams(vmem_limit_bytes=VMEM_LIMIT_BYTES),
        name="mlstm_gates",
    )(i_t, f_t, b_i, b_f)


def _mlstm_kernel(q_ref, k_ref, v_ref, o_ref, a_ref, g_ref, m_ref, gh_ref, out_ref, *, tq):
    qi = pl.program_id(2)
    q = (q_ref[0] * (A_DQK ** -0.5)).astype(BF16)
    g_c = _col(g_ref[0, 0, :, pl.ds(pl.multiple_of(qi * tq, tq), tq)])
    m_c = _col(m_ref[0, 0, :, pl.ds(pl.multiple_of(qi * tq, tq), tq)])
    row = lax.broadcasted_iota(I32, (tq, tq), 0)
    col = lax.broadcasted_iota(I32, (tq, tq), 1)

    def body(kb, carry):
        num, den = carry
        off = pl.multiple_of(kb * tq, tq)
        k = k_ref[0, pl.ds(off, tq), :].astype(BF16)
        v = v_ref[0, pl.ds(off, tq), :].astype(BF16)
        a = a_ref[0, 0, :, pl.ds(off, tq)]
        causal = (col + kb * tq) <= (row + qi * tq)
        w = jnp.exp(jnp.where(causal, a - g_c, NEG_INF))
        s = _dot_nt(q, k) * w
        return num + _dot(s.astype(BF16), v), den + jnp.sum(s, axis=1, keepdims=True)

    num, den = lax.fori_loop(0, qi + 1, body,
                             (jnp.zeros((tq, A_DV), F32), jnp.zeros((tq, 1), F32)))
    h = num / jnp.maximum(jnp.abs(den), jnp.exp(-m_c))
    h = h * lax.rsqrt(jnp.mean(h * h, axis=-1, keepdims=True) + EPS) * gh_ref[0]
    out_ref[0] = (_sigmoid(o_ref[0]) * h).astype(out_ref.dtype)


def _mlstm_prompt(p3, a, g, m, g_head):
    b, t, _ = p3.shape
    tq = 256
    row_spec = pl.BlockSpec((1, 1, 1, t), lambda i, h, j: (i, h, 0, 0))
    return pl.pallas_call(
        functools.partial(_mlstm_kernel, tq=tq),
        grid=(b, A_HEADS, t // tq),
        in_specs=[pl.BlockSpec((1, tq, A_DQK), lambda i, h, j: (i, j, _AQ // A_DQK + h)),
                  pl.BlockSpec((1, t, A_DQK), lambda i, h, j: (i, 0, _AK // A_DQK + h)),
                  pl.BlockSpec((1, t, A_DV), lambda i, h, j: (i, 0, _AV // A_DV + h)),
                  pl.BlockSpec((1, tq, A_DV), lambda i, h, j: (i, j, _AO // A_DV + h)),
                  row_spec, row_spec, row_spec,
                  pl.BlockSpec((1, 1, A_DV), lambda i, h, j: (h, 0, 0))],
        out_specs=pl.BlockSpec((1, tq, A_DV), lambda i, h, j: (i, j, h)),
        out_shape=jax.ShapeDtypeStruct((b, t, A_HEADS * A_DV), BF16),
        compiler_params=_params(("parallel", "parallel", "arbitrary")),
        name="mlstm_prompt",
    )(p3, p3, p3, p3, a, g, m, g_head)


def _mlstm_state_kernel(k_ref, v_ref, a_ref, g_ref, c_ref, n_ref, *, t, tc):
    g_last = g_ref[0, 0, :, t - 1:t]

    def body(cb, carry):
        c_acc, n_acc = carry
        off = pl.multiple_of(cb * tc, tc)
        w = _col(jnp.exp(a_ref[0, 0, :, pl.ds(off, tc)] - g_last))
        kw = k_ref[0, pl.ds(off, tc), :] * w
        vt = v_ref[0, pl.ds(off, tc), :].T.astype(BF16)
        return c_acc + _dot(vt, kw.astype(BF16)), n_acc + jnp.sum(kw, axis=0, keepdims=True)

    c_acc, n_acc = lax.fori_loop(0, t // tc, body,
                                 (jnp.zeros((A_DV, A_DQK), F32), jnp.zeros((1, A_DQK), F32)))
    c_ref[0, 0] = c_acc
    n_ref[0, 0] = n_acc


def _mlstm_prompt_state(p3, a, g):
    b, t, _ = p3.shape
    row_spec = pl.BlockSpec((1, 1, 1, t), lambda i, h: (i, h, 0, 0))
    return pl.pallas_call(
        functools.partial(_mlstm_state_kernel, t=t, tc=256),
        grid=(b, A_HEADS),
        in_specs=[pl.BlockSpec((1, t, A_DQK), lambda i, h: (i, 0, _AK // A_DQK + h)),
                  pl.BlockSpec((1, t, A_DV), lambda i, h: (i, 0, _AV // A_DV + h)),
                  row_spec, row_spec],
        out_specs=(pl.BlockSpec((1, 1, A_DV, A_DQK), lambda i, h: (i, h, 0, 0)),
                   pl.BlockSpec((1, 1, 1, A_DQK), lambda i, h: (i, h, 0, 0))),
        out_shape=(jax.ShapeDtypeStruct((b, A_HEADS, A_DV, A_DQK), F32),
                   jax.ShapeDtypeStruct((b, A_HEADS, 1, A_DQK), F32)),
        compiler_params=_params(("parallel", "parallel")),
        name="mlstm_prompt_state",
    )(p3, p3, a, g)


def _mlstm_step_kernel(q_ref, k_ref, v_ref, o_ref, gh_ref, ig_ref, fp_ref, c0_ref, n0_ref, m0_ref,
                       h_ref, c_ref, n_ref, m_ref):
    q = q_ref[0, 0] * (A_DQK ** -0.5)
    k = k_ref[0, 0]
    v = v_ref[0, 0]
    ig = ig_ref[0, 0]
    lf = _log_sigmoid(fp_ref[0, 0])
    c0 = c0_ref[0, 0]
    n0 = n0_ref[0, 0]
    m0 = m0_ref[0, 0]
    log_inter = lf + m0
    m_t = jnp.maximum(log_inter, ig)
    w_intra = jnp.exp(ig - m_t)
    w_inter = jnp.exp(log_inter - m_t)
    s = jnp.sum(q * k, axis=1, keepdims=True) * w_intra
    num = s * v + w_inter * jnp.sum(c0 * q, axis=1, keepdims=True)
    den = s + w_inter * jnp.sum(n0 * q, axis=1, keepdims=True)
    h = num / jnp.maximum(jnp.abs(den), jnp.exp(-m_t))
    h = h * lax.rsqrt(jnp.mean(h * h, axis=0, keepdims=True) + EPS) * gh_ref[0]
    h_ref[0, 0] = _sigmoid(o_ref[0, 0]) * h
    c_ref[0, 0] = w_inter * c0 + w_intra * (v * k)
    n_ref[0, 0] = w_inter * n0 + w_intra * k
    m_ref[0, 0] = m_t


def _mlstm_step(q, k, v_col, o_col, gh_col, ig, f_pre, c0, n0, m0):
    b = q.shape[0]
    h = A_HEADS
    row = pl.BlockSpec((1, 1, 1, A_DQK), lambda i, j: (i, j, 0, 0))
    colv = pl.BlockSpec((1, 1, A_DV, 1), lambda i, j: (i, j, 0, 0))
    one = pl.BlockSpec((1, 1, 1, 1), lambda i, j: (i, j, 0, 0))
    mat = pl.BlockSpec((1, 1, A_DV, A_DQK), lambda i, j: (i, j, 0, 0))
    return pl.pallas_call(
        _mlstm_step_kernel,
        grid=(b, h),
        in_specs=[row, row, colv, colv, pl.BlockSpec((1, A_DV, 1), lambda i, j: (j, 0, 0)),
                  one, one, mat, row, one],
        out_specs=(colv, mat, row, one),
        out_shape=(jax.ShapeDtypeStruct((b, h, A_DV, 1), F32),
                   jax.ShapeDtypeStruct((b, h, A_DV, A_DQK), F32),
                   jax.ShapeDtypeStruct((b, h, 1, A_DQK), F32),
                   jax.ShapeDtypeStruct((b, h, 1, 1), F32)),
        compiler_params=_params(("parallel", "parallel")),
        name="mlstm_step",
    )(q, k, v_col, o_col, gh_col, ig, f_pre, c0, n0, m0)


def _sort_key(x):
    x = jnp.where(x == 0.0, 0.0, x)
    bits = pltpu.bitcast(x, I32)
    return jnp.where(bits >= 0, bits, bits ^ jnp.int32(0x7FFFFFFF))


def _kth_largest_key(key, k):
    def count_ge(c):
        return jnp.sum(jnp.where(key >= c, 1.0, 0.0), axis=1, keepdims=True)

    kf = jnp.float32(k)
    thr = jnp.where(count_ge(jnp.int32(0)) >= kf, jnp.int32(0), jnp.int32(INT_MIN))
    thr = jnp.broadcast_to(thr, (key.shape[0], 1))

    def body(i, thr):
        cand = thr + lax.shift_left(jnp.int32(1), jnp.int32(30) - i)
        return jnp.where(count_ge(cand) >= kf, cand, thr)

    return lax.fori_loop(0, 31, body, thr)


def _tri_ones(n, dtype):
    r = lax.broadcasted_iota(I32, (n, n), 0)
    c = lax.broadcasted_iota(I32, (n, n), 1)
    return jnp.where(r <= c, 1.0, 0.0).astype(dtype)


def _select_topk(scores, visible, k):
    key = _sort_key(jnp.where(visible, scores, NEG_INF))
    thr = _kth_largest_key(key, k)
    gt = key > thr
    eq = jnp.logical_and(key == thr, visible)
    need = jnp.float32(k) - jnp.sum(jnp.where(gt, 1.0, 0.0), axis=1, keepdims=True)
    tri = _tri_ones(LANE, BF16)
    off = jnp.zeros((scores.shape[0], 1), F32)
    rank = []
    eqf = jnp.where(eq, 1.0, 0.0)
    for c in range(scores.shape[1] // LANE):
        p = _dot(eqf[:, c * LANE:(c + 1) * LANE].astype(BF16), tri) + off
        rank.append(p)
        off = p[:, LANE - 1:LANE]
    rank = jnp.concatenate(rank, axis=1)
    return jnp.logical_or(jnp.logical_and(gt, visible), jnp.logical_and(eq, rank <= need))


def _indexer_scores(qi_head, wi, ki):
    acc = None
    for h in range(IDX_HEADS):
        s = _dot_nt(qi_head(h).astype(BF16), ki) * (IDX_DIM ** -0.5)
        term = jnp.maximum(s, 0.0) * wi[:, h:h + 1]
        acc = term if acc is None else acc + term
    return acc * (IDX_HEADS ** -0.5)


def _dsa_prompt_kernel(q_ref, k_ref, v_ref, qi_ref, sm_ref, smq_ref, o_ref, *, tq, topk):
    j = pl.program_id(1)
    t = k_ref.shape[1]
    ki = sm_ref[0, :, _S_IK:_S_IK + IDX_DIM].astype(BF16)
    wi = smq_ref[0, :, _S_IW:_S_IW + IDX_HEADS]
    scores = _indexer_scores(lambda h: qi_ref[0, :, h * IDX_DIM:(h + 1) * IDX_DIM], wi, ki)
    qpos = lax.broadcasted_iota(I32, (tq, t), 0) + j * tq
    kpos = lax.broadcasted_iota(I32, (tq, t), 1)
    sel = _select_topk(scores, kpos <= qpos, topk)
    bias = jnp.where(sel, 0.0, NEG_INF)
    group = B_HEADS // B_KV_HEADS
    for n in range(B_KV_HEADS):
        kn = k_ref[0, :, n * B_DH:(n + 1) * B_DH].astype(BF16)
        vn = v_ref[0, :, n * B_DH:(n + 1) * B_DH].astype(BF16)
        for gidx in range(group):
            h = n * group + gidx
            qh = q_ref[0, :, h * B_DH:(h + 1) * B_DH].astype(BF16)
            s = _dot_nt(qh, kn) * (B_DH ** -0.5) + bias
            p = jnp.exp(s - jnp.max(s, axis=1, keepdims=True))
            l = jnp.sum(p, axis=1, keepdims=True)
            o_ref[0, :, h * B_DH:(h + 1) * B_DH] = (_dot(p.astype(BF16), vn) / l).astype(o_ref.dtype)


def _dsa_prompt(p3, small3):
    b, t, _ = p3.shape
    tq = 128
    topk = min(TOPK_MAX, t // 4)
    w_q, w_kv, w_qi = B_HEADS * B_DH, B_KV_HEADS * B_DH, IDX_HEADS * IDX_DIM
    return pl.pallas_call(
        functools.partial(_dsa_prompt_kernel, tq=tq, topk=topk),
        grid=(b, t // tq),
        in_specs=[pl.BlockSpec((1, tq, w_q), lambda i, j: (i, j, _BQ // w_q)),
                  pl.BlockSpec((1, t, w_kv), lambda i, j: (i, 0, _BK // w_kv)),
                  pl.BlockSpec((1, t, w_kv), lambda i, j: (i, 0, _BV // w_kv)),
                  pl.BlockSpec((1, tq, w_qi), lambda i, j: (i, j, _IQ // w_qi)),
                  pl.BlockSpec((1, t, LANE), lambda i, j: (i, 0, 0)),
                  pl.BlockSpec((1, tq, LANE), lambda i, j: (i, j, 0))],
        out_specs=pl.BlockSpec((1, tq, w_q), lambda i, j: (i, j, 0)),
        out_shape=jax.ShapeDtypeStruct((b, t, w_q), BF16),
        compiler_params=_params(("parallel", "parallel")),
        name="dsa_prompt",
    )(p3, p3, p3, p3, small3, small3)


def _dsa_step_scores_kernel(pt_ref, qi_ref, wi_ref, ki_ref, o_ref):
    del pt_ref
    qi = qi_ref[0].astype(BF16)
    s = _dot_nt(qi, ki_ref[0].astype(BF16)) * (IDX_DIM ** -0.5)
    o_ref[0] = jnp.sum(jnp.maximum(s, 0.0) * wi_ref[0], axis=0, keepdims=True) * (IDX_HEADS ** -0.5)


def _dsa_step_scores(page_table, qi, wi_col, cache_ki):
    b, n_pages = page_table.shape
    grid_spec = pltpu.PrefetchScalarGridSpec(
        num_scalar_prefetch=1,
        grid=(b, n_pages),
        in_specs=[pl.BlockSpec((1, IDX_HEADS, IDX_DIM), lambda i, p, pt: (i, 0, 0)),
                  pl.BlockSpec((1, IDX_HEADS, 1), lambda i, p, pt: (i, 0, 0)),
                  pl.BlockSpec((1, PAGE, IDX_DIM), lambda i, p, pt: (pt[i, p], 0, 0))],
        out_specs=pl.BlockSpec((1, 1, PAGE), lambda i, p, pt: (i, 0, p)),
    )
    return pl.pallas_call(
        _dsa_step_scores_kernel,
        grid_spec=grid_spec,
        out_shape=jax.ShapeDtypeStruct((b, 1, n_pages * PAGE), F32),
        compiler_params=_params(("parallel", "arbitrary")),
        name="dsa_step_scores",
    )(page_table, qi, wi_col, cache_ki)


def _dsa_step_select_kernel(sc_ref, self_ref, sel_ref, selself_ref, *, topk):
    sc = sc_ref[...]
    sself = self_ref[...]
    key = _sort_key(sc)
    key_self = _sort_key(sself)
    kf = jnp.float32(topk)

    def count_ge(c):
        return (jnp.sum(jnp.where(key >= c, 1.0, 0.0), axis=1, keepdims=True)
                + jnp.where(key_self >= c, 1.0, 0.0))

    thr = jnp.where(count_ge(jnp.int32(0)) >= kf, jnp.int32(0), jnp.int32(INT_MIN))

    def body(i, thr):
        cand = thr + lax.shift_left(jnp.int32(1), jnp.int32(30) - i)
        return jnp.where(count_ge(cand) >= kf, cand, thr)

    thr = lax.fori_loop(0, 31, body, thr)
    gt = key > thr
    eq = key == thr
    n_gt = jnp.sum(jnp.where(gt, 1.0, 0.0), axis=1, keepdims=True) + jnp.where(key_self > thr, 1.0, 0.0)
    need = kf - n_gt
    rank = _scan_lanes(jnp.where(eq, 1.0, 0.0), jnp.add, 0.0)
    sel_ref[...] = jnp.where(jnp.logical_or(gt, jnp.logical_and(eq, rank <= need)), 1.0, 0.0)
    n_eq = rank[:, -1:]
    selself_ref[...] = jnp.where(
        jnp.logical_or(key_self > thr, jnp.logical_and(key_self == thr, n_eq < need)), 1.0, 0.0)


def _dsa_step_select(scores, self_score, topk):
    b, s = scores.shape
    return pl.pallas_call(
        functools.partial(_dsa_step_select_kernel, topk=topk),
        out_shape=(jax.ShapeDtypeStruct((b, s), F32), jax.ShapeDtypeStruct((b, 1), F32)),
        compiler_params=pltpu.CompilerParams(vmem_limit_bytes=VMEM_LIMIT_BYTES),
        name="dsa_step_select",
    )(scores, self_score)


def _dsa_step_attend_kernel(pt_ref, q_ref, sel_ref, selself_ref, kn_ref, vn_ref, k_ref, v_ref, o_ref,
                            m_sc, l_sc, acc_sc):
    del pt_ref
    p_idx = pl.program_id(1)
    n_pages = pl.num_programs(1)
    group = B_HEADS // B_KV_HEADS

    @pl.when(p_idx == 0)
    def _():
        m_sc[...] = jnp.full_like(m_sc, NEG_INF)
        l_sc[...] = jnp.zeros_like(l_sc)
        acc_sc[...] = jnp.zeros_like(acc_sc)

    q = q_ref[0].astype(BF16)
    kv_of_row = lax.broadcasted_iota(I32, (B_HEADS, 1), 0) // group

    def update(k_blk, v_blk, sel_row):
        s = jnp.zeros((B_HEADS, sel_row.shape[1]), F32)
        for n in range(B_KV_HEADS):
            kn = k_blk[:, n * B_DH:(n + 1) * B_DH].astype(BF16)
            s = jnp.where(kv_of_row == n, _dot_nt(q, kn), s)
        s = s * (B_DH ** -0.5) + jnp.where(sel_row > 0.0, 0.0, NEG_INF)
        m_old = m_sc[...]
        m_new = jnp.maximum(m_old, jnp.max(s, axis=1, keepdims=True))
        m_safe = jnp.where(m_new == NEG_INF, 0.0, m_new)
        alpha = jnp.exp(m_old - m_safe)
        p = jnp.exp(s - m_safe)
        pb = p.astype(BF16)
        pv = jnp.zeros((B_HEADS, B_DH), F32)
        for n in range(B_KV_HEADS):
            vn = v_blk[:, n * B_DH:(n + 1) * B_DH].astype(BF16)
            pv = jnp.where(kv_of_row == n, _dot(pb, vn), pv)
        l_sc[...] = alpha * l_sc[...] + jnp.sum(p, axis=1, keepdims=True)
        acc_sc[...] = alpha * acc_sc[...] + pv
        m_sc[...] = m_new

    update(k_ref.at[0], v_ref.at[0], sel_ref[0])

    @pl.when(p_idx == n_pages - 1)
    def _():
        update(kn_ref.at[0], vn_ref.at[0], selself_ref[0])
        o_ref[0] = acc_sc[...] / l_sc[...]


def _dsa_step_attend(page_table, q, sel, sel_self, k_new, v_new, cache_k, cache_v):
    b, n_pages = page_table.shape
    w_kv = B_KV_HEADS * B_DH
    grid_spec = pltpu.PrefetchScalarGridSpec(
        num_scalar_prefetch=1,
        grid=(b, n_pages),
        in_specs=[pl.BlockSpec((1, B_HEADS, B_DH), lambda i, p, pt: (i, 0, 0)),
                  pl.BlockSpec((1, 1, PAGE), lambda i, p, pt: (i, 0, p)),
                  pl.BlockSpec((1, 1, PAGE), lambda i, p, pt: (i, 0, 0)),
                  pl.BlockSpec((1, PAGE, w_kv), lambda i, p, pt: (i, 0, 0)),
                  pl.BlockSpec((1, PAGE, w_kv), lambda i, p, pt: (i, 0, 0)),
                  pl.BlockSpec((1, PAGE, w_kv), lambda i, p, pt: (pt[i, p], 0, 0)),
                  pl.BlockSpec((1, PAGE, w_kv), lambda i, p, pt: (pt[i, p], 0, 0))],
        out_specs=pl.BlockSpec((1, B_HEADS, B_DH), lambda i, p, pt: (i, 0, 0)),
        scratch_shapes=[pltpu.VMEM((B_HEADS, 1), F32), pltpu.VMEM((B_HEADS, 1), F32),
                        pltpu.VMEM((B_HEADS, B_DH), F32)],
    )
    return pl.pallas_call(
        _dsa_step_attend_kernel,
        grid_spec=grid_spec,
        out_shape=jax.ShapeDtypeStruct((b, B_HEADS, B_DH), F32),
        compiler_params=_params(("parallel", "arbitrary")),
        name="dsa_step_attend",
    )(page_table, q, sel, sel_self, k_new, v_new, cache_k, cache_v)


def _layernorm(x, g, b):
    xc = x - jnp.mean(x, axis=-1, keepdims=True)
    return xc * lax.rsqrt(jnp.mean(xc * xc, axis=-1, keepdims=True) + EPS) * g + b


def _odd_mix_kernel(u_ref, v_ref, a_ref, gt_ref, gv_ref, bv_ref, wsp_ref, bsp_ref, wdc_ref, bdc_ref,
                    gdn_ref, bdn_ref, o_ref, tail_ref, xbuf, *, tt, width, halo):
    j = pl.program_id(1)
    nj = pl.num_programs(1)

    @pl.when(j == 0)
    def _():
        xbuf[0:halo, :] = jnp.zeros((halo, width), F32)

    vn = _layernorm(v_ref[0], gv_ref[...], bv_ref[...]).astype(BF16)
    r = lax.broadcasted_iota(I32, (tt, tt), 0)
    c = lax.broadcasted_iota(I32, (tt, tt), 1)
    gw = width // C_GROUPS
    for gi in range(C_GROUPS):
        wm = jnp.where(c <= r, wsp_ref[gi], 0.0).astype(BF16)
        sg = _dot(wm, vn[:, gi * gw:(gi + 1) * gw]) + bsp_ref[gi]
        o_ref[0, :, gi * gw:(gi + 1) * gw] = (u_ref[0, :, gi * gw:(gi + 1) * gw] * sg).astype(o_ref.dtype)

    glu = a_ref[0] * _sigmoid(gt_ref[0])
    xbuf[halo:halo + tt, :] = glu
    wdc = wdc_ref[...]
    dc = jnp.zeros((tt, width), F32) + bdc_ref[...]
    base = halo - (D_CONV - 1)
    for kk in range(D_CONV):
        dc = dc + wdc[kk:kk + 1, :] * xbuf[base + kk:base + kk + tt, :]
    d = _layernorm(dc, gdn_ref[...], bdn_ref[...])
    o_ref[0, :, width:2 * width] = (d * _sigmoid(d)).astype(o_ref.dtype)

    xbuf[0:halo, :] = glu[tt - halo:tt, :]

    @pl.when(j == nj - 1)
    def _():
        tail_ref[0] = glu[tt - halo:tt, :]


def _odd_mix(p3, g_v, b_v, w_sp, b_sp, w_dc, b_dc, g_dn, b_dn):
    b, t, w4 = p3.shape
    width = w4 // 4
    tt = C_CHUNK
    halo = 32
    vec = lambda x: x.reshape(1, width)
    vspec = pl.BlockSpec((1, width), lambda i, j: (0, 0))
    return pl.pallas_call(
        functools.partial(_odd_mix_kernel, tt=tt, width=width, halo=halo),
        grid=(b, t // tt),
        in_specs=[pl.BlockSpec((1, tt, width), lambda i, j: (i, j, 0)),
                  pl.BlockSpec((1, tt, width), lambda i, j: (i, j, 1)),
                  pl.BlockSpec((1, tt, width), lambda i, j: (i, j, 2)),
                  pl.BlockSpec((1, tt, width), lambda i, j: (i, j, 3)),
                  vspec, vspec,
                  pl.BlockSpec((C_GROUPS, tt, tt), lambda i, j: (0, 0, 0)),
                  pl.BlockSpec((C_GROUPS, tt, 1), lambda i, j: (0, 0, 0)),
                  pl.BlockSpec((D_CONV, width), lambda i, j: (0, 0)),
                  vspec, vspec, vspec],
        out_specs=(pl.BlockSpec((1, tt, 2 * width), lambda i, j: (i, j, 0)),
                   pl.BlockSpec((1, halo, width), lambda i, j: (i, 0, 0))),
        out_shape=(jax.ShapeDtypeStruct((b, t, 2 * width), BF16),
                   jax.ShapeDtypeStruct((b, halo, width), F32)),
        scratch_shapes=[pltpu.VMEM((halo + tt, width), F32)],
        compiler_params=_params(("parallel", "arbitrary")),
        name="odd_mix",
    )(p3, p3, p3, p3, vec(g_v), vec(b_v), w_sp, b_sp.reshape(C_GROUPS, tt, 1), w_dc, vec(b_dc),
      vec(g_dn), vec(b_dn))


def _odd_mix_step_kernel(p_ref, buf_ref, gv_ref, bv_ref, wsp_ref, bsp_ref, wdc_ref, bdc_ref,
                         gdn_ref, bdn_ref, o_ref, vn_ref, glu_ref, *, width):
    u = p_ref[:, 0:width]
    v = p_ref[:, width:2 * width]
    a = p_ref[:, 2 * width:3 * width]
    gt = p_ref[:, 3 * width:4 * width]
    vn = _layernorm(v, gv_ref[...], bv_ref[...])
    vn_ref[...] = vn
    o_ref[:, 0:width] = u * (wsp_ref[...] * vn + bsp_ref[...])
    glu = a * _sigmoid(gt)
    glu_ref[...] = glu
    wdc = wdc_ref[...]
    dc = wdc[D_CONV - 1:D_CONV, :] * glu + bdc_ref[...]
    for kk in range(D_CONV - 1):
        dc = dc + wdc[kk:kk + 1, :] * buf_ref[kk]
    d = _layernorm(dc, gdn_ref[...], bdn_ref[...])
    o_ref[:, width:2 * width] = d * _sigmoid(d)


def _odd_mix_step(p, buf, g_v, b_v, w_sp_vec, b_sp_vec, w_dc, b_dc, g_dn, b_dn):
    b, w4 = p.shape
    width = w4 // 4
    vec = lambda x: x.reshape(1, width)
    return pl.pallas_call(
        functools.partial(_odd_mix_step_kernel, width=width),
        out_shape=(jax.ShapeDtypeStruct((b, 2 * width), F32), jax.ShapeDtypeStruct((b, width), F32),
                   jax.ShapeDtypeStruct((b, width), F32)),
        compiler_params=pltpu.CompilerParams(vmem_limit_bytes=VMEM_LIMIT_BYTES),
        name="odd_mix_step",
    )(p, buf, vec(g_v), vec(b_v), vec(w_sp_vec), vec(b_sp_vec), w_dc, vec(b_dc), vec(g_dn), vec(b_dn))


def _pad_rows(x, rows):
    return jnp.pad(x, ((0, rows - x.shape[0]), (0, 0)))


def _split_even_weights(w):
    sizes = (1024, 1024, 2048, 2048, 8, 8, 2048, 512, 512, 1024, 16, 64)
    offs = [0]
    for s in sizes:
        offs.append(offs[-1] + s)
    col = lambda i: w[:, offs[i]:offs[i + 1]]
    main = jnp.concatenate([col(0), col(1), col(2), col(3), col(6), col(7), col(8), col(9)], axis=1)
    small = jnp.concatenate([col(4), col(5), col(10), col(11),
                             jnp.zeros((w.shape[0], LANE - 96), w.dtype)], axis=1)
    return main.astype(BF16), small.astype(BF16)


def _conv_ffn(h_p, h_s, bp, tp, g, w_up, w_fc, b_fc, w_down, buf_s):
    f2 = w_up.shape[1]
    nb = buf_s.shape[0]
    w_up_b = w_up.astype(BF16)
    w_down_b = w_down.astype(BF16)
    xn = _rmsnorm(h_p, g, BF16)
    hup = _matmul(xn, w_up_b).reshape(bp, tp, f2)
    act = _ffn_act(hup, w_fc, b_fc)
    h_p_new = _matmul(act.reshape(bp * tp, f2 // 2), w_down_b, res=h_p, tm=512, tk=f2 // 4)
    ff_p = hup[:, tp - (FF_CONV - 1):, :]
    xs = _rmsnorm(h_s, g, BF16)
    hup_s = _matmul(xs, w_up_b)
    act_s = _ffn_act_step(hup_s[:nb], buf_s.transpose(1, 0, 2), w_fc, b_fc)
    h_s_new = _matmul(_pad_rows(act_s.astype(BF16), h_s.shape[0]), w_down_b, res=h_s, tn=1024, tk=f2 // 4)
    ff_s = jnp.concatenate([buf_s[:, 1:, :], hup_s[:nb, None, :]], axis=1)
    return h_p_new, h_s_new, ff_p, ff_s


def kernel(x_prompt, x_sample, cache_k, cache_v, cache_kidx, state_mlstm_C, state_mlstm_n, state_mlstm_m,
           state_conv_d, state_ffn_conv, page_table, w_norm_mix, w_norm_ffn, w_in_even, b_gate_even,
           g_head_even, w_out_even, w_in_odd, g_vnorm, b_vnorm, w_spatial, b_spatial, w_dconv, b_dconv,
           g_dnorm, b_dnorm, w_out_odd, w_up, w_fconv, b_fconv, w_down, w_norm_final):
    bp, tp, d = x_prompt.shape
    bs = x_sample.shape[0]
    assert x_sample.shape[1] == 1
    ms = 16
    h_p = x_prompt.reshape(bp * tp, d)
    h_s = _pad_rows(x_sample.reshape(bs, d), ms)
    n_pages = page_table.shape[1]
    past = n_pages * PAGE

    w_main, w_small = _split_even_weights(w_in_even[0])
    b_i = b_gate_even[0, :A_HEADS]
    b_f = b_gate_even[0, A_HEADS:]
    g_head = g_head_even[0]

    xn = _rmsnorm(h_p, w_norm_mix[0], BF16)
    proj = _matmul(xn, w_main)
    small = _matmul(xn, w_small, tn=LANE)
    p3 = proj.reshape(bp, tp, _N_MAIN)
    small3 = small.reshape(bp, tp, LANE)

    def heads_t(x):
        return x.transpose(0, 2, 1).reshape(bp * A_HEADS, tp)

    a_g, g_g, m_g = _mlstm_gates(heads_t(small3[:, :, _S_AI:_S_AI + A_HEADS]),
                                 heads_t(small3[:, :, _S_AF:_S_AF + A_HEADS]),
                                 jnp.tile(b_i, bp).reshape(-1, 1), jnp.tile(b_f, bp).reshape(-1, 1))
    a4 = a_g.reshape(bp, A_HEADS, 1, tp)
    g4 = g_g.reshape(bp, A_HEADS, 1, tp)
    m4 = m_g.reshape(bp, A_HEADS, 1, tp)
    h_a = _mlstm_prompt(p3, a4, g4, m4, g_head.reshape(A_HEADS, 1, A_DV))
    c_p, n_p = _mlstm_prompt_state(p3, a4, g4)
    m_p = m_g.reshape(bp, A_HEADS, tp)[:, :, -1]
    h_b = _dsa_prompt(p3, small3)
    mix = jnp.concatenate([h_a, h_b], axis=-1).reshape(bp * tp, -1)
    h_p = _matmul(mix, w_out_even[0].astype(BF16), res=h_p)

    k_p = p3[:, :, _BK:_BK + B_KV_HEADS * B_DH].reshape(bp, tp, B_KV_HEADS, B_DH)
    v_p = p3[:, :, _BV:_BV + B_KV_HEADS * B_DH].reshape(bp, tp, B_KV_HEADS, B_DH)
    ki_p = small3[:, :, _S_IK:_S_IK + IDX_DIM]

    xs = _rmsnorm(h_s, w_norm_mix[0], BF16)
    proj_s = _matmul(xs, w_main, tn=1024)[:bs]
    small_s = _matmul(xs, w_small, tn=LANE)[:bs]
    q_s = proj_s[:, _AQ:_AQ + 1024].reshape(bs, A_HEADS, 1, A_DQK)
    kk_s = proj_s[:, _AK:_AK + 1024].reshape(bs, A_HEADS, 1, A_DQK)
    v_s = proj_s[:, _AV:_AV + 2048].reshape(bs, A_HEADS, A_DV, 1)
    o_s = proj_s[:, _AO:_AO + 2048].reshape(bs, A_HEADS, A_DV, 1)
    ig_s = (small_s[:, _S_AI:_S_AI + A_HEADS] + b_i).reshape(bs, A_HEADS, 1, 1)
    fp_s = (small_s[:, _S_AF:_S_AF + A_HEADS] + b_f).reshape(bs, A_HEADS, 1, 1)
    hcol_s, c_s, n_s, m_s = _mlstm_step(
        q_s, kk_s, v_s, o_s, g_head.reshape(A_HEADS, A_DV, 1), ig_s, fp_s,
        state_mlstm_C[0], state_mlstm_n[0].reshape(bs, A_HEADS, 1, A_DQK),
        state_mlstm_m[0].reshape(bs, A_HEADS, 1, 1))
    h_a_s = hcol_s.reshape(bs, A_HEADS * A_DV)

    bq_s = proj_s[:, _BQ:_BQ + 2048].reshape(bs, B_HEADS, B_DH)
    bk_s = proj_s[:, _BK:_BK + 512]
    bv_s = proj_s[:, _BV:_BV + 512]
    iq_s = proj_s[:, _IQ:_IQ + 1024].reshape(bs, IDX_HEADS, IDX_DIM)
    iw_s = small_s[:, _S_IW:_S_IW + IDX_HEADS]
    ik_s = small_s[:, _S_IK:_S_IK + IDX_DIM]
    sc_past = _dsa_step_scores(page_table, iq_s, iw_s.reshape(bs, IDX_HEADS, 1), cache_kidx[0])
    own_page = lambda x: jnp.pad(x[:, None, :], ((0, 0), (0, PAGE - 1), (0, 0)))
    sc_self = _dsa_step_scores(jnp.arange(bs, dtype=I32).reshape(bs, 1), iq_s, iw_s.reshape(bs, IDX_HEADS, 1),
                               own_page(ik_s))[:, 0, 0:1]
    topk_s = min(TOPK_MAX, (past + 1) // 4)
    sel, sel_self = _dsa_step_select(sc_past.reshape(bs, past), sc_self, topk_s)
    w_kv = B_KV_HEADS * B_DH
    h_b_s = _dsa_step_attend(page_table, bq_s, sel.reshape(bs, 1, past),
                             jnp.pad(sel_self, ((0, 0), (0, PAGE - 1))).reshape(bs, 1, PAGE),
                             own_page(bk_s), own_page(bv_s),
                             cache_k[0].reshape(-1, PAGE, w_kv), cache_v[0].reshape(-1, PAGE, w_kv))
    mix_s = jnp.concatenate([h_a_s, h_b_s.reshape(bs, B_HEADS * B_DH)], axis=-1).astype(BF16)
    h_s = _matmul(_pad_rows(mix_s, ms), w_out_even[0].astype(BF16), res=h_s, tn=1024)

    h_p, h_s, ff_p0, ff_s0 = _conv_ffn(h_p, h_s, bp, tp, w_norm_ffn[0], w_up[0], w_fconv[0], b_fconv[0],
                                       w_down[0], state_ffn_conv[0])

    w_in1 = w_in_odd[0].astype(BF16)
    width = w_in1.shape[1] // 4
    xn = _rmsnorm(h_p, w_norm_mix[1], BF16)
    proj1 = _matmul(xn, w_in1).reshape(bp, tp, 4 * width)
    mix1, tail = _odd_mix(proj1, g_vnorm[0], b_vnorm[0], w_spatial[0], b_spatial[0], w_dconv[0], b_dconv[0],
                          g_dnorm[0], b_dnorm[0])
    h_p = _matmul(mix1.reshape(bp * tp, 2 * width), w_out_odd[0].astype(BF16), res=h_p)
    cd_p = tail[:, tail.shape[1] - (D_CONV - 1):, :]

    xs = _rmsnorm(h_s, w_norm_mix[1], BF16)
    proj1_s = _matmul(xs, w_in1, tn=1024)[:bs]
    gw = width // C_GROUPS
    mix1_s, vn_s, glu_s = _odd_mix_step(
        proj1_s, state_conv_d[0].transpose(1, 0, 2), g_vnorm[0], b_vnorm[0],
        jnp.repeat(w_spatial[0, :, 0, 0], gw), jnp.repeat(b_spatial[0, :, 0], gw),
        w_dconv[0], b_dconv[0], g_dnorm[0], b_dnorm[0])
    h_s = _matmul(_pad_rows(mix1_s.astype(BF16), ms), w_out_odd[0].astype(BF16), res=h_s, tn=1024)
    cd_s = jnp.concatenate([state_conv_d[0][:, 1:, :], glu_s[:, None, :]], axis=1)

    h_p, h_s, ff_p1, ff_s1 = _conv_ffn(h_p, h_s, bp, tp, w_norm_ffn[1], w_up[1], w_fconv[1], b_fconv[1],
                                       w_down[1], state_ffn_conv[1])

    y_p = _rmsnorm(h_p, w_norm_final, F32).reshape(bp, tp, d)
    y_s = _rmsnorm(h_s, w_norm_final, F32)[:bs].reshape(bs, 1, d)

    return (y_p, y_s,
            k_p[None], v_p[None], ki_p[None],
            bk_s.reshape(1, bs, 1, B_KV_HEADS, B_DH), bv_s.reshape(1, bs, 1, B_KV_HEADS, B_DH),
            ik_s.reshape(1, bs, 1, IDX_DIM),
            c_p[None], n_p.reshape(1, bp, A_HEADS, A_DQK), m_p[None],
            c_s[None], n_s.reshape(1, bs, A_HEADS, A_DQK), m_s.reshape(1, bs, A_HEADS),
            vn_s.reshape(1, bs, 1, width),
            cd_p[None], cd_s[None],
            jnp.stack([ff_p0, ff_p1]), jnp.stack([ff_s0, ff_s1]))
```

```python
import functools
import math

import jax
import jax.numpy as jnp
from jax import lax
from jax.experimental import pallas as pl
from jax.experimental.pallas import tpu as pltpu

F32 = jnp.float32
BF16 = jnp.bfloat16
I32 = jnp.int32
EPS = 1e-6
NEG_INF = float("-inf")
INT_MIN = -(2 ** 31)

A_HEADS = 8
A_DQK = 128
A_DV = 256
B_HEADS = 16
B_KV_HEADS = 4
B_DH = 128
IDX_HEADS = 16
IDX_DIM = 64
TOPK_MAX = 256
PAGE = 128
C_GROUPS = 8
C_CHUNK = 128
D_CONV = 31
FF_CONV = 3
LANE = 128
VMEM_LIMIT_BYTES = 48 * 1024 * 1024
WS_VMEM_LIMIT_BYTES = 56 * 1024 * 1024

_AQ, _AK, _AV, _AO = 0, 1024, 2048, 4096
_BQ, _BK, _BV, _IQ = 6144, 8192, 8704, 9216
_N_MAIN = 10240
_S_AI, _S_AF, _S_IW, _S_IK = 0, 8, 16, 32


def _params(sem):
    return pltpu.CompilerParams(dimension_semantics=sem, vmem_limit_bytes=VMEM_LIMIT_BYTES)


def _dot(a, b):
    return jnp.dot(a, b, preferred_element_type=F32)


def _dot_nt(a, b):
    return lax.dot_general(a, b, (((1,), (1,)), ((), ())), preferred_element_type=F32)


def _sigmoid(x):
    return 1.0 / (1.0 + jnp.exp(-x))


def _col(row):
    n = row.shape[1]
    r = lax.broadcasted_iota(I32, (n, n), 0)
    c = lax.broadcasted_iota(I32, (n, n), 1)
    return jnp.sum(jnp.where(r == c, row, 0.0), axis=1, keepdims=True)


def _rms_kernel(x_ref, g_ref, o_ref):
    x = x_ref[...]
    y = x * lax.rsqrt(jnp.mean(x * x, axis=-1, keepdims=True) + EPS)
    o_ref[...] = (y * g_ref[...]).astype(o_ref.dtype)


def _rmsnorm(x, g, out_dtype):
    m, d = x.shape
    tm = min(m, 256)
    return pl.pallas_call(
        _rms_kernel,
        grid=(m // tm,),
        in_specs=[pl.BlockSpec((tm, d), lambda i: (i, 0)), pl.BlockSpec((1, d), lambda i: (0, 0))],
        out_specs=pl.BlockSpec((tm, d), lambda i: (i, 0)),
        out_shape=jax.ShapeDtypeStruct((m, d), out_dtype),
        compiler_params=_params(("parallel",)),
        name="rmsnorm",
    )(x, g.reshape(1, d))


def _mm_kernel(*refs, nk, has_res):
    if has_res:
        x_ref, w_ref, r_ref, o_ref = refs[:4]
        scratch = refs[4:]
    else:
        x_ref, w_ref, o_ref = refs[:3]
        r_ref = None
        scratch = refs[3:]
    if nk == 1:
        acc = _dot(x_ref[...], w_ref[...])
        if has_res:
            acc = acc + r_ref[...]
        o_ref[...] = acc.astype(o_ref.dtype)
        return
    acc_ref, = scratch
    k = pl.program_id(2)

    @pl.when(k == 0)
    def _():
        acc_ref[...] = jnp.zeros_like(acc_ref)

    acc_ref[...] += _dot(x_ref[...], w_ref[...])

    @pl.when(k == nk - 1)
    def _():
        acc = acc_ref[...]
        if has_res:
            acc = acc + r_ref[...]
        o_ref[...] = acc.astype(o_ref.dtype)


def _matmul(x, w, layer=0, res=None, out_dtype=F32, tm=1024, tn=512, tk=None):
    m, kdim = x.shape
    n = w.shape[2]
    tm = min(tm, m)
    tn = min(tn, n)
    tk = kdim if tk is None else tk
    assert m % tm == 0 and n % tn == 0 and kdim % tk == 0
    nk = kdim // tk
    in_specs = [pl.BlockSpec((tm, tk), lambda i, j, k: (i, k)),
                pl.BlockSpec((None, tk, tn), lambda i, j, k: (layer, k, j))]
    args = [x, w]
    if res is not None:
        in_specs.append(pl.BlockSpec((tm, tn), lambda i, j, k: (i, j)))
        args.append(res)
    return pl.pallas_call(
        functools.partial(_mm_kernel, nk=nk, has_res=res is not None),
        grid=(m // tm, n // tn, nk),
        in_specs=in_specs,
        out_specs=pl.BlockSpec((tm, tn), lambda i, j, k: (i, j)),
        out_shape=jax.ShapeDtypeStruct((m, n), out_dtype),
        scratch_shapes=[pltpu.VMEM((tm, tn), F32)] if nk > 1 else [],
        compiler_params=_params(("parallel", "parallel", "arbitrary")),
        name="matmul",
    )(*args)


def _mm_ws_kernel(*refs, has_res):
    if has_res:
        x_ref, xs_ref, w_ref, r_ref, rs_ref, o_ref, os_ref, wb_ref = refs
    else:
        x_ref, xs_ref, w_ref, o_ref, os_ref, wb_ref = refs
        r_ref = rs_ref = None

    @pl.when(pl.program_id(1) == 0)
    def _():
        wb_ref[...] = w_ref[...].astype(BF16)
        acc_s = _dot(xs_ref[...], wb_ref[...])
        os_ref[...] = acc_s + rs_ref[...] if has_res else acc_s

    acc = _dot(x_ref[...], wb_ref[...])
    o_ref[...] = (acc + r_ref[...] if has_res else acc).astype(o_ref.dtype)


def _matmul_ws(x, xs, w, layer, col0, n, res=None, res_s=None, tm=1024, tn=512):
    m, kdim = x.shape
    ms = xs.shape[0]
    assert m % tm == 0 and n % tn == 0 and col0 % tn == 0
    c0 = col0 // tn
    has_res = res is not None
    in_specs = [pl.BlockSpec((tm, kdim), lambda j, i: (i, 0)),
                pl.BlockSpec((ms, kdim), lambda j, i: (0, 0)),
                pl.BlockSpec((None, kdim, tn), lambda j, i: (layer, 0, c0 + j))]
    args = [x, xs, w]
    if has_res:
        in_specs += [pl.BlockSpec((tm, tn), lambda j, i: (i, j)), pl.BlockSpec((ms, tn), lambda j, i: (0, j))]
        args += [res, res_s]
    return pl.pallas_call(
        functools.partial(_mm_ws_kernel, has_res=has_res),
        grid=(n // tn, m // tm),
        in_specs=in_specs,
        out_specs=(pl.BlockSpec((tm, tn), lambda j, i: (i, j)), pl.BlockSpec((ms, tn), lambda j, i: (0, j))),
        out_shape=(jax.ShapeDtypeStruct((m, n), F32), jax.ShapeDtypeStruct((ms, n), F32)),
        scratch_shapes=[pltpu.VMEM((kdim, tn), BF16)],
        compiler_params=pltpu.CompilerParams(dimension_semantics=("parallel", "arbitrary"),
                                             vmem_limit_bytes=WS_VMEM_LIMIT_BYTES),
        name="matmul_ws",
    )(*args)


def _matmul_xres(x, w, layer, res, tm=512, tn=256):
    m, kdim = x.shape
    n = w.shape[2]
    assert m % tm == 0 and n % tn == 0
    return pl.pallas_call(
        functools.partial(_mm_kernel, nk=1, has_res=True),
        grid=(m // tm, n // tn),
        in_specs=[pl.BlockSpec((tm, kdim), lambda i, j: (i, 0)),
                  pl.BlockSpec((None, kdim, tn), lambda i, j: (layer, 0, j)),
                  pl.BlockSpec((tm, tn), lambda i, j: (i, j))],
        out_specs=pl.BlockSpec((tm, tn), lambda i, j: (i, j)),
        out_shape=jax.ShapeDtypeStruct((m, n), F32),
        compiler_params=_params(("parallel", "arbitrary")),
        name="matmul_xres",
    )(x, w, res)


def _conv_rows(h, prev, w, b):
    row = lax.broadcasted_iota(I32, h.shape, 0)
    y = w[FF_CONV - 1:FF_CONV, :] * h + b
    for s in range(1, FF_CONV):
        shifted = pltpu.roll(h, s, 0)
        for r in range(s):
            shifted = jnp.where(row == r, prev[8 - s + r:8 - s + r + 1, :], shifted)
        y = y + w[FF_CONV - 1 - s:FF_CONV - s, :] * shifted
    return y


def _ffn_up_kernel(x_ref, xs_ref, wg_ref, wv_ref, cg_ref, cv_ref, bg_ref, bv_ref,
                   act_ref, tg_ref, tv_ref, sg_ref, sv_ref, wb_ref, carry_ref, *, tn, tiles_per_seq):
    i = pl.program_id(1)

    @pl.when(i == 0)
    def _():
        wb_ref[:, 0:tn] = wg_ref[...].astype(BF16)
        wb_ref[:, tn:2 * tn] = wv_ref[...].astype(BF16)
        hs = _dot(xs_ref[...], wb_ref[...])
        sg_ref[...] = hs[:, 0:tn]
        sv_ref[...] = hs[:, tn:2 * tn]

    @pl.when(i % tiles_per_seq == 0)
    def _():
        carry_ref[...] = jnp.zeros_like(carry_ref)

    h = _dot(x_ref[...], wb_ref[...])
    tm = h.shape[0]
    prev = carry_ref[...]
    g = _conv_rows(h[:, 0:tn], prev[:, 0:tn], cg_ref[...], bg_ref[...])
    v = _conv_rows(h[:, tn:2 * tn], prev[:, tn:2 * tn], cv_ref[...], bv_ref[...])
    act_ref[...] = (g * _sigmoid(g) * v).astype(act_ref.dtype)
    carry_ref[...] = h[tm - 8:tm, :]

    @pl.when(i % tiles_per_seq == tiles_per_seq - 1)
    def _():
        tg_ref[0] = h[tm - 8:tm, 0:tn]
        tv_ref[0] = h[tm - 8:tm, tn:2 * tn]


def _ffn_up(x, xs, w_up, w_fc, b_fc, layer, seq_len, tm=1024, tn=256):
    m, kdim = x.shape
    ms = xs.shape[0]
    f = w_up.shape[2] // 2
    nf = f // tn
    tm = min(tm, seq_len)
    assert m % tm == 0 and seq_len % tm == 0 and f % tn == 0
    tiles_per_seq = seq_len // tm
    nseq = m // seq_len
    return pl.pallas_call(
        functools.partial(_ffn_up_kernel, tn=tn, tiles_per_seq=tiles_per_seq),
        grid=(nf, m // tm),
        in_specs=[pl.BlockSpec((tm, kdim), lambda j, i: (i, 0)),
                  pl.BlockSpec((ms, kdim), lambda j, i: (0, 0)),
                  pl.BlockSpec((None, kdim, tn), lambda j, i: (layer, 0, j)),
                  pl.BlockSpec((None, kdim, tn), lambda j, i: (layer, 0, j + nf)),
                  pl.BlockSpec((None, FF_CONV, tn), lambda j, i: (layer, 0, j)),
                  pl.BlockSpec((None, FF_CONV, tn), lambda j, i: (layer, 0, j + nf)),
                  pl.BlockSpec((None, 1, tn), lambda j, i: (layer, 0, j)),
                  pl.BlockSpec((None, 1, tn), lambda j, i: (layer, 0, j + nf))],
        out_specs=(pl.BlockSpec((tm, tn), lambda j, i: (i, j)),
                   pl.BlockSpec((1, 8, tn), lambda j, i: (i // tiles_per_seq, 0, j)),
                   pl.BlockSpec((1, 8, tn), lambda j, i: (i // tiles_per_seq, 0, j)),
                   pl.BlockSpec((ms, tn), lambda j, i: (0, j)),
                   pl.BlockSpec((ms, tn), lambda j, i: (0, j))),
        out_shape=(jax.ShapeDtypeStruct((m, f), BF16),
                   jax.ShapeDtypeStruct((nseq, 8, f), F32), jax.ShapeDtypeStruct((nseq, 8, f), F32),
                   jax.ShapeDtypeStruct((ms, f), F32), jax.ShapeDtypeStruct((ms, f), F32)),
        scratch_shapes=[pltpu.VMEM((kdim, 2 * tn), BF16), pltpu.VMEM((8, 2 * tn), F32)],
        compiler_params=pltpu.CompilerParams(dimension_semantics=("parallel", "arbitrary"),
                                             vmem_limit_bytes=WS_VMEM_LIMIT_BYTES),
        name="ffn_up",
    )(x, xs, w_up, w_up, w_fc, w_fc, b_fc, b_fc)


def _ffn_act_step_kernel(hg_ref, hv_ref, bufg_ref, bufv_ref, wg_ref, wv_ref, bg_ref, bv_ref, o_ref):
    def conv(h_ref, buf_ref, w_ref, b_ref):
        w = w_ref[...]
        y = w[FF_CONV - 1:FF_CONV, :] * h_ref[...] + b_ref[...]
        for s in range(FF_CONV - 1):
            y = y + w[s:s + 1, :] * buf_ref[s]
        return y

    g = conv(hg_ref, bufg_ref, wg_ref, bg_ref)
    o_ref[...] = g * _sigmoid(g) * conv(hv_ref, bufv_ref, wv_ref, bv_ref)


def _ffn_act_step(h_g, h_v, buf, w_fc, b_fc):
    b, f = h_g.shape
    return pl.pallas_call(
        _ffn_act_step_kernel,
        out_shape=jax.ShapeDtypeStruct((b, f), F32),
        compiler_params=pltpu.CompilerParams(vmem_limit_bytes=VMEM_LIMIT_BYTES),
        name="ffn_act_step",
    )(h_g, h_v, buf[:, :, :f], buf[:, :, f:], w_fc[:, :f], w_fc[:, f:],
      b_fc[:, :f], b_fc[:, f:])


def _scan_lanes(x, op, fill):
    n = x.shape[1]
    lane = lax.broadcasted_iota(I32, x.shape, 1)
    d = 1
    while d < n:
        x = op(x, jnp.where(lane >= d, pltpu.roll(x, d, 1), fill))
        d *= 2
    return x


def _log_sigmoid(x):
    return jnp.minimum(x, 0.0) - jnp.log(1.0 + jnp.exp(-jnp.abs(x)))


def _gate_kernel(i_ref, f_ref, bi_ref, bf_ref, a_ref, g_ref, m_ref):
    ig = i_ref[...] + bi_ref[...]
    lf = _log_sigmoid(f_ref[...] + bf_ref[...])
    bsum = _scan_lanes(lf, jnp.add, 0.0)
    a = ig - bsum
    g = jnp.maximum(_scan_lanes(a, jnp.maximum, NEG_INF), 0.0)
    a_ref[...] = a
    g_ref[...] = g
    m_ref[...] = bsum + g


def _mlstm_gates(i_t, f_t, b_i, b_f):
    shp = jax.ShapeDtypeStruct(i_t.shape, F32)
    return pl.pallas_call(
        _gate_kernel, out_shape=(shp, shp, shp),
        compiler_params=pltpu.CompilerParams(vmem_limit_bytes=VMEM_LIMIT_BYTES),
        name="mlstm_gates",
    )(i_t, f_t, b_i, b_f)


def _mlstm_kernel(q_ref, k_ref, v_ref, o_ref, a_ref, g_ref, m_ref, gh_ref, out_ref, *, tq):
    qi = pl.program_id(2)
    q = (q_ref[0] * (A_DQK ** -0.5)).astype(BF16)
    g_c = _col(g_ref[0, 0, :, pl.ds(pl.multiple_of(qi * tq, tq), tq)])
    m_c = _col(m_ref[0, 0, :, pl.ds(pl.multiple_of(qi * tq, tq), tq)])
    row = lax.broadcasted_iota(I32, (tq, tq), 0)
    col = lax.broadcasted_iota(I32, (tq, tq), 1)

    def body(kb, carry):
        num, den = carry
        off = pl.multiple_of(kb * tq, tq)
        k = k_ref[0, pl.ds(off, tq), :].astype(BF16)
        v = v_ref[0, pl.ds(off, tq), :].astype(BF16)
        a = a_ref[0, 0, :, pl.ds(off, tq)]
        causal = (col + kb * tq) <= (row + qi * tq)
        w = jnp.exp(jnp.where(causal, a - g_c, NEG_INF))
        s = _dot_nt(q, k) * w
        return num + _dot(s.astype(BF16), v), den + jnp.sum(s, axis=1, keepdims=True)

    num, den = lax.fori_loop(0, qi + 1, body,
                             (jnp.zeros((tq, A_DV), F32), jnp.zeros((tq, 1), F32)))
    h = num / jnp.maximum(jnp.abs(den), jnp.exp(-m_c))
    h = h * lax.rsqrt(jnp.mean(h * h, axis=-1, keepdims=True) + EPS) * gh_ref[0]
    out_ref[0] = (_sigmoid(o_ref[0]) * h).astype(out_ref.dtype)


def _mlstm_prompt(p3, a, g, m, g_head):
    b, t, _ = p3.shape
    tq = 256
    row_spec = pl.BlockSpec((1, 1, 1, t), lambda i, h, j: (i, h, 0, 0))
    return pl.pallas_call(
        functools.partial(_mlstm_kernel, tq=tq),
        grid=(b, A_HEADS, t // tq),
        in_specs=[pl.BlockSpec((1, tq, A_DQK), lambda i, h, j: (i, j, _AQ // A_DQK + h)),
                  pl.BlockSpec((1, t, A_DQK), lambda i, h, j: (i, 0, _AK // A_DQK + h)),
                  pl.BlockSpec((1, t, A_DV), lambda i, h, j: (i, 0, _AV // A_DV + h)),
                  pl.BlockSpec((1, tq, A_DV), lambda i, h, j: (i, j, _AO // A_DV + h)),
                  row_spec, row_spec, row_spec,
                  pl.BlockSpec((1, 1, A_DV), lambda i, h, j: (h, 0, 0))],
        out_specs=pl.BlockSpec((1, tq, A_DV), lambda i, h, j: (i, j, h)),
        out_shape=jax.ShapeDtypeStruct((b, t, A_HEADS * A_DV), BF16),
        compiler_params=_params(("parallel", "parallel", "arbitrary")),
        name="mlstm_prompt",
    )(p3, p3, p3, p3, a, g, m, g_head)


def _mlstm_state_kernel(k_ref, v_ref, a_ref, g_ref, c_ref, n_ref, *, t, tc):
    g_last = g_ref[0, 0, :, t - 1:t]

    def body(cb, carry):
        c_acc, n_acc = carry
        off = pl.multiple_of(cb * tc, tc)
        w = _col(jnp.exp(a_ref[0, 0, :, pl.ds(off, tc)] - g_last))
        kw = k_ref[0, pl.ds(off, tc), :] * w
        vt = v_ref[0, pl.ds(off, tc), :].T.astype(BF16)
        return c_acc + _dot(vt, kw.astype(BF16)), n_acc + jnp.sum(kw, axis=0, keepdims=True)

    c_acc, n_acc = lax.fori_loop(0, t // tc, body,
                                 (jnp.zeros((A_DV, A_DQK), F32), jnp.zeros((1, A_DQK), F32)))
    c_ref[0, 0] = c_acc
    n_ref[0, 0] = n_acc


def _mlstm_prompt_state(p3, a, g):
    b, t, _ = p3.shape
    row_spec = pl.BlockSpec((1, 1, 1, t), lambda i, h: (i, h, 0, 0))
    return pl.pallas_call(
        functools.partial(_mlstm_state_kernel, t=t, tc=256),
        grid=(b, A_HEADS),
        in_specs=[pl.BlockSpec((1, t, A_DQK), lambda i, h: (i, 0, _AK // A_DQK + h)),
                  pl.BlockSpec((1, t, A_DV), lambda i, h: (i, 0, _AV // A_DV + h)),
                  row_spec, row_spec],
        out_specs=(pl.BlockSpec((1, 1, A_DV, A_DQK), lambda i, h: (i, h, 0, 0)),
                   pl.BlockSpec((1, 1, 1, A_DQK), lambda i, h: (i, h, 0, 0))),
        out_shape=(jax.ShapeDtypeStruct((b, A_HEADS, A_DV, A_DQK), F32),
                   jax.ShapeDtypeStruct((b, A_HEADS, 1, A_DQK), F32)),
        compiler_params=_params(("parallel", "parallel")),
        name="mlstm_prompt_state",
    )(p3, p3, a, g)


def _mlstm_step_kernel(q_ref, k_ref, v_ref, o_ref, gh_ref, ig_ref, fp_ref, c0_ref, n0_ref, m0_ref,
                       h_ref, c_ref, n_ref, m_ref):
    q = q_ref[0, 0] * (A_DQK ** -0.5)
    k = k_ref[0, 0]
    v = v_ref[0, 0]
    ig = ig_ref[0, 0]
    lf = _log_sigmoid(fp_ref[0, 0])
    c0 = c0_ref[0, 0]
    n0 = n0_ref[0, 0]
    m0 = m0_ref[0, 0]
    log_inter = lf + m0
    m_t = jnp.maximum(log_inter, ig)
    w_intra = jnp.exp(ig - m_t)
    w_inter = jnp.exp(log_inter - m_t)
    s = jnp.sum(q * k, axis=1, keepdims=True) * w_intra
    num = s * v + w_inter * jnp.sum(c0 * q, axis=1, keepdims=True)
    den = s + w_inter * jnp.sum(n0 * q, axis=1, keepdims=True)
    h = num / jnp.maximum(jnp.abs(den), jnp.exp(-m_t))
    h = h * lax.rsqrt(jnp.mean(h * h, axis=0, keepdims=True) + EPS) * gh_ref[0]
    h_ref[0, 0] = _sigmoid(o_ref[0, 0]) * h
    c_ref[0, 0] = w_inter * c0 + w_intra * (v * k)
    n_ref[0, 0] = w_inter * n0 + w_intra * k
    m_ref[0, 0] = m_t


def _mlstm_step(q, k, v_col, o_col, gh_col, ig, f_pre, c0, n0, m0):
    b = q.shape[0]
    h = A_HEADS
    row = pl.BlockSpec((1, 1, 1, A_DQK), lambda i, j: (i, j, 0, 0))
    colv = pl.BlockSpec((1, 1, A_DV, 1), lambda i, j: (i, j, 0, 0))
    one = pl.BlockSpec((1, 1, 1, 1), lambda i, j: (i, j, 0, 0))
    mat = pl.BlockSpec((1, 1, A_DV, A_DQK), lambda i, j: (i, j, 0, 0))
    return pl.pallas_call(
        _mlstm_step_kernel,
        grid=(b, h),
        in_specs=[row, row, colv, colv, pl.BlockSpec((1, A_DV, 1), lambda i, j: (j, 0, 0)),
                  one, one, mat, row, one],
        out_specs=(colv, mat, row, one),
        out_shape=(jax.ShapeDtypeStruct((b, h, A_DV, 1), F32),
                   jax.ShapeDtypeStruct((b, h, A_DV, A_DQK), F32),
                   jax.ShapeDtypeStruct((b, h, 1, A_DQK), F32),
                   jax.ShapeDtypeStruct((b, h, 1, 1), F32)),
        compiler_params=_params(("parallel", "parallel")),
        name="mlstm_step",
    )(q, k, v_col, o_col, gh_col, ig, f_pre, c0, n0, m0)


def _sort_key(x):
    x = jnp.where(x == 0.0, 0.0, x)
    bits = pltpu.bitcast(x, I32)
    return jnp.where(bits >= 0, bits, bits ^ jnp.int32(0x7FFFFFFF))


def _kth_largest_key(key, k):
    def count_ge(c):
        return jnp.sum(jnp.where(key >= c, 1.0, 0.0), axis=1, keepdims=True)

    kf = jnp.float32(k)
    thr = jnp.where(count_ge(jnp.int32(0)) >= kf, jnp.int32(0), jnp.int32(INT_MIN))
    thr = jnp.broadcast_to(thr, (key.shape[0], 1))

    def body(i, thr):
        cand = thr + lax.shift_left(jnp.int32(1), jnp.int32(30) - i)
        return jnp.where(count_ge(cand) >= kf, cand, thr)

    return lax.fori_loop(0, 31, body, thr)


def _tri_ones(n, dtype):
    r = lax.broadcasted_iota(I32, (n, n), 0)
    c = lax.broadcasted_iota(I32, (n, n), 1)
    return jnp.where(r <= c, 1.0, 0.0).astype(dtype)


def _select_topk(scores, visible, k):
    key = _sort_key(jnp.where(visible, scores, NEG_INF))
    thr = _kth_largest_key(key, k)
    gt = key > thr
    eq = jnp.logical_and(key == thr, visible)
    need = jnp.float32(k) - jnp.sum(jnp.where(gt, 1.0, 0.0), axis=1, keepdims=True)
    tri = _tri_ones(LANE, BF16)
    off = jnp.zeros((scores.shape[0], 1), F32)
    rank = []
    eqf = jnp.where(eq, 1.0, 0.0)
    for c in range(scores.shape[1] // LANE):
        p = _dot(eqf[:, c * LANE:(c + 1) * LANE].astype(BF16), tri) + off
        rank.append(p)
        off = p[:, LANE - 1:LANE]
    rank = jnp.concatenate(rank, axis=1)
    return jnp.logical_or(jnp.logical_and(gt, visible), jnp.logical_and(eq, rank <= need))


def _indexer_scores(qi_head, wi, ki):
    acc = None
    for h in range(IDX_HEADS):
        s = _dot_nt(qi_head(h).astype(BF16), ki) * (IDX_DIM ** -0.5)
        term = jnp.maximum(s, 0.0) * wi[:, h:h + 1]
        acc = term if acc is None else acc + term
    return acc * (IDX_HEADS ** -0.5)


def _dsa_prompt_kernel(q_ref, k_ref, v_ref, qi_ref, sm_ref, smq_ref, o_ref, *, tq, topk):
    j = pl.program_id(1)
    t = k_ref.shape[1]
    ki = sm_ref[0, :, _S_IK:_S_IK + IDX_DIM].astype(BF16)
    wi = smq_ref[0, :, _S_IW:_S_IW + IDX_HEADS]
    scores = _indexer_scores(lambda h: qi_ref[0, :, h * IDX_DIM:(h + 1) * IDX_DIM], wi, ki)
    qpos = lax.broadcasted_iota(I32, (tq, t), 0) + j * tq
    kpos = lax.broadcasted_iota(I32, (tq, t), 1)
    sel = _select_topk(scores, kpos <= qpos, topk)
    bias = jnp.where(sel, 0.0, NEG_INF)
    group = B_HEADS // B_KV_HEADS
    for n in range(B_KV_HEADS):
        kn = k_ref[0, :, n * B_DH:(n + 1) * B_DH].astype(BF16)
        vn = v_ref[0, :, n * B_DH:(n + 1) * B_DH].astype(BF16)
        for gidx in range(group):
            h = n * group + gidx
            qh = q_ref[0, :, h * B_DH:(h + 1) * B_DH].astype(BF16)
            s = _dot_nt(qh, kn) * (B_DH ** -0.5) + bias
            p = jnp.exp(s - jnp.max(s, axis=1, keepdims=True))
            l = jnp.sum(p, axis=1, keepdims=True)
            o_ref[0, :, h * B_DH:(h + 1) * B_DH] = (_dot(p.astype(BF16), vn) / l).astype(o_ref.dtype)


def _dsa_prompt(p3, small3):
    b, t, _ = p3.shape
    tq = 128
    topk = min(TOPK_MAX, t // 4)
    w_q, w_kv, w_qi = B_HEADS * B_DH, B_KV_HEADS * B_DH, IDX_HEADS * IDX_DIM
    return pl.pallas_call(
        functools.partial(_dsa_prompt_kernel, tq=tq, topk=topk),
        grid=(b, t // tq),
        in_specs=[pl.BlockSpec((1, tq, w_q), lambda i, j: (i, j, _BQ // w_q)),
                  pl.BlockSpec((1, t, w_kv), lambda i, j: (i, 0, _BK // w_kv)),
                  pl.BlockSpec((1, t, w_kv), lambda i, j: (i, 0, _BV // w_kv)),
                  pl.BlockSpec((1, tq, w_qi), lambda i, j: (i, j, _IQ // w_qi)),
                  pl.BlockSpec((1, t, LANE), lambda i, j: (i, 0, 0)),
                  pl.BlockSpec((1, tq, LANE), lambda i, j: (i, j, 0))],
        out_specs=pl.BlockSpec((1, tq, w_q), lambda i, j: (i, j, 0)),
        out_shape=jax.ShapeDtypeStruct((b, t, w_q), BF16),
        compiler_params=_params(("parallel", "parallel")),
        name="dsa_prompt",
    )(p3, p3, p3, p3, small3, small3)


def _dsa_step_scores_kernel(pt_ref, qi_ref, wi_ref, ki_ref, o_ref):
    del pt_ref
    qi = qi_ref[0].astype(BF16)
    s = _dot_nt(qi, ki_ref[0].astype(BF16)) * (IDX_DIM ** -0.5)
    o_ref[0] = jnp.sum(jnp.maximum(s, 0.0) * wi_ref[0], axis=0, keepdims=True) * (IDX_HEADS ** -0.5)


def _dsa_step_scores(page_table, qi, wi_col, cache_ki):
    b, n_pages = page_table.shape
    grid_spec = pltpu.PrefetchScalarGridSpec(
        num_scalar_prefetch=1,
        grid=(b, n_pages),
        in_specs=[pl.BlockSpec((1, IDX_HEADS, IDX_DIM), lambda i, p, pt: (i, 0, 0)),
                  pl.BlockSpec((1, IDX_HEADS, 1), lambda i, p, pt: (i, 0, 0)),
                  pl.BlockSpec((1, PAGE, IDX_DIM), lambda i, p, pt: (pt[i, p], 0, 0))],
        out_specs=pl.BlockSpec((1, 1, PAGE), lambda i, p, pt: (i, 0, p)),
    )
    return pl.pallas_call(
        _dsa_step_scores_kernel,
        grid_spec=grid_spec,
        out_shape=jax.ShapeDtypeStruct((b, 1, n_pages * PAGE), F32),
        compiler_params=_params(("parallel", "arbitrary")),
        name="dsa_step_scores",
    )(page_table, qi, wi_col, cache_ki)


def _dsa_step_select_kernel(sc_ref, self_ref, sel_ref, selself_ref, *, topk):
    sc = sc_ref[...]
    sself = self_ref[...]
    key = _sort_key(sc)
    key_self = _sort_key(sself)
    kf = jnp.float32(topk)

    def count_ge(c):
        return (jnp.sum(jnp.where(key >= c, 1.0, 0.0), axis=1, keepdims=True)
                + jnp.where(key_self >= c, 1.0, 0.0))

    thr = jnp.where(count_ge(jnp.int32(0)) >= kf, jnp.int32(0), jnp.int32(INT_MIN))

    def body(i, thr):
        cand = thr + lax.shift_left(jnp.int32(1), jnp.int32(30) - i)
        return jnp.where(count_ge(cand) >= kf, cand, thr)

    thr = lax.fori_loop(0, 31, body, thr)
    gt = key > thr
    eq = key == thr
    n_gt = jnp.sum(jnp.where(gt, 1.0, 0.0), axis=1, keepdims=True) + jnp.where(key_self > thr, 1.0, 0.0)
    need = kf - n_gt
    rank = _scan_lanes(jnp.where(eq, 1.0, 0.0), jnp.add, 0.0)
    sel_ref[...] = jnp.where(jnp.logical_or(gt, jnp.logical_and(eq, rank <= need)), 1.0, 0.0)
    n_eq = rank[:, -1:]
    selself_ref[...] = jnp.where(
        jnp.logical_or(key_self > thr, jnp.logical_and(key_self == thr, n_eq < need)), 1.0, 0.0)


def _dsa_step_select(scores, self_score, topk):
    b, s = scores.shape
    return pl.pallas_call(
        functools.partial(_dsa_step_select_kernel, topk=topk),
        out_shape=(jax.ShapeDtypeStruct((b, s), F32), jax.ShapeDtypeStruct((b, 1), F32)),
        compiler_params=pltpu.CompilerParams(vmem_limit_bytes=VMEM_LIMIT_BYTES),
        name="dsa_step_select",
    )(scores, self_score)


def _dsa_step_attend_kernel(pt_ref, q_ref, sel_ref, selself_ref, kn_ref, vn_ref, k_ref, v_ref, o_ref,
                            m_sc, l_sc, acc_sc):
    del pt_ref
    p_idx = pl.program_id(1)
    n_pages = pl.num_programs(1)
    group = B_HEADS // B_KV_HEADS

    @pl.when(p_idx == 0)
    def _():
        m_sc[...] = jnp.full_like(m_sc, NEG_INF)
        l_sc[...] = jnp.zeros_like(l_sc)
        acc_sc[...] = jnp.zeros_like(acc_sc)

    q = q_ref[0].astype(BF16)
    kv_of_row = lax.broadcasted_iota(I32, (B_HEADS, 1), 0) // group

    def update(k_blk, v_blk, sel_row):
        s = jnp.zeros((B_HEADS, sel_row.shape[1]), F32)
        for n in range(B_KV_HEADS):
            kn = k_blk[:, n * B_DH:(n + 1) * B_DH].astype(BF16)
            s = jnp.where(kv_of_row == n, _dot_nt(q, kn), s)
        s = s * (B_DH ** -0.5) + jnp.where(sel_row > 0.0, 0.0, NEG_INF)
        m_old = m_sc[...]
        m_new = jnp.maximum(m_old, jnp.max(s, axis=1, keepdims=True))
        m_safe = jnp.where(m_new == NEG_INF, 0.0, m_new)
        alpha = jnp.exp(m_old - m_safe)
        p = jnp.exp(s - m_safe)
        pb = p.astype(BF16)
        pv = jnp.zeros((B_HEADS, B_DH), F32)
        for n in range(B_KV_HEADS):
            vn = v_blk[:, n * B_DH:(n + 1) * B_DH].astype(BF16)
            pv = jnp.where(kv_of_row == n, _dot(pb, vn), pv)
        l_sc[...] = alpha * l_sc[...] + jnp.sum(p, axis=1, keepdims=True)
        acc_sc[...] = alpha * acc_sc[...] + pv
        m_sc[...] = m_new

    update(k_ref.at[0], v_ref.at[0], sel_ref[0])

    @pl.when(p_idx == n_pages - 1)
    def _():
        update(kn_ref.at[0], vn_ref.at[0], selself_ref[0])
        o_ref[0] = acc_sc[...] / l_sc[...]


def _dsa_step_attend(page_table, q, sel, sel_self, k_new, v_new, cache_k, cache_v):
    b, n_pages = page_table.shape
    w_kv = B_KV_HEADS * B_DH
    grid_spec = pltpu.PrefetchScalarGridSpec(
        num_scalar_prefetch=1,
        grid=(b, n_pages),
        in_specs=[pl.BlockSpec((1, B_HEADS, B_DH), lambda i, p, pt: (i, 0, 0)),
                  pl.BlockSpec((1, 1, PAGE), lambda i, p, pt: (i, 0, p)),
                  pl.BlockSpec((1, 1, PAGE), lambda i, p, pt: (i, 0, 0)),
                  pl.BlockSpec((1, PAGE, w_kv), lambda i, p, pt: (i, 0, 0)),
                  pl.BlockSpec((1, PAGE, w_kv), lambda i, p, pt: (i, 0, 0)),
                  pl.BlockSpec((1, PAGE, w_kv), lambda i, p, pt: (pt[i, p], 0, 0)),
                  pl.BlockSpec((1, PAGE, w_kv), lambda i, p, pt: (pt[i, p], 0, 0))],
        out_specs=pl.BlockSpec((1, B_HEADS, B_DH), lambda i, p, pt: (i, 0, 0)),
        scratch_shapes=[pltpu.VMEM((B_HEADS, 1), F32), pltpu.VMEM((B_HEADS, 1), F32),
                        pltpu.VMEM((B_HEADS, B_DH), F32)],
    )
    return pl.pallas_call(
        _dsa_step_attend_kernel,
        grid_spec=grid_spec,
        out_shape=jax.ShapeDtypeStruct((b, B_HEADS, B_DH), F32),
        compiler_params=_params(("parallel", "arbitrary")),
        name="dsa_step_attend",
    )(page_table, q, sel, sel_self, k_new, v_new, cache_k, cache_v)


def _layernorm(x, g, b):
    xc = x - jnp.mean(x, axis=-1, keepdims=True)
    return xc * lax.rsqrt(jnp.mean(xc * xc, axis=-1, keepdims=True) + EPS) * g + b


def _odd_mix_kernel(u_ref, v_ref, a_ref, gt_ref, gv_ref, bv_ref, wsp_ref, bsp_ref, wdc_ref, bdc_ref,
                    gdn_ref, bdn_ref, o_ref, tail_ref, xbuf, *, tt, width, halo):
    j = pl.program_id(1)
    nj = pl.num_programs(1)

    @pl.when(j == 0)
    def _():
        xbuf[0:halo, :] = jnp.zeros((halo, width), F32)

    vn = _layernorm(v_ref[0], gv_ref[...], bv_ref[...]).astype(BF16)
    r = lax.broadcasted_iota(I32, (tt, tt), 0)
    c = lax.broadcasted_iota(I32, (tt, tt), 1)
    gw = width // C_GROUPS
    for gi in range(C_GROUPS):
        wm = jnp.where(c <= r, wsp_ref[gi], 0.0).astype(BF16)
        sg = _dot(wm, vn[:, gi * gw:(gi + 1) * gw]) + bsp_ref[gi]
        o_ref[0, :, gi * gw:(gi + 1) * gw] = (u_ref[0, :, gi * gw:(gi + 1) * gw] * sg).astype(o_ref.dtype)

    glu = a_ref[0] * _sigmoid(gt_ref[0])
    xbuf[halo:halo + tt, :] = glu
    wdc = wdc_ref[...]
    dc = jnp.zeros((tt, width), F32) + bdc_ref[...]
    base = halo - (D_CONV - 1)
    for kk in range(D_CONV):
        dc = dc + wdc[kk:kk + 1, :] * xbuf[base + kk:base + kk + tt, :]
    d = _layernorm(dc, gdn_ref[...], bdn_ref[...])
    o_ref[0, :, width:2 * width] = (d * _sigmoid(d)).astype(o_ref.dtype)

    xbuf[0:halo, :] = glu[tt - halo:tt, :]

    @pl.when(j == nj - 1)
    def _():
        tail_ref[0] = glu[tt - halo:tt, :]


def _odd_mix(p3, g_v, b_v, w_sp, b_sp, w_dc, b_dc, g_dn, b_dn):
    b, t, w4 = p3.shape
    width = w4 // 4
    tt = C_CHUNK
    halo = 32
    vec = lambda x: x.reshape(1, width)
    vspec = pl.BlockSpec((1, width), lambda i, j: (0, 0))
    return pl.pallas_call(
        functools.partial(_odd_mix_kernel, tt=tt, width=width, halo=halo),
        grid=(b, t // tt),
        in_specs=[pl.BlockSpec((1, tt, width), lambda i, j: (i, j, 0)),
                  pl.BlockSpec((1, tt, width), lambda i, j: (i, j, 1)),
                  pl.BlockSpec((1, tt, width), lambda i, j: (i, j, 2)),
                  pl.BlockSpec((1, tt, width), lambda i, j: (i, j, 3)),
                  vspec, vspec,
                  pl.BlockSpec((C_GROUPS, tt, tt), lambda i, j: (0, 0, 0)),
                  pl.BlockSpec((C_GROUPS, tt, 1), lambda i, j: (0, 0, 0)),
                  pl.BlockSpec((D_CONV, width), lambda i, j: (0, 0)),
                  vspec, vspec, vspec],
        out_specs=(pl.BlockSpec((1, tt, 2 * width), lambda i, j: (i, j, 0)),
                   pl.BlockSpec((1, halo, width), lambda i, j: (i, 0, 0))),
        out_shape=(jax.ShapeDtypeStruct((b, t, 2 * width), BF16),
                   jax.ShapeDtypeStruct((b, halo, width), F32)),
        scratch_shapes=[pltpu.VMEM((halo + tt, width), F32)],
        compiler_params=_params(("parallel", "arbitrary")),
        name="odd_mix",
    )(p3, p3, p3, p3, vec(g_v), vec(b_v), w_sp, b_sp.reshape(C_GROUPS, tt, 1), w_dc, vec(b_dc),
      vec(g_dn), vec(b_dn))


def _odd_mix_step_kernel(p_ref, buf_ref, gv_ref, bv_ref, wsp_ref, bsp_ref, wdc_ref, bdc_ref,
                         gdn_ref, bdn_ref, o_ref, vn_ref, glu_ref, *, width):
    u = p_ref[:, 0:width]
    v = p_ref[:, width:2 * width]
    a = p_ref[:, 2 * width:3 * width]
    gt = p_ref[:, 3 * width:4 * width]
    vn = _layernorm(v, gv_ref[...], bv_ref[...])
    vn_ref[...] = vn
    o_ref[:, 0:width] = u * (wsp_ref[...] * vn + bsp_ref[...])
    glu = a * _sigmoid(gt)
    glu_ref[...] = glu
    wdc = wdc_ref[...]
    dc = wdc[D_CONV - 1:D_CONV, :] * glu + bdc_ref[...]
    for kk in range(D_CONV - 1):
        dc = dc + wdc[kk:kk + 1, :] * buf_ref[kk]
    d = _layernorm(dc, gdn_ref[...], bdn_ref[...])
    o_ref[:, width:2 * width] = d * _sigmoid(d)


def _odd_mix_step(p, buf, g_v, b_v, w_sp_vec, b_sp_vec, w_dc, b_dc, g_dn, b_dn):
    b, w4 = p.shape
    width = w4 // 4
    vec = lambda x: x.reshape(1, width)
    return pl.pallas_call(
        functools.partial(_odd_mix_step_kernel, width=width),
        out_shape=(jax.ShapeDtypeStruct((b, 2 * width), F32), jax.ShapeDtypeStruct((b, width), F32),
                   jax.ShapeDtypeStruct((b, width), F32)),
        compiler_params=pltpu.CompilerParams(vmem_limit_bytes=VMEM_LIMIT_BYTES),
        name="odd_mix_step",
    )(p, buf, vec(g_v), vec(b_v), vec(w_sp_vec), vec(b_sp_vec), w_dc, vec(b_dc), vec(g_dn), vec(b_dn))


def _pad_rows(x, rows):
    return jnp.pad(x, ((0, rows - x.shape[0]), (0, 0)))


def _split_even_weights(w):
    sizes = (1024, 1024, 2048, 2048, 8, 8, 2048, 512, 512, 1024, 16, 64)
    offs = [0]
    for s in sizes:
        offs.append(offs[-1] + s)
    col = lambda i: w[:, offs[i]:offs[i + 1]]
    return jnp.concatenate([col(0), col(1), col(2), col(3), col(6), col(7), col(8), col(9),
                            col(4), col(5), col(10), col(11),
                            jnp.zeros((w.shape[0], LANE - 96), w.dtype)], axis=1)


def _conv_ffn(h_p, h_s, tp, g, layer, w_up, w_fc, b_fc, w_down_b, buf_s):
    f = w_up.shape[2] // 2
    nb = buf_s.shape[0]
    xn = _rmsnorm(h_p, g, BF16)
    xs = _rmsnorm(h_s, g, BF16)
    act, tail_g, tail_v, hs_g, hs_v = _ffn_up(xn, xs, w_up, w_fc, b_fc.reshape(b_fc.shape[0], 1, 2 * f),
                                              layer, tp)
    h_p_new = _matmul_xres(act, w_down_b, layer, h_p)
    ff_p = jnp.concatenate([tail_g, tail_v], axis=-1)[:, 8 - (FF_CONV - 1):, :]
    act_s = _ffn_act_step(hs_g[:nb], hs_v[:nb], buf_s.transpose(1, 0, 2), w_fc[layer], b_fc[layer].reshape(1, 2 * f))
    h_s_new = _matmul(_pad_rows(act_s.astype(BF16), h_s.shape[0]), w_down_b, layer, res=h_s, tn=1024, tk=f // 2)
    ff_s = jnp.concatenate([buf_s[:, 1:, :], jnp.concatenate([hs_g[:nb], hs_v[:nb]], axis=-1)[:, None, :]], axis=1)
    return h_p_new, h_s_new, ff_p, ff_s


def kernel(x_prompt, x_sample, cache_k, cache_v, cache_kidx, state_mlstm_C, state_mlstm_n, state_mlstm_m,
           state_conv_d, state_ffn_conv, page_table, w_norm_mix, w_norm_ffn, w_in_even, b_gate_even,
           g_head_even, w_out_even, w_in_odd, g_vnorm, b_vnorm, w_spatial, b_spatial, w_dconv, b_dconv,
           g_dnorm, b_dnorm, w_out_odd, w_up, w_fconv, b_fconv, w_down, w_norm_final):
    bp, tp, d = x_prompt.shape
    bs = x_sample.shape[0]
    assert x_sample.shape[1] == 1
    ms = 16
    h_p = x_prompt.reshape(bp * tp, d)
    h_s = _pad_rows(x_sample.reshape(bs, d), ms)
    n_pages = page_table.shape[1]
    past = n_pages * PAGE

    w_even = _split_even_weights(w_in_even[0])[None]
    w_down_b = w_down.astype(BF16)
    b_i = b_gate_even[0, :A_HEADS]
    b_f = b_gate_even[0, A_HEADS:]
    g_head = g_head_even[0]

    xn = _rmsnorm(h_p, w_norm_mix[0], BF16)
    xs = _rmsnorm(h_s, w_norm_mix[0], BF16)
    proj, proj_s = _matmul_ws(xn, xs, w_even, 0, 0, _N_MAIN)
    small, small_s = _matmul_ws(xn, xs, w_even, 0, _N_MAIN, LANE, tn=LANE)
    proj_s = proj_s[:bs]
    small_s = small_s[:bs]
    p3 = proj.reshape(bp, tp, _N_MAIN)
    small3 = small.reshape(bp, tp, LANE)

    def heads_t(x):
        return x.transpose(0, 2, 1).reshape(bp * A_HEADS, tp)

    a_g, g_g, m_g = _mlstm_gates(heads_t(small3[:, :, _S_AI:_S_AI + A_HEADS]),
                                 heads_t(small3[:, :, _S_AF:_S_AF + A_HEADS]),
                                 jnp.tile(b_i, bp).reshape(-1, 1), jnp.tile(b_f, bp).reshape(-1, 1))
    a4 = a_g.reshape(bp, A_HEADS, 1, tp)
    g4 = g_g.reshape(bp, A_HEADS, 1, tp)
    m4 = m_g.reshape(bp, A_HEADS, 1, tp)
    h_a = _mlstm_prompt(p3, a4, g4, m4, g_head.reshape(A_HEADS, 1, A_DV))
    c_p, n_p = _mlstm_prompt_state(p3, a4, g4)
    m_p = m_g.reshape(bp, A_HEADS, tp)[:, :, -1]
    h_b = _dsa_prompt(p3, small3)
    mix = jnp.concatenate([h_a, h_b], axis=-1).reshape(bp * tp, -1)

    k_p = p3[:, :, _BK:_BK + B_KV_HEADS * B_DH].reshape(bp, tp, B_KV_HEADS, B_DH)
    v_p = p3[:, :, _BV:_BV + B_KV_HEADS * B_DH].reshape(bp, tp, B_KV_HEADS, B_DH)
    ki_p = small3[:, :, _S_IK:_S_IK + IDX_DIM]

    q_s = proj_s[:, _AQ:_AQ + 1024].reshape(bs, A_HEADS, 1, A_DQK)
    kk_s = proj_s[:, _AK:_AK + 1024].reshape(bs, A_HEADS, 1, A_DQK)
    v_s = proj_s[:, _AV:_AV + 2048].reshape(bs, A_HEADS, A_DV, 1)
    o_s = proj_s[:, _AO:_AO + 2048].reshape(bs, A_HEADS, A_DV, 1)
    ig_s = (small_s[:, _S_AI:_S_AI + A_HEADS] + b_i).reshape(bs, A_HEADS, 1, 1)
    fp_s = (small_s[:, _S_AF:_S_AF + A_HEADS] + b_f).reshape(bs, A_HEADS, 1, 1)
    hcol_s, c_s, n_s, m_s = _mlstm_step(
        q_s, kk_s, v_s, o_s, g_head.reshape(A_HEADS, A_DV, 1), ig_s, fp_s,
        state_mlstm_C[0], state_mlstm_n[0].reshape(bs, A_HEADS, 1, A_DQK),
        state_mlstm_m[0].reshape(bs, A_HEADS, 1, 1))
    h_a_s = hcol_s.reshape(bs, A_HEADS * A_DV)

    bq_s = proj_s[:, _BQ:_BQ + 2048].reshape(bs, B_HEADS, B_DH)
    bk_s = proj_s[:, _BK:_BK + 512]
    bv_s = proj_s[:, _BV:_BV + 512]
    iq_s = proj_s[:, _IQ:_IQ + 1024].reshape(bs, IDX_HEADS, IDX_DIM)
    iw_s = small_s[:, _S_IW:_S_IW + IDX_HEADS]
    ik_s = small_s[:, _S_IK:_S_IK + IDX_DIM]
    sc_past = _dsa_step_scores(page_table, iq_s, iw_s.reshape(bs, IDX_HEADS, 1), cache_kidx[0])
    own_page = lambda x: jnp.pad(x[:, None, :], ((0, 0), (0, PAGE - 1), (0, 0)))
    sc_self = _dsa_step_scores(jnp.arange(bs, dtype=I32).reshape(bs, 1), iq_s, iw_s.reshape(bs, IDX_HEADS, 1),
                               own_page(ik_s))[:, 0, 0:1]
    topk_s = min(TOPK_MAX, (past + 1) // 4)
    sel, sel_self = _dsa_step_select(sc_past.reshape(bs, past), sc_self, topk_s)
    w_kv = B_KV_HEADS * B_DH
    h_b_s = _dsa_step_attend(page_table, bq_s, sel.reshape(bs, 1, past),
                             jnp.pad(sel_self, ((0, 0), (0, PAGE - 1))).reshape(bs, 1, PAGE),
                             own_page(bk_s), own_page(bv_s),
                             cache_k[0].reshape(-1, PAGE, w_kv), cache_v[0].reshape(-1, PAGE, w_kv))
    mix_s = jnp.concatenate([h_a_s, h_b_s.reshape(bs, B_HEADS * B_DH)], axis=-1).astype(BF16)
    h_p, h_s = _matmul_ws(mix, _pad_rows(mix_s, ms), w_out_even, 0, 0, d, res=h_p, res_s=h_s)

    h_p, h_s, ff_p0, ff_s0 = _conv_ffn(h_p, h_s, tp, w_norm_ffn[0], 0, w_up, w_fconv, b_fconv, w_down_b,
                                       state_ffn_conv[0])

    width = w_in_odd.shape[2] // 4
    xn = _rmsnorm(h_p, w_norm_mix[1], BF16)
    xs = _rmsnorm(h_s, w_norm_mix[1], BF16)
    proj1, proj1_s = _matmul_ws(xn, xs, w_in_odd, 0, 0, 4 * width)
    proj1_s = proj1_s[:bs]
    mix1, tail = _odd_mix(proj1.reshape(bp, tp, 4 * width), g_vnorm[0], b_vnorm[0], w_spatial[0], b_spatial[0],
                          w_dconv[0], b_dconv[0], g_dnorm[0], b_dnorm[0])
    cd_p = tail[:, tail.shape[1] - (D_CONV - 1):, :]
    gw = width // C_GROUPS
    mix1_s, vn_s, glu_s = _odd_mix_step(
        proj1_s, state_conv_d[0].transpose(1, 0, 2), g_vnorm[0], b_vnorm[0],
        jnp.repeat(w_spatial[0, :, 0, 0], gw), jnp.repeat(b_spatial[0, :, 0], gw),
        w_dconv[0], b_dconv[0], g_dnorm[0], b_dnorm[0])
    h_p, h_s = _matmul_ws(mix1.reshape(bp * tp, 2 * width), _pad_rows(mix1_s.astype(BF16), ms), w_out_odd, 0, 0, d,
                          res=h_p, res_s=h_s)
    cd_s = jnp.concatenate([state_conv_d[0][:, 1:, :], glu_s[:, None, :]], axis=1)

    h_p, h_s, ff_p1, ff_s1 = _conv_ffn(h_p, h_s, tp, w_norm_ffn[1], 1, w_up, w_fconv, b_fconv, w_down_b,
                                       state_ffn_conv[1])

    y_p = _rmsnorm(h_p, w_norm_final, F32).reshape(bp, tp, d)
    y_s = _rmsnorm(h_s, w_norm_final, F32)[:bs].reshape(bs, 1, d)

    return (y_p, y_s,
            k_p[None], v_p[None], ki_p[None],
            bk_s.reshape(1, bs, 1, B_KV_HEADS, B_DH), bv_s.reshape(1, bs, 1, B_KV_HEADS, B_DH),
            ik_s.reshape(1, bs, 1, IDX_DIM),
            c_p[None], n_p.reshape(1, bp, A_HEADS, A_DQK), m_p[None],
            c_s[None], n_s.reshape(1, bs, A_HEADS, A_DQK), m_s.reshape(1, bs, A_HEADS),
            vn_s.reshape(1, bs, 1, width),
            cd_p[None], cd_s[None],
            jnp.stack([ff_p0, ff_p1]), jnp.stack([ff_s0, ff_s1]))
```

```python
import functools
import math

import jax
import jax.numpy as jnp
from jax import lax
from jax.experimental import pallas as pl
from jax.experimental.pallas import tpu as pltpu

F32 = jnp.float32
BF16 = jnp.bfloat16
I32 = jnp.int32
EPS = 1e-6
NEG_INF = float("-inf")
INT_MIN = -(2 ** 31)

A_HEADS = 8
A_DQK = 128
A_DV = 256
B_HEADS = 16
B_KV_HEADS = 4
B_DH = 128
IDX_HEADS = 16
IDX_DIM = 64
TOPK_MAX = 256
PAGE = 128
C_GROUPS = 8
C_CHUNK = 128
D_CONV = 31
FF_CONV = 3
LANE = 128
VMEM_LIMIT_BYTES = 48 * 1024 * 1024
WS_VMEM_LIMIT_BYTES = 56 * 1024 * 1024
FFN_ROW_CHUNK = 1024
DSA_PAGES_PER_STEP = 8

_AQ, _AK, _AV, _AO = 0, 1024, 2048, 4096
_BQ, _BK, _BV, _IQ = 6144, 8192, 8704, 9216
_N_MAIN = 10240
_S_AI, _S_AF, _S_IW, _S_IK = 0, 8, 16, 32


def _params(sem):
    return pltpu.CompilerParams(dimension_semantics=sem, vmem_limit_bytes=VMEM_LIMIT_BYTES)


def _dot(a, b):
    return jnp.dot(a, b, preferred_element_type=F32)


def _dot_nt(a, b):
    return lax.dot_general(a, b, (((1,), (1,)), ((), ())), preferred_element_type=F32)


def _sigmoid(x):
    return 1.0 / (1.0 + jnp.exp(-x))


def _col(row):
    n = row.shape[1]
    r = lax.broadcasted_iota(I32, (n, n), 0)
    c = lax.broadcasted_iota(I32, (n, n), 1)
    return jnp.sum(jnp.where(r == c, row, 0.0), axis=1, keepdims=True)


def _rms_kernel(x_ref, g_ref, o_ref):
    x = x_ref[...]
    y = x * lax.rsqrt(jnp.mean(x * x, axis=-1, keepdims=True) + EPS)
    o_ref[...] = (y * g_ref[...]).astype(o_ref.dtype)


def _rmsnorm(x, g, out_dtype):
    m, d = x.shape
    tm = min(m, 256)
    return pl.pallas_call(
        _rms_kernel,
        grid=(m // tm,),
        in_specs=[pl.BlockSpec((tm, d), lambda i: (i, 0)), pl.BlockSpec((1, d), lambda i: (0, 0))],
        out_specs=pl.BlockSpec((tm, d), lambda i: (i, 0)),
        out_shape=jax.ShapeDtypeStruct((m, d), out_dtype),
        compiler_params=_params(("parallel",)),
        name="rmsnorm",
    )(x, g.reshape(1, d))


def _mm_kernel(*refs, nk, has_res):
    if has_res:
        x_ref, w_ref, r_ref, o_ref = refs[:4]
        scratch = refs[4:]
    else:
        x_ref, w_ref, o_ref = refs[:3]
        r_ref = None
        scratch = refs[3:]
    if nk == 1:
        acc = _dot(x_ref[...], w_ref[...])
        if has_res:
            acc = acc + r_ref[...]
        o_ref[...] = acc.astype(o_ref.dtype)
        return
    acc_ref, = scratch
    k = pl.program_id(2)

    @pl.when(k == 0)
    def _():
        acc_ref[...] = jnp.zeros_like(acc_ref)

    acc_ref[...] += _dot(x_ref[...], w_ref[...])

    @pl.when(k == nk - 1)
    def _():
        acc = acc_ref[...]
        if has_res:
            acc = acc + r_ref[...]
        o_ref[...] = acc.astype(o_ref.dtype)


def _matmul(x, w, layer=0, res=None, out_dtype=F32, tm=1024, tn=512, tk=None):
    m, kdim = x.shape
    n = w.shape[2]
    tm = min(tm, m)
    tn = min(tn, n)
    tk = kdim if tk is None else tk
    assert m % tm == 0 and n % tn == 0 and kdim % tk == 0
    nk = kdim // tk
    in_specs = [pl.BlockSpec((tm, tk), lambda i, j, k: (i, k)),
                pl.BlockSpec((None, tk, tn), lambda i, j, k: (layer, k, j))]
    args = [x, w]
    if res is not None:
        in_specs.append(pl.BlockSpec((tm, tn), lambda i, j, k: (i, j)))
        args.append(res)
    return pl.pallas_call(
        functools.partial(_mm_kernel, nk=nk, has_res=res is not None),
        grid=(m // tm, n // tn, nk),
        in_specs=in_specs,
        out_specs=pl.BlockSpec((tm, tn), lambda i, j, k: (i, j)),
        out_shape=jax.ShapeDtypeStruct((m, n), out_dtype),
        scratch_shapes=[pltpu.VMEM((tm, tn), F32)] if nk > 1 else [],
        compiler_params=_params(("parallel", "parallel", "arbitrary")),
        name="matmul",
    )(*args)


def _mm_ws_kernel(*refs, has_res, emit_bf16):
    refs = list(refs)
    x_ref, xs_ref, w_ref = refs[:3]
    r_ref, rs_ref = refs[3:5] if has_res else (None, None)
    outs = refs[5:] if has_res else refs[3:]
    o_ref, os_ref = outs[:2]
    ob_ref = outs[2] if emit_bf16 else None
    wb_ref = outs[-1]

    @pl.when(pl.program_id(1) == 0)
    def _():
        wb_ref[...] = w_ref[...].astype(BF16)
        acc_s = _dot(xs_ref[...], wb_ref[...])
        os_ref[...] = acc_s + rs_ref[...] if has_res else acc_s

    acc = _dot(x_ref[...], wb_ref[...])
    if has_res:
        acc = acc + r_ref[...]
    o_ref[...] = acc
    if emit_bf16:
        ob_ref[...] = acc.astype(BF16)


def _matmul_ws(x, xs, w, layer, col0, n, res=None, res_s=None, emit_bf16=False, tm=1024, tn=512):
    m, kdim = x.shape
    ms = xs.shape[0]
    assert m % tm == 0 and n % tn == 0 and col0 % tn == 0
    c0 = col0 // tn
    has_res = res is not None
    out_specs = [pl.BlockSpec((tm, tn), lambda j, i: (i, j)), pl.BlockSpec((ms, tn), lambda j, i: (0, j))]
    out_shape = [jax.ShapeDtypeStruct((m, n), F32), jax.ShapeDtypeStruct((ms, n), F32)]
    if emit_bf16:
        out_specs.append(pl.BlockSpec((tm, tn), lambda j, i: (i, j)))
        out_shape.append(jax.ShapeDtypeStruct((m, n), BF16))
    in_specs = [pl.BlockSpec((tm, kdim), lambda j, i: (i, 0)),
                pl.BlockSpec((ms, kdim), lambda j, i: (0, 0)),
                pl.BlockSpec((None, kdim, tn), lambda j, i: (layer, 0, c0 + j))]
    args = [x, xs, w]
    if has_res:
        in_specs += [pl.BlockSpec((tm, tn), lambda j, i: (i, j)), pl.BlockSpec((ms, tn), lambda j, i: (0, j))]
        args += [res, res_s]
    return pl.pallas_call(
        functools.partial(_mm_ws_kernel, has_res=has_res, emit_bf16=emit_bf16),
        grid=(n // tn, m // tm),
        in_specs=in_specs,
        out_specs=tuple(out_specs),
        out_shape=tuple(out_shape),
        scratch_shapes=[pltpu.VMEM((kdim, tn), BF16)],
        compiler_params=pltpu.CompilerParams(dimension_semantics=("parallel", "arbitrary"),
                                             vmem_limit_bytes=WS_VMEM_LIMIT_BYTES),
        name="matmul_ws",
    )(*args)


def _matmul_xres(x, w, layer, res, tm=512, tn=256):
    m, kdim = x.shape
    n = w.shape[2]
    assert m % tm == 0 and n % tn == 0
    return pl.pallas_call(
        functools.partial(_mm_kernel, nk=1, has_res=True),
        grid=(m // tm, n // tn),
        in_specs=[pl.BlockSpec((tm, kdim), lambda i, j: (i, 0)),
                  pl.BlockSpec((None, kdim, tn), lambda i, j: (layer, 0, j)),
                  pl.BlockSpec((tm, tn), lambda i, j: (i, j))],
        out_specs=pl.BlockSpec((tm, tn), lambda i, j: (i, j)),
        out_shape=jax.ShapeDtypeStruct((m, n), F32),
        compiler_params=_params(("parallel", "arbitrary")),
        name="matmul_xres",
    )(x, w, res)


def _conv_rows(h, prev, w, b):
    def taps(x, shifted):
        y = w[FF_CONV - 1:FF_CONV, :] * x + b
        for s in range(1, FF_CONV):
            y = y + w[FF_CONV - 1 - s:FF_CONV - s, :] * shifted(s)
        return y

    bulk = taps(h, lambda s: pltpu.roll(h, s, 0))
    h8 = h[0:8, :]
    row = lax.broadcasted_iota(I32, h8.shape, 0)

    def head_shift(s):
        x = pltpu.roll(h8, s, 0)
        for r in range(s):
            x = jnp.where(row == r, prev[8 - s + r:8 - s + r + 1, :], x)
        return x

    return jnp.concatenate([taps(h8, head_shift), bulk[8:, :]], axis=0)


def _ffn_up_kernel(x_ref, xs_ref, wg_ref, wv_ref, cg_ref, cv_ref, bg_ref, bv_ref,
                   act_ref, tg_ref, tv_ref, sg_ref, sv_ref, wb_ref, carry_ref, *, tn, tiles_per_seq):
    i = pl.program_id(1)

    @pl.when(i == 0)
    def _():
        wb_ref[:, 0:tn] = wg_ref[...].astype(BF16)
        wb_ref[:, tn:2 * tn] = wv_ref[...].astype(BF16)
        hs = _dot(xs_ref[...], wb_ref[...])
        sg_ref[...] = hs[:, 0:tn]
        sv_ref[...] = hs[:, tn:2 * tn]

    @pl.when(i % tiles_per_seq == 0)
    def _():
        carry_ref[...] = jnp.zeros_like(carry_ref)

    tm = x_ref.shape[0]
    rc = min(tm, FFN_ROW_CHUNK)
    prev = carry_ref[...]
    for r in range(tm // rc):
        h = _dot(x_ref[r * rc:(r + 1) * rc, :], wb_ref[...])
        g = _conv_rows(h[:, 0:tn], prev[:, 0:tn], cg_ref[...], bg_ref[...])
        v = _conv_rows(h[:, tn:2 * tn], prev[:, tn:2 * tn], cv_ref[...], bv_ref[...])
        act_ref[r * rc:(r + 1) * rc, :] = (g * _sigmoid(g) * v).astype(act_ref.dtype)
        prev = h[rc - 8:rc, :]
    carry_ref[...] = prev

    @pl.when(i % tiles_per_seq == tiles_per_seq - 1)
    def _():
        tg_ref[0] = prev[:, 0:tn]
        tv_ref[0] = prev[:, tn:2 * tn]


def _ffn_up(x, xs, w_up, w_fc, b_fc, layer, seq_len, tm=1024, tn=256):
    m, kdim = x.shape
    ms = xs.shape[0]
    f = w_up.shape[2] // 2
    nf = f // tn
    tm = min(tm, seq_len)
    assert m % tm == 0 and seq_len % tm == 0 and f % tn == 0
    tiles_per_seq = seq_len // tm
    nseq = m // seq_len
    return pl.pallas_call(
        functools.partial(_ffn_up_kernel, tn=tn, tiles_per_seq=tiles_per_seq),
        grid=(nf, m // tm),
        in_specs=[pl.BlockSpec((tm, kdim), lambda j, i: (i, 0)),
                  pl.BlockSpec((ms, kdim), lambda j, i: (0, 0)),
                  pl.BlockSpec((None, kdim, tn), lambda j, i: (layer, 0, j)),
                  pl.BlockSpec((None, kdim, tn), lambda j, i: (layer, 0, j + nf)),
                  pl.BlockSpec((None, FF_CONV, tn), lambda j, i: (layer, 0, j)),
                  pl.BlockSpec((None, FF_CONV, tn), lambda j, i: (layer, 0, j + nf)),
                  pl.BlockSpec((None, 1, tn), lambda j, i: (layer, 0, j)),
                  pl.BlockSpec((None, 1, tn), lambda j, i: (layer, 0, j + nf))],
        out_specs=(pl.BlockSpec((tm, tn), lambda j, i: (i, j)),
                   pl.BlockSpec((1, 8, tn), lambda j, i: (i // tiles_per_seq, 0, j)),
                   pl.BlockSpec((1, 8, tn), lambda j, i: (i // tiles_per_seq, 0, j)),
                   pl.BlockSpec((ms, tn), lambda j, i: (0, j)),
                   pl.BlockSpec((ms, tn), lambda j, i: (0, j))),
        out_shape=(jax.ShapeDtypeStruct((m, f), BF16),
                   jax.ShapeDtypeStruct((nseq, 8, f), F32), jax.ShapeDtypeStruct((nseq, 8, f), F32),
                   jax.ShapeDtypeStruct((ms, f), F32), jax.ShapeDtypeStruct((ms, f), F32)),
        scratch_shapes=[pltpu.VMEM((kdim, 2 * tn), BF16), pltpu.VMEM((8, 2 * tn), F32)],
        compiler_params=pltpu.CompilerParams(dimension_semantics=("parallel", "arbitrary"),
                                             vmem_limit_bytes=WS_VMEM_LIMIT_BYTES),
        name="ffn_up",
    )(x, xs, w_up, w_up, w_fc, w_fc, b_fc, b_fc)


def _ffn_act_step_kernel(hg_ref, hv_ref, bufg_ref, bufv_ref, wg_ref, wv_ref, bg_ref, bv_ref, o_ref):
    def conv(h_ref, buf_ref, w_ref, b_ref):
        w = w_ref[...]
        y = w[FF_CONV - 1:FF_CONV, :] * h_ref[...] + b_ref[...]
        for s in range(FF_CONV - 1):
            y = y + w[s:s + 1, :] * buf_ref[s]
        return y

    g = conv(hg_ref, bufg_ref, wg_ref, bg_ref)
    o_ref[...] = g * _sigmoid(g) * conv(hv_ref, bufv_ref, wv_ref, bv_ref)


def _ffn_act_step(h_g, h_v, buf, w_fc, b_fc):
    b, f = h_g.shape
    return pl.pallas_call(
        _ffn_act_step_kernel,
        out_shape=jax.ShapeDtypeStruct((b, f), F32),
        compiler_params=pltpu.CompilerParams(vmem_limit_bytes=VMEM_LIMIT_BYTES),
        name="ffn_act_step",
    )(h_g, h_v, buf[:, :, :f], buf[:, :, f:], w_fc[:, :f], w_fc[:, f:],
      b_fc[:, :f], b_fc[:, f:])


def _scan_lanes(x, op, fill):
    n = x.shape[1]
    lane = lax.broadcasted_iota(I32, x.shape, 1)
    d = 1
    while d < n:
        x = op(x, jnp.where(lane >= d, pltpu.roll(x, d, 1), fill))
        d *= 2
    return x


def _log_sigmoid(x):
    return jnp.minimum(x, 0.0) - jnp.log(1.0 + jnp.exp(-jnp.abs(x)))


def _gate_kernel(i_ref, f_ref, bi_ref, bf_ref, a_ref, g_ref, m_ref):
    ig = i_ref[...] + bi_ref[...]
    lf = _log_sigmoid(f_ref[...] + bf_ref[...])
    bsum = _scan_lanes(lf, jnp.add, 0.0)
    a = ig - bsum
    g = jnp.maximum(_scan_lanes(a, jnp.maximum, NEG_INF), 0.0)
    a_ref[...] = a
    g_ref[...] = g
    m_ref[...] = bsum + g


def _mlstm_gates(i_t, f_t, b_i, b_f):
    shp = jax.ShapeDtypeStruct(i_t.shape, F32)
    return pl.pallas_call(
        _gate_kernel, out_shape=(shp, shp, shp),
        compiler_params=pltpu.CompilerParams(vmem_limit_bytes=VMEM_LIMIT_BYTES),
        name="mlstm_gates",
    )(i_t, f_t, b_i, b_f)


def _mlstm_kernel(q_ref, k_ref, v_ref, o_ref, a_ref, g_ref, m_ref, gh_ref, out_ref, num_sc, den_sc, *, tq, hp):
    qi = pl.program_id(2)
    off_q = pl.multiple_of(qi * tq, tq)
    row = lax.broadcasted_iota(I32, (tq, tq), 0)
    col = lax.broadcasted_iota(I32, (tq, tq), 1)
    qs = [(q_ref[0, :, hh * A_DQK:(hh + 1) * A_DQK] * (A_DQK ** -0.5)).astype(BF16) for hh in range(hp)]
    g_cs = [_col(g_ref[0, hh, :, pl.ds(off_q, tq)]) for hh in range(hp)]
    num_sc[...] = jnp.zeros_like(num_sc)
    den_sc[...] = jnp.zeros_like(den_sc)

    def body(kb, carry):
        off = pl.multiple_of(kb * tq, tq)
        causal = (col + kb * tq) <= (row + qi * tq)
        for hh in range(hp):
            k = k_ref[0, pl.ds(off, tq), hh * A_DQK:(hh + 1) * A_DQK]
            v = v_ref[0, pl.ds(off, tq), hh * A_DV:(hh + 1) * A_DV]
            a = a_ref[0, hh, :, pl.ds(off, tq)]
            w = jnp.exp(jnp.where(causal, a - g_cs[hh], NEG_INF))
            s = _dot_nt(qs[hh], k) * w
            num_sc[hh] += _dot(s.astype(BF16), v)
            den_sc[hh] += jnp.sum(s, axis=1, keepdims=True)
        return carry

    lax.fori_loop(0, qi + 1, body, 0)
    for hh in range(hp):
        m_c = _col(m_ref[0, hh, :, pl.ds(off_q, tq)])
        h = num_sc[hh] / jnp.maximum(jnp.abs(den_sc[hh]), jnp.exp(-m_c))
        h = h * lax.rsqrt(jnp.mean(h * h, axis=-1, keepdims=True) + EPS) * gh_ref[hh]
        cols = slice(hh * A_DV, (hh + 1) * A_DV)
        out_ref[0, :, cols] = (_sigmoid(o_ref[0, :, cols]) * h).astype(out_ref.dtype)


def _mlstm_prompt(p3, pb3, a, g, m, g_head):
    b, t, _ = p3.shape
    tq = 256
    hp = 4
    row_spec = pl.BlockSpec((1, hp, 1, t), lambda i, h, j: (i, h, 0, 0))
    wq, wv = hp * A_DQK, hp * A_DV
    return pl.pallas_call(
        functools.partial(_mlstm_kernel, tq=tq, hp=hp),
        grid=(b, A_HEADS // hp, t // tq),
        in_specs=[pl.BlockSpec((1, tq, wq), lambda i, h, j: (i, j, _AQ // wq + h)),
                  pl.BlockSpec((1, t, wq), lambda i, h, j: (i, 0, _AK // wq + h)),
                  pl.BlockSpec((1, t, wv), lambda i, h, j: (i, 0, _AV // wv + h)),
                  pl.BlockSpec((1, tq, wv), lambda i, h, j: (i, j, _AO // wv + h)),
                  row_spec, row_spec, row_spec,
                  pl.BlockSpec((hp, 1, A_DV), lambda i, h, j: (h, 0, 0))],
        out_specs=pl.BlockSpec((1, tq, wv), lambda i, h, j: (i, j, h)),
        out_shape=jax.ShapeDtypeStruct((b, t, A_HEADS * A_DV), BF16),
        scratch_shapes=[pltpu.VMEM((hp, tq, A_DV), F32), pltpu.VMEM((hp, tq, 1), F32)],
        compiler_params=_params(("parallel", "parallel", "arbitrary")),
        name="mlstm_prompt",
    )(p3, pb3, pb3, p3, a, g, m, g_head)


def _mlstm_state_kernel(k_ref, v_ref, a_ref, g_ref, c_ref, n_ref, *, t, tc):
    g_last = g_ref[0, 0, :, t - 1:t]

    def body(cb, carry):
        c_acc, n_acc = carry
        off = pl.multiple_of(cb * tc, tc)
        w = _col(jnp.exp(a_ref[0, 0, :, pl.ds(off, tc)] - g_last))
        kw = k_ref[0, pl.ds(off, tc), :] * w
        vt = v_ref[0, pl.ds(off, tc), :].T.astype(BF16)
        return c_acc + _dot(vt, kw.astype(BF16)), n_acc + jnp.sum(kw, axis=0, keepdims=True)

    c_acc, n_acc = lax.fori_loop(0, t // tc, body,
                                 (jnp.zeros((A_DV, A_DQK), F32), jnp.zeros((1, A_DQK), F32)))
    c_ref[0, 0] = c_acc
    n_ref[0, 0] = n_acc


def _mlstm_prompt_state(p3, a, g):
    b, t, _ = p3.shape
    row_spec = pl.BlockSpec((1, 1, 1, t), lambda i, h: (i, h, 0, 0))
    return pl.pallas_call(
        functools.partial(_mlstm_state_kernel, t=t, tc=256),
        grid=(b, A_HEADS),
        in_specs=[pl.BlockSpec((1, t, A_DQK), lambda i, h: (i, 0, _AK // A_DQK + h)),
                  pl.BlockSpec((1, t, A_DV), lambda i, h: (i, 0, _AV // A_DV + h)),
                  row_spec, row_spec],
        out_specs=(pl.BlockSpec((1, 1, A_DV, A_DQK), lambda i, h: (i, h, 0, 0)),
                   pl.BlockSpec((1, 1, 1, A_DQK), lambda i, h: (i, h, 0, 0))),
        out_shape=(jax.ShapeDtypeStruct((b, A_HEADS, A_DV, A_DQK), F32),
                   jax.ShapeDtypeStruct((b, A_HEADS, 1, A_DQK), F32)),
        compiler_params=_params(("parallel", "parallel")),
        name="mlstm_prompt_state",
    )(p3, p3, a, g)


def _mlstm_step_kernel(q_ref, k_ref, v_ref, o_ref, gh_ref, ig_ref, fp_ref, c0_ref, n0_ref, m0_ref,
                       h_ref, c_ref, n_ref, m_ref):
    q = q_ref[0, 0] * (A_DQK ** -0.5)
    k = k_ref[0, 0]
    v = v_ref[0, 0]
    ig = ig_ref[0, 0]
    lf = _log_sigmoid(fp_ref[0, 0])
    c0 = c0_ref[0, 0]
    n0 = n0_ref[0, 0]
    m0 = m0_ref[0, 0]
    log_inter = lf + m0
    m_t = jnp.maximum(log_inter, ig)
    w_intra = jnp.exp(ig - m_t)
    w_inter = jnp.exp(log_inter - m_t)
    s = jnp.sum(q * k, axis=1, keepdims=True) * w_intra
    num = s * v + w_inter * jnp.sum(c0 * q, axis=1, keepdims=True)
    den = s + w_inter * jnp.sum(n0 * q, axis=1, keepdims=True)
    h = num / jnp.maximum(jnp.abs(den), jnp.exp(-m_t))
    h = h * lax.rsqrt(jnp.mean(h * h, axis=0, keepdims=True) + EPS) * gh_ref[0]
    h_ref[0, 0] = _sigmoid(o_ref[0, 0]) * h
    c_ref[0, 0] = w_inter * c0 + w_intra * (v * k)
    n_ref[0, 0] = w_inter * n0 + w_intra * k
    m_ref[0, 0] = m_t


def _mlstm_step(q, k, v_col, o_col, gh_col, ig, f_pre, c0, n0, m0):
    b = q.shape[0]
    h = A_HEADS
    row = pl.BlockSpec((1, 1, 1, A_DQK), lambda i, j: (i, j, 0, 0))
    colv = pl.BlockSpec((1, 1, A_DV, 1), lambda i, j: (i, j, 0, 0))
    one = pl.BlockSpec((1, 1, 1, 1), lambda i, j: (i, j, 0, 0))
    mat = pl.BlockSpec((1, 1, A_DV, A_DQK), lambda i, j: (i, j, 0, 0))
    return pl.pallas_call(
        _mlstm_step_kernel,
        grid=(b, h),
        in_specs=[row, row, colv, colv, pl.BlockSpec((1, A_DV, 1), lambda i, j: (j, 0, 0)),
                  one, one, mat, row, one],
        out_specs=(colv, mat, row, one),
        out_shape=(jax.ShapeDtypeStruct((b, h, A_DV, 1), F32),
                   jax.ShapeDtypeStruct((b, h, A_DV, A_DQK), F32),
                   jax.ShapeDtypeStruct((b, h, 1, A_DQK), F32),
                   jax.ShapeDtypeStruct((b, h, 1, 1), F32)),
        compiler_params=_params(("parallel", "parallel")),
        name="mlstm_step",
    )(q, k, v_col, o_col, gh_col, ig, f_pre, c0, n0, m0)


def _sort_key(x):
    x = jnp.where(x == 0.0, 0.0, x)
    bits = pltpu.bitcast(x, I32)
    return jnp.where(bits >= 0, bits, bits ^ jnp.int32(0x7FFFFFFF))


def _kth_largest_key(key, k):
    def count_ge(c):
        return jnp.sum(jnp.where(key >= c, 1.0, 0.0), axis=1, keepdims=True)

    kf = jnp.float32(k)
    thr = jnp.where(count_ge(jnp.int32(0)) >= kf, jnp.int32(0), jnp.int32(INT_MIN))
    thr = jnp.broadcast_to(thr, (key.shape[0], 1))

    def body(i, thr):
        cand = thr + lax.shift_left(jnp.int32(1), jnp.int32(30) - i)
        return jnp.where(count_ge(cand) >= kf, cand, thr)

    return lax.fori_loop(0, 31, body, thr)


def _tri_ones(n, dtype):
    r = lax.broadcasted_iota(I32, (n, n), 0)
    c = lax.broadcasted_iota(I32, (n, n), 1)
    return jnp.where(r <= c, 1.0, 0.0).astype(dtype)


def _select_topk(scores, visible, k):
    key = _sort_key(jnp.where(visible, scores, NEG_INF))
    thr = _kth_largest_key(key, k)
    gt = key > thr
    eq = jnp.logical_and(key == thr, visible)
    need = jnp.float32(k) - jnp.sum(jnp.where(gt, 1.0, 0.0), axis=1, keepdims=True)
    tri = _tri_ones(LANE, BF16)
    off = jnp.zeros((scores.shape[0], 1), F32)
    rank = []
    eqf = jnp.where(eq, 1.0, 0.0)
    for c in range(scores.shape[1] // LANE):
        p = _dot(eqf[:, c * LANE:(c + 1) * LANE].astype(BF16), tri) + off
        rank.append(p)
        off = p[:, LANE - 1:LANE]
    rank = jnp.concatenate(rank, axis=1)
    return jnp.logical_or(jnp.logical_and(gt, visible), jnp.logical_and(eq, rank <= need))


def _indexer_scores(qi_head, wi, ki):
    acc = None
    for h in range(IDX_HEADS):
        s = _dot_nt(qi_head(h).astype(BF16), ki) * (IDX_DIM ** -0.5)
        term = jnp.maximum(s, 0.0) * wi[:, h:h + 1]
        acc = term if acc is None else acc + term
    return acc * (IDX_HEADS ** -0.5)


def _dsa_prompt_kernel(q_ref, k_ref, v_ref, qi_ref, sm_ref, smq_ref, o_ref, *, tq, topk, extents):
    j = pl.program_id(1)
    wi = smq_ref[0, :, _S_IW:_S_IW + IDX_HEADS]
    group = B_HEADS // B_KV_HEADS

    def attend(e):
        ki = sm_ref[0, 0:e, _S_IK:_S_IK + IDX_DIM].astype(BF16)
        scores = _indexer_scores(lambda h: qi_ref[0, :, h * IDX_DIM:(h + 1) * IDX_DIM], wi, ki)
        qpos = lax.broadcasted_iota(I32, (tq, e), 0) + j * tq
        kpos = lax.broadcasted_iota(I32, (tq, e), 1)
        sel = _select_topk(scores, kpos <= qpos, topk)
        bias = jnp.where(sel, 0.0, NEG_INF)
        for n in range(B_KV_HEADS):
            kn = k_ref[0, 0:e, n * B_DH:(n + 1) * B_DH]
            vn = v_ref[0, 0:e, n * B_DH:(n + 1) * B_DH]
            for gidx in range(group):
                h = n * group + gidx
                qh = q_ref[0, :, h * B_DH:(h + 1) * B_DH]
                s = _dot_nt(qh, kn) * (B_DH ** -0.5) + bias
                p = jnp.exp(s - jnp.max(s, axis=1, keepdims=True))
                l = jnp.sum(p, axis=1, keepdims=True)
                o_ref[0, :, h * B_DH:(h + 1) * B_DH] = (_dot(p.astype(BF16), vn) / l).astype(o_ref.dtype)

    last = (j + 1) * tq
    for idx, e in enumerate(extents):
        lo = extents[idx - 1] if idx else 0
        pl.when(jnp.logical_and(last > lo, last <= e))(functools.partial(attend, e))


def _dsa_prompt(pb3, small3):
    b, t, _ = pb3.shape
    tq = 256
    topk = min(TOPK_MAX, t // 4)
    n_ext = min(4, t // tq)
    extents = tuple(t * (i + 1) // n_ext for i in range(n_ext))
    assert all(e % tq == 0 for e in extents) and extents[0] >= topk
    w_q, w_kv, w_qi = B_HEADS * B_DH, B_KV_HEADS * B_DH, IDX_HEADS * IDX_DIM
    return pl.pallas_call(
        functools.partial(_dsa_prompt_kernel, tq=tq, topk=topk, extents=extents),
        grid=(b, t // tq),
        in_specs=[pl.BlockSpec((1, tq, w_q), lambda i, j: (i, j, _BQ // w_q)),
                  pl.BlockSpec((1, t, w_kv), lambda i, j: (i, 0, _BK // w_kv)),
                  pl.BlockSpec((1, t, w_kv), lambda i, j: (i, 0, _BV // w_kv)),
                  pl.BlockSpec((1, tq, w_qi), lambda i, j: (i, j, _IQ // w_qi)),
                  pl.BlockSpec((1, t, LANE), lambda i, j: (i, 0, 0)),
                  pl.BlockSpec((1, tq, LANE), lambda i, j: (i, j, 0))],
        out_specs=pl.BlockSpec((1, tq, w_q), lambda i, j: (i, j, 0)),
        out_shape=jax.ShapeDtypeStruct((b, t, w_q), BF16),
        compiler_params=_params(("parallel", "parallel")),
        name="dsa_prompt",
    )(pb3, pb3, pb3, pb3, small3, small3)


def _page_specs(block, layer, pages_per_step):
    def spec(r):
        zeros = (0,) * (len(block) - 2)
        return pl.BlockSpec(block, lambda i, p, pt: (layer, pt[i, p * pages_per_step + r]) + zeros)
    return [spec(r) for r in range(pages_per_step)]


def _dsa_step_scores_kernel(pt_ref, qi_ref, wi_ref, *refs):
    del pt_ref
    ki_refs, o_ref = refs[:-1], refs[-1]
    qi = qi_ref[0].astype(BF16)
    ki = jnp.concatenate([r[0] for r in ki_refs], axis=0).astype(BF16)
    s = _dot_nt(qi, ki) * (IDX_DIM ** -0.5)
    o_ref[0] = jnp.sum(jnp.maximum(s, 0.0) * wi_ref[0], axis=0, keepdims=True) * (IDX_HEADS ** -0.5)


def _dsa_step_scores(page_table, qi, wi_col, cache_ki, layer):
    b, n_pages = page_table.shape
    g = math.gcd(n_pages, DSA_PAGES_PER_STEP)
    grid_spec = pltpu.PrefetchScalarGridSpec(
        num_scalar_prefetch=1,
        grid=(b, n_pages // g),
        in_specs=[pl.BlockSpec((1, IDX_HEADS, IDX_DIM), lambda i, p, pt: (i, 0, 0)),
                  pl.BlockSpec((1, IDX_HEADS, 1), lambda i, p, pt: (i, 0, 0))]
                 + _page_specs((None, 1, PAGE, IDX_DIM), layer, g),
        out_specs=pl.BlockSpec((1, 1, g * PAGE), lambda i, p, pt: (i, 0, p)),
    )
    return pl.pallas_call(
        _dsa_step_scores_kernel,
        grid_spec=grid_spec,
        out_shape=jax.ShapeDtypeStruct((b, 1, n_pages * PAGE), F32),
        compiler_params=_params(("parallel", "arbitrary")),
        name="dsa_step_scores",
    )(page_table, qi, wi_col, *([cache_ki] * g))


def _dsa_step_select_kernel(sc_ref, self_ref, sel_ref, selself_ref, *, topk):
    sc = sc_ref[...]
    sself = self_ref[...]
    key = _sort_key(sc)
    key_self = _sort_key(sself)
    kf = jnp.float32(topk)

    def count_ge(c):
        return (jnp.sum(jnp.where(key >= c, 1.0, 0.0), axis=1, keepdims=True)
                + jnp.where(key_self >= c, 1.0, 0.0))

    thr = jnp.where(count_ge(jnp.int32(0)) >= kf, jnp.int32(0), jnp.int32(INT_MIN))

    def body(i, thr):
        cand = thr + lax.shift_left(jnp.int32(1), jnp.int32(30) - i)
        return jnp.where(count_ge(cand) >= kf, cand, thr)

    thr = lax.fori_loop(0, 31, body, thr)
    gt = key > thr
    eq = key == thr
    n_gt = jnp.sum(jnp.where(gt, 1.0, 0.0), axis=1, keepdims=True) + jnp.where(key_self > thr, 1.0, 0.0)
    need = kf - n_gt
    rank = _scan_lanes(jnp.where(eq, 1.0, 0.0), jnp.add, 0.0)
    sel_ref[...] = jnp.where(jnp.logical_or(gt, jnp.logical_and(eq, rank <= need)), 1.0, 0.0)
    n_eq = rank[:, -1:]
    selself_ref[...] = jnp.where(
        jnp.logical_or(key_self > thr, jnp.logical_and(key_self == thr, n_eq < need)), 1.0, 0.0)


def _dsa_step_select(scores, self_score, topk):
    b, s = scores.shape
    return pl.pallas_call(
        functools.partial(_dsa_step_select_kernel, topk=topk),
        out_shape=(jax.ShapeDtypeStruct((b, s), F32), jax.ShapeDtypeStruct((b, 1), F32)),
        compiler_params=pltpu.CompilerParams(vmem_limit_bytes=VMEM_LIMIT_BYTES),
        name="dsa_step_select",
    )(scores, self_score)


def _dsa_step_attend_kernel(pt_ref, q_ref, sel_ref, selself_ref, kn_ref, vn_ref, *refs, pages_per_step):
    del pt_ref
    g = pages_per_step
    k_refs, v_refs = refs[:g], refs[g:2 * g]
    o_ref, m_sc, l_sc, acc_sc = refs[2 * g:]
    p_idx = pl.program_id(1)
    group = B_HEADS // B_KV_HEADS

    @pl.when(p_idx == 0)
    def _():
        m_sc[...] = jnp.full_like(m_sc, NEG_INF)
        l_sc[...] = jnp.zeros_like(l_sc)
        acc_sc[...] = jnp.zeros_like(acc_sc)

    q = q_ref[0].astype(BF16)
    kv_of_row = lax.broadcasted_iota(I32, (B_HEADS, 1), 0) // group

    def head_rows(page_refs, n):
        return jnp.concatenate([r[0, :, n, :] for r in page_refs], axis=0).astype(BF16)

    def update(k_pages, v_pages, sel_row):
        s = jnp.zeros((B_HEADS, sel_row.shape[1]), F32)
        for n in range(B_KV_HEADS):
            s = jnp.where(kv_of_row == n, _dot_nt(q, head_rows(k_pages, n)), s)
        s = s * (B_DH ** -0.5) + jnp.where(sel_row > 0.0, 0.0, NEG_INF)
        m_old = m_sc[...]
        m_new = jnp.maximum(m_old, jnp.max(s, axis=1, keepdims=True))
        m_safe = jnp.where(m_new == NEG_INF, 0.0, m_new)
        alpha = jnp.exp(m_old - m_safe)
        p = jnp.exp(s - m_safe)
        pb = p.astype(BF16)
        pv = jnp.zeros((B_HEADS, B_DH), F32)
        for n in range(B_KV_HEADS):
            pv = jnp.where(kv_of_row == n, _dot(pb, head_rows(v_pages, n)), pv)
        l_sc[...] = alpha * l_sc[...] + jnp.sum(p, axis=1, keepdims=True)
        acc_sc[...] = alpha * acc_sc[...] + pv
        m_sc[...] = m_new

    update(k_refs, v_refs, sel_ref[0])

    @pl.when(p_idx == pl.num_programs(1) - 1)
    def _():
        update([kn_ref], [vn_ref], selself_ref[0])
        o_ref[0] = acc_sc[...] / l_sc[...]


def _dsa_step_attend(page_table, q, sel, sel_self, k_new, v_new, cache_k, cache_v, layer):
    b, n_pages = page_table.shape
    g = math.gcd(n_pages, DSA_PAGES_PER_STEP)
    own = pl.BlockSpec((1, PAGE, B_KV_HEADS, B_DH), lambda i, p, pt: (i, 0, 0, 0))
    page_block = (None, 1, PAGE, B_KV_HEADS, B_DH)
    grid_spec = pltpu.PrefetchScalarGridSpec(
        num_scalar_prefetch=1,
        grid=(b, n_pages // g),
        in_specs=[pl.BlockSpec((1, B_HEADS, B_DH), lambda i, p, pt: (i, 0, 0)),
                  pl.BlockSpec((1, 1, g * PAGE), lambda i, p, pt: (i, 0, p)),
                  pl.BlockSpec((1, 1, PAGE), lambda i, p, pt: (i, 0, 0)),
                  own, own]
                 + _page_specs(page_block, layer, g) + _page_specs(page_block, layer, g),
        out_specs=pl.BlockSpec((1, B_HEADS, B_DH), lambda i, p, pt: (i, 0, 0)),
        scratch_shapes=[pltpu.VMEM((B_HEADS, 1), F32), pltpu.VMEM((B_HEADS, 1), F32),
                        pltpu.VMEM((B_HEADS, B_DH), F32)],
    )
    return pl.pallas_call(
        functools.partial(_dsa_step_attend_kernel, pages_per_step=g),
        grid_spec=grid_spec,
        out_shape=jax.ShapeDtypeStruct((b, B_HEADS, B_DH), F32),
        compiler_params=_params(("parallel", "arbitrary")),
        name="dsa_step_attend",
    )(page_table, q, sel, sel_self, k_new, v_new, *([cache_k] * g), *([cache_v] * g))


def _layernorm(x, g, b):
    xc = x - jnp.mean(x, axis=-1, keepdims=True)
    return xc * lax.rsqrt(jnp.mean(xc * xc, axis=-1, keepdims=True) + EPS) * g + b


def _odd_mix_kernel(u_ref, v_ref, a_ref, gt_ref, gv_ref, bv_ref, wsp_ref, bsp_ref, wdc_ref, bdc_ref,
                    gdn_ref, bdn_ref, o_ref, tail_ref, xbuf, *, tt, width, halo):
    j = pl.program_id(1)
    nj = pl.num_programs(1)

    @pl.when(j == 0)
    def _():
        xbuf[0:halo, :] = jnp.zeros((halo, width), F32)

    vn = _layernorm(v_ref[0], gv_ref[...], bv_ref[...]).astype(BF16)
    r = lax.broadcasted_iota(I32, (tt, tt), 0)
    c = lax.broadcasted_iota(I32, (tt, tt), 1)
    gw = width // C_GROUPS
    for gi in range(C_GROUPS):
        wm = jnp.where(c <= r, wsp_ref[gi], 0.0).astype(BF16)
        sg = _dot(wm, vn[:, gi * gw:(gi + 1) * gw]) + bsp_ref[gi]
        o_ref[0, :, gi * gw:(gi + 1) * gw] = (u_ref[0, :, gi * gw:(gi + 1) * gw] * sg).astype(o_ref.dtype)

    glu = a_ref[0] * _sigmoid(gt_ref[0])
    xbuf[halo:halo + tt, :] = glu
    wdc = wdc_ref[...]
    dc = jnp.zeros((tt, width), F32) + bdc_ref[...]
    base = halo - (D_CONV - 1)
    for kk in range(D_CONV):
        dc = dc + wdc[kk:kk + 1, :] * xbuf[base + kk:base + kk + tt, :]
    d = _layernorm(dc, gdn_ref[...], bdn_ref[...])
    o_ref[0, :, width:2 * width] = (d * _sigmoid(d)).astype(o_ref.dtype)

    xbuf[0:halo, :] = glu[tt - halo:tt, :]

    @pl.when(j == nj - 1)
    def _():
        tail_ref[0] = glu[tt - halo:tt, :]


def _odd_mix(p3, g_v, b_v, w_sp, b_sp, w_dc, b_dc, g_dn, b_dn):
    b, t, w4 = p3.shape
    width = w4 // 4
    tt = C_CHUNK
    halo = 32
    vec = lambda x: x.reshape(1, width)
    vspec = pl.BlockSpec((1, width), lambda i, j: (0, 0))
    return pl.pallas_call(
        functools.partial(_odd_mix_kernel, tt=tt, width=width, halo=halo),
        grid=(b, t // tt),
        in_specs=[pl.BlockSpec((1, tt, width), lambda i, j: (i, j, 0)),
                  pl.BlockSpec((1, tt, width), lambda i, j: (i, j, 1)),
                  pl.BlockSpec((1, tt, width), lambda i, j: (i, j, 2)),
                  pl.BlockSpec((1, tt, width), lambda i, j: (i, j, 3)),
                  vspec, vspec,
                  pl.BlockSpec((C_GROUPS, tt, tt), lambda i, j: (0, 0, 0)),
                  pl.BlockSpec((C_GROUPS, tt, 1), lambda i, j: (0, 0, 0)),
                  pl.BlockSpec((D_CONV, width), lambda i, j: (0, 0)),
                  vspec, vspec, vspec],
        out_specs=(pl.BlockSpec((1, tt, 2 * width), lambda i, j: (i, j, 0)),
                   pl.BlockSpec((1, halo, width), lambda i, j: (i, 0, 0))),
        out_shape=(jax.ShapeDtypeStruct((b, t, 2 * width), BF16),
                   jax.ShapeDtypeStruct((b, halo, width), F32)),
        scratch_shapes=[pltpu.VMEM((halo + tt, width), F32)],
        compiler_params=_params(("parallel", "arbitrary")),
        name="odd_mix",
    )(p3, p3, p3, p3, vec(g_v), vec(b_v), w_sp, b_sp.reshape(C_GROUPS, tt, 1), w_dc, vec(b_dc),
      vec(g_dn), vec(b_dn))


def _odd_mix_step_kernel(p_ref, buf_ref, gv_ref, bv_ref, wsp_ref, bsp_ref, wdc_ref, bdc_ref,
                         gdn_ref, bdn_ref, o_ref, vn_ref, glu_ref, *, width):
    u = p_ref[:, 0:width]
    v = p_ref[:, width:2 * width]
    a = p_ref[:, 2 * width:3 * width]
    gt = p_ref[:, 3 * width:4 * width]
    vn = _layernorm(v, gv_ref[...], bv_ref[...])
    vn_ref[...] = vn
    o_ref[:, 0:width] = u * (wsp_ref[...] * vn + bsp_ref[...])
    glu = a * _sigmoid(gt)
    glu_ref[...] = glu
    wdc = wdc_ref[...]
    dc = wdc[D_CONV - 1:D_CONV, :] * glu + bdc_ref[...]
    for kk in range(D_CONV - 1):
        dc = dc + wdc[kk:kk + 1, :] * buf_ref[kk]
    d = _layernorm(dc, gdn_ref[...], bdn_ref[...])
    o_ref[:, width:2 * width] = d * _sigmoid(d)


def _odd_mix_step(p, buf, g_v, b_v, w_sp_vec, b_sp_vec, w_dc, b_dc, g_dn, b_dn):
    b, w4 = p.shape
    width = w4 // 4
    vec = lambda x: x.reshape(1, width)
    return pl.pallas_call(
        functools.partial(_odd_mix_step_kernel, width=width),
        out_shape=(jax.ShapeDtypeStruct((b, 2 * width), F32), jax.ShapeDtypeStruct((b, width), F32),
                   jax.ShapeDtypeStruct((b, width), F32)),
        compiler_params=pltpu.CompilerParams(vmem_limit_bytes=VMEM_LIMIT_BYTES),
        name="odd_mix_step",
    )(p, buf, vec(g_v), vec(b_v), vec(w_sp_vec), vec(b_sp_vec), w_dc, vec(b_dc), vec(g_dn), vec(b_dn))


def _pad_rows(x, rows):
    return jnp.pad(x, ((0, rows - x.shape[0]), (0, 0)))


def _split_even_weights(w):
    sizes = (1024, 1024, 2048, 2048, 8, 8, 2048, 512, 512, 1024, 16, 64)
    offs = [0]
    for s in sizes:
        offs.append(offs[-1] + s)
    col = lambda i: w[:, offs[i]:offs[i + 1]]
    return jnp.concatenate([col(0), col(1), col(2), col(3), col(6), col(7), col(8), col(9),
                            col(4), col(5), col(10), col(11),
                            jnp.zeros((w.shape[0], LANE - 96), w.dtype)], axis=1)


def _conv_ffn(h_p, h_s, tp, g, layer, w_up, w_fc, b_fc, w_down_b, buf_s):
    f = w_up.shape[2] // 2
    nb = buf_s.shape[0]
    xn = _rmsnorm(h_p, g, BF16)
    xs = _rmsnorm(h_s, g, BF16)
    act, tail_g, tail_v, hs_g, hs_v = _ffn_up(xn, xs, w_up, w_fc, b_fc.reshape(b_fc.shape[0], 1, 2 * f),
                                              layer, tp)
    h_p_new = _matmul_xres(act, w_down_b, layer, h_p)
    ff_p = jnp.concatenate([tail_g, tail_v], axis=-1)[:, 8 - (FF_CONV - 1):, :]
    act_s = _ffn_act_step(hs_g[:nb], hs_v[:nb], buf_s.transpose(1, 0, 2), w_fc[layer], b_fc[layer].reshape(1, 2 * f))
    h_s_new = _matmul(_pad_rows(act_s.astype(BF16), h_s.shape[0]), w_down_b, layer, res=h_s, tn=1024, tk=f // 2)
    ff_s = jnp.concatenate([buf_s[:, 1:, :], jnp.concatenate([hs_g[:nb], hs_v[:nb]], axis=-1)[:, None, :]], axis=1)
    return h_p_new, h_s_new, ff_p, ff_s


def kernel(x_prompt, x_sample, cache_k, cache_v, cache_kidx, state_mlstm_C, state_mlstm_n, state_mlstm_m,
           state_conv_d, state_ffn_conv, page_table, w_norm_mix, w_norm_ffn, w_in_even, b_gate_even,
           g_head_even, w_out_even, w_in_odd, g_vnorm, b_vnorm, w_spatial, b_spatial, w_dconv, b_dconv,
           g_dnorm, b_dnorm, w_out_odd, w_up, w_fconv, b_fconv, w_down, w_norm_final):
    bp, tp, d = x_prompt.shape
    bs = x_sample.shape[0]
    assert x_sample.shape[1] == 1
    ms = 16
    h_p = x_prompt.reshape(bp * tp, d)
    h_s = _pad_rows(x_sample.reshape(bs, d), ms)
    n_pages = page_table.shape[1]
    past = n_pages * PAGE

    w_even = _split_even_weights(w_in_even[0])[None]
    w_down_b = w_down.astype(BF16)
    b_i = b_gate_even[0, :A_HEADS]
    b_f = b_gate_even[0, A_HEADS:]
    g_head = g_head_even[0]

    xn = _rmsnorm(h_p, w_norm_mix[0], BF16)
    xs = _rmsnorm(h_s, w_norm_mix[0], BF16)
    proj, proj_s, proj_b = _matmul_ws(xn, xs, w_even, 0, 0, _N_MAIN, emit_bf16=True)
    small, small_s = _matmul_ws(xn, xs, w_even, 0, _N_MAIN, LANE, tn=LANE)
    proj_s = proj_s[:bs]
    small_s = small_s[:bs]
    p3 = proj.reshape(bp, tp, _N_MAIN)
    pb3 = proj_b.reshape(bp, tp, _N_MAIN)
    small3 = small.reshape(bp, tp, LANE)

    def heads_t(x):
        return x.transpose(0, 2, 1).reshape(bp * A_HEADS, tp)

    a_g, g_g, m_g = _mlstm_gates(heads_t(small3[:, :, _S_AI:_S_AI + A_HEADS]),
                                 heads_t(small3[:, :, _S_AF:_S_AF + A_HEADS]),
                                 jnp.tile(b_i, bp).reshape(-1, 1), jnp.tile(b_f, bp).reshape(-1, 1))
    a4 = a_g.reshape(bp, A_HEADS, 1, tp)
    g4 = g_g.reshape(bp, A_HEADS, 1, tp)
    m4 = m_g.reshape(bp, A_HEADS, 1, tp)
    h_a = _mlstm_prompt(p3, pb3, a4, g4, m4, g_head.reshape(A_HEADS, 1, A_DV))
    c_p, n_p = _mlstm_prompt_state(p3, a4, g4)
    m_p = m_g.reshape(bp, A_HEADS, tp)[:, :, -1]
    h_b = _dsa_prompt(pb3, small3)
    mix = jnp.concatenate([h_a, h_b], axis=-1).reshape(bp * tp, -1)

    k_p = p3[:, :, _BK:_BK + B_KV_HEADS * B_DH].reshape(bp, tp, B_KV_HEADS, B_DH)
    v_p = p3[:, :, _BV:_BV + B_KV_HEADS * B_DH].reshape(bp, tp, B_KV_HEADS, B_DH)
    ki_p = small3[:, :, _S_IK:_S_IK + IDX_DIM]

    q_s = proj_s[:, _AQ:_AQ + 1024].reshape(bs, A_HEADS, 1, A_DQK)
    kk_s = proj_s[:, _AK:_AK + 1024].reshape(bs, A_HEADS, 1, A_DQK)
    v_s = proj_s[:, _AV:_AV + 2048].reshape(bs, A_HEADS, A_DV, 1)
    o_s = proj_s[:, _AO:_AO + 2048].reshape(bs, A_HEADS, A_DV, 1)
    ig_s = (small_s[:, _S_AI:_S_AI + A_HEADS] + b_i).reshape(bs, A_HEADS, 1, 1)
    fp_s = (small_s[:, _S_AF:_S_AF + A_HEADS] + b_f).reshape(bs, A_HEADS, 1, 1)
    hcol_s, c_s, n_s, m_s = _mlstm_step(
        q_s, kk_s, v_s, o_s, g_head.reshape(A_HEADS, A_DV, 1), ig_s, fp_s,
        state_mlstm_C[0], state_mlstm_n[0].reshape(bs, A_HEADS, 1, A_DQK),
        state_mlstm_m[0].reshape(bs, A_HEADS, 1, 1))
    h_a_s = hcol_s.reshape(bs, A_HEADS * A_DV)

    bq_s = proj_s[:, _BQ:_BQ + 2048].reshape(bs, B_HEADS, B_DH)
    bk_s = proj_s[:, _BK:_BK + 512]
    bv_s = proj_s[:, _BV:_BV + 512]
    iq_s = proj_s[:, _IQ:_IQ + 1024].reshape(bs, IDX_HEADS, IDX_DIM)
    iw_s = small_s[:, _S_IW:_S_IW + IDX_HEADS]
    ik_s = small_s[:, _S_IK:_S_IK + IDX_DIM]
    sc_past = _dsa_step_scores(page_table, iq_s, iw_s.reshape(bs, IDX_HEADS, 1), cache_kidx, 0)
    own_page = lambda x: jnp.pad(x[:, None], ((0, 0), (0, PAGE - 1)) + ((0, 0),) * (x.ndim - 1))
    sc_self = _dsa_step_scores(jnp.arange(bs, dtype=I32).reshape(bs, 1), iq_s, iw_s.reshape(bs, IDX_HEADS, 1),
                               own_page(ik_s)[None], 0)[:, 0, 0:1]
    topk_s = min(TOPK_MAX, (past + 1) // 4)
    sel, sel_self = _dsa_step_select(sc_past.reshape(bs, past), sc_self, topk_s)
    h_b_s = _dsa_step_attend(page_table, bq_s, sel.reshape(bs, 1, past),
                             jnp.pad(sel_self, ((0, 0), (0, PAGE - 1))).reshape(bs, 1, PAGE),
                             own_page(bk_s.reshape(bs, B_KV_HEADS, B_DH)), own_page(bv_s.reshape(bs, B_KV_HEADS, B_DH)),
                             cache_k, cache_v, 0)
    mix_s = jnp.concatenate([h_a_s, h_b_s.reshape(bs, B_HEADS * B_DH)], axis=-1).astype(BF16)
    h_p, h_s = _matmul_ws(mix, _pad_rows(mix_s, ms), w_out_even, 0, 0, d, res=h_p, res_s=h_s)

    h_p, h_s, ff_p0, ff_s0 = _conv_ffn(h_p, h_s, tp, w_norm_ffn[0], 0, w_up, w_fconv, b_fconv, w_down_b,
                                       state_ffn_conv[0])

    width = w_in_odd.shape[2] // 4
    xn = _rmsnorm(h_p, w_norm_mix[1], BF16)
    xs = _rmsnorm(h_s, w_norm_mix[1], BF16)
    proj1, proj1_s = _matmul_ws(xn, xs, w_in_odd, 0, 0, 4 * width)
    proj1_s = proj1_s[:bs]
    mix1, tail = _odd_mix(proj1.reshape(bp, tp, 4 * width), g_vnorm[0], b_vnorm[0], w_spatial[0], b_spatial[0],
                          w_dconv[0], b_dconv[0], g_dnorm[0], b_dnorm[0])
    cd_p = tail[:, tail.shape[1] - (D_CONV - 1):, :]
    gw = width // C_GROUPS
    mix1_s, vn_s, glu_s = _odd_mix_step(
        proj1_s, state_conv_d[0].transpose(1, 0, 2), g_vnorm[0], b_vnorm[0],
        jnp.repeat(w_spatial[0, :, 0, 0], gw), jnp.repeat(b_spatial[0, :, 0], gw),
        w_dconv[0], b_dconv[0], g_dnorm[0], b_dnorm[0])
    h_p, h_s = _matmul_ws(mix1.reshape(bp * tp, 2 * width), _pad_rows(mix1_s.astype(BF16), ms), w_out_odd, 0, 0, d,
                          res=h_p, res_s=h_s)
    cd_s = jnp.concatenate([state_conv_d[0][:, 1:, :], glu_s[:, None, :]], axis=1)

    h_p, h_s, ff_p1, ff_s1 = _conv_ffn(h_p, h_s, tp, w_norm_ffn[1], 1, w_up, w_fconv, b_fconv, w_down_b,
                                       state_ffn_conv[1])

    y_p = _rmsnorm(h_p, w_norm_final, F32).reshape(bp, tp, d)
    y_s = _rmsnorm(h_s, w_norm_final, F32)[:bs].reshape(bs, 1, d)

    return (y_p, y_s,
            k_p[None], v_p[None], ki_p[None],
            bk_s.reshape(1, bs, 1, B_KV_HEADS, B_DH), bv_s.reshape(1, bs, 1, B_KV_HEADS, B_DH),
            ik_s.reshape(1, bs, 1, IDX_DIM),
            c_p[None], n_p.reshape(1, bp, A_HEADS, A_DQK), m_p[None],
            c_s[None], n_s.reshape(1, bs, A_HEADS, A_DQK), m_s.reshape(1, bs, A_HEADS),
            vn_s.reshape(1, bs, 1, width),
            cd_p[None], cd_s[None],
            jnp.stack([ff_p0, ff_p1]), jnp.stack([ff_s0, ff_s1]))
```

```python
import functools
import math

import jax
import jax.numpy as jnp
from jax import lax
from jax.experimental import pallas as pl
from jax.experimental.pallas import tpu as pltpu

F32 = jnp.float32
BF16 = jnp.bfloat16
I32 = jnp.int32
EPS = 1e-6
NEG_INF = float("-inf")
INT_MIN = -(2 ** 31)

A_HEADS = 8
A_DQK = 128
A_DV = 256
B_HEADS = 16
B_KV_HEADS = 4
B_DH = 128
IDX_HEADS = 16
IDX_DIM = 64
TOPK_MAX = 256
PAGE = 128
C_GROUPS = 8
C_CHUNK = 128
D_CONV = 31
FF_CONV = 3
LANE = 128
VMEM_LIMIT_BYTES = 48 * 1024 * 1024
WS_VMEM_LIMIT_BYTES = 56 * 1024 * 1024
FFN_ROW_CHUNK = 1024
DSA_PAGES_PER_STEP = 8

_AQ, _AK, _AV, _AO = 0, 1024, 2048, 4096
_BQ, _BK, _BV, _IQ = 6144, 8192, 8704, 9216
_N_MAIN = 10240
_S_AI, _S_AF, _S_IW, _S_IK = 0, 8, 16, 32


def _params(sem):
    return pltpu.CompilerParams(dimension_semantics=sem, vmem_limit_bytes=VMEM_LIMIT_BYTES)


def _dot(a, b):
    return jnp.dot(a, b, preferred_element_type=F32)


def _dot_nt(a, b):
    return lax.dot_general(a, b, (((1,), (1,)), ((), ())), preferred_element_type=F32)


def _sigmoid(x):
    return 1.0 / (1.0 + jnp.exp(-x))


def _col(row):
    n = row.shape[1]
    r = lax.broadcasted_iota(I32, (n, n), 0)
    c = lax.broadcasted_iota(I32, (n, n), 1)
    return jnp.sum(jnp.where(r == c, row, 0.0), axis=1, keepdims=True)


def _rms_kernel(x_ref, g_ref, o_ref):
    x = x_ref[...]
    y = x * lax.rsqrt(jnp.mean(x * x, axis=-1, keepdims=True) + EPS)
    o_ref[...] = (y * g_ref[...]).astype(o_ref.dtype)


def _rmsnorm(x, g, out_dtype):
    m, d = x.shape
    tm = min(m, 256)
    return pl.pallas_call(
        _rms_kernel,
        grid=(m // tm,),
        in_specs=[pl.BlockSpec((tm, d), lambda i: (i, 0)), pl.BlockSpec((1, d), lambda i: (0, 0))],
        out_specs=pl.BlockSpec((tm, d), lambda i: (i, 0)),
        out_shape=jax.ShapeDtypeStruct((m, d), out_dtype),
        compiler_params=_params(("parallel",)),
        name="rmsnorm",
    )(x, g.reshape(1, d))


def _mm_kernel(*refs, nk, has_res):
    if has_res:
        x_ref, w_ref, r_ref, o_ref = refs[:4]
        scratch = refs[4:]
    else:
        x_ref, w_ref, o_ref = refs[:3]
        r_ref = None
        scratch = refs[3:]
    if nk == 1:
        acc = _dot(x_ref[...], w_ref[...])
        if has_res:
            acc = acc + r_ref[...]
        o_ref[...] = acc.astype(o_ref.dtype)
        return
    acc_ref, = scratch
    k = pl.program_id(2)

    @pl.when(k == 0)
    def _():
        acc_ref[...] = jnp.zeros_like(acc_ref)

    acc_ref[...] += _dot(x_ref[...], w_ref[...])

    @pl.when(k == nk - 1)
    def _():
        acc = acc_ref[...]
        if has_res:
            acc = acc + r_ref[...]
        o_ref[...] = acc.astype(o_ref.dtype)


def _matmul(x, w, layer=0, res=None, out_dtype=F32, tm=1024, tn=512, tk=None):
    m, kdim = x.shape
    n = w.shape[2]
    tm = min(tm, m)
    tn = min(tn, n)
    tk = kdim if tk is None else tk
    assert m % tm == 0 and n % tn == 0 and kdim % tk == 0
    nk = kdim // tk
    in_specs = [pl.BlockSpec((tm, tk), lambda i, j, k: (i, k)),
                pl.BlockSpec((None, tk, tn), lambda i, j, k: (layer, k, j))]
    args = [x, w]
    if res is not None:
        in_specs.append(pl.BlockSpec((tm, tn), lambda i, j, k: (i, j)))
        args.append(res)
    return pl.pallas_call(
        functools.partial(_mm_kernel, nk=nk, has_res=res is not None),
        grid=(m // tm, n // tn, nk),
        in_specs=in_specs,
        out_specs=pl.BlockSpec((tm, tn), lambda i, j, k: (i, j)),
        out_shape=jax.ShapeDtypeStruct((m, n), out_dtype),
        scratch_shapes=[pltpu.VMEM((tm, tn), F32)] if nk > 1 else [],
        compiler_params=_params(("parallel", "parallel", "arbitrary")),
        name="matmul",
    )(*args)


def _mm_ws_kernel(*refs, has_res, emit_bf16):
    refs = list(refs)
    x_ref, xs_ref, w_ref = refs[:3]
    r_ref, rs_ref = refs[3:5] if has_res else (None, None)
    outs = refs[5:] if has_res else refs[3:]
    o_ref, os_ref = outs[:2]
    ob_ref = outs[2] if emit_bf16 else None
    wb_ref = outs[-1]

    @pl.when(pl.program_id(1) == 0)
    def _():
        wb_ref[...] = w_ref[...].astype(BF16)
        acc_s = _dot(xs_ref[...], wb_ref[...])
        os_ref[...] = acc_s + rs_ref[...] if has_res else acc_s

    acc = _dot(x_ref[...], wb_ref[...])
    if has_res:
        acc = acc + r_ref[...]
    o_ref[...] = acc
    if emit_bf16:
        ob_ref[...] = acc.astype(BF16)


def _matmul_ws(x, xs, w, layer, col0, n, res=None, res_s=None, emit_bf16=False, tm=1024, tn=512):
    m, kdim = x.shape
    ms = xs.shape[0]
    assert m % tm == 0 and n % tn == 0 and col0 % tn == 0
    c0 = col0 // tn
    has_res = res is not None
    out_specs = [pl.BlockSpec((tm, tn), lambda j, i: (i, j)), pl.BlockSpec((ms, tn), lambda j, i: (0, j))]
    out_shape = [jax.ShapeDtypeStruct((m, n), F32), jax.ShapeDtypeStruct((ms, n), F32)]
    if emit_bf16:
        out_specs.append(pl.BlockSpec((tm, tn), lambda j, i: (i, j)))
        out_shape.append(jax.ShapeDtypeStruct((m, n), BF16))
    in_specs = [pl.BlockSpec((tm, kdim), lambda j, i: (i, 0)),
                pl.BlockSpec((ms, kdim), lambda j, i: (0, 0)),
                pl.BlockSpec((None, kdim, tn), lambda j, i: (layer, 0, c0 + j))]
    args = [x, xs, w]
    if has_res:
        in_specs += [pl.BlockSpec((tm, tn), lambda j, i: (i, j)), pl.BlockSpec((ms, tn), lambda j, i: (0, j))]
        args += [res, res_s]
    return pl.pallas_call(
        functools.partial(_mm_ws_kernel, has_res=has_res, emit_bf16=emit_bf16),
        grid=(n // tn, m // tm),
        in_specs=in_specs,
        out_specs=tuple(out_specs),
        out_shape=tuple(out_shape),
        scratch_shapes=[pltpu.VMEM((kdim, tn), BF16)],
        compiler_params=pltpu.CompilerParams(dimension_semantics=("parallel", "arbitrary"),
                                             vmem_limit_bytes=WS_VMEM_LIMIT_BYTES),
        name="matmul_ws",
    )(*args)


def _matmul_xres(x, w, layer, res, tm=512, tn=256):
    m, kdim = x.shape
    n = w.shape[2]
    assert m % tm == 0 and n % tn == 0
    return pl.pallas_call(
        functools.partial(_mm_kernel, nk=1, has_res=True),
        grid=(m // tm, n // tn),
        in_specs=[pl.BlockSpec((tm, kdim), lambda i, j: (i, 0)),
                  pl.BlockSpec((None, kdim, tn), lambda i, j: (layer, 0, j)),
                  pl.BlockSpec((tm, tn), lambda i, j: (i, j))],
        out_specs=pl.BlockSpec((tm, tn), lambda i, j: (i, j)),
        out_shape=jax.ShapeDtypeStruct((m, n), F32),
        compiler_params=_params(("parallel", "arbitrary")),
        name="matmul_xres",
    )(x, w, res)


def _conv_rows(h, prev, w, b):
    def taps(x, shifted):
        y = w[FF_CONV - 1:FF_CONV, :] * x + b
        for s in range(1, FF_CONV):
            y = y + w[FF_CONV - 1 - s:FF_CONV - s, :] * shifted(s)
        return y

    bulk = taps(h, lambda s: pltpu.roll(h, s, 0))
    h8 = h[0:8, :]
    row = lax.broadcasted_iota(I32, h8.shape, 0)

    def head_shift(s):
        x = pltpu.roll(h8, s, 0)
        for r in range(s):
            x = jnp.where(row == r, prev[8 - s + r:8 - s + r + 1, :], x)
        return x

    return jnp.concatenate([taps(h8, head_shift), bulk[8:, :]], axis=0)


def _ffn_up_kernel(x_ref, xs_ref, wg_ref, wv_ref, cg_ref, cv_ref, bg_ref, bv_ref,
                   act_ref, tg_ref, tv_ref, sg_ref, sv_ref, wb_ref, carry_ref, *, tn, tiles_per_seq):
    i = pl.program_id(1)

    @pl.when(i == 0)
    def _():
        wb_ref[:, 0:tn] = wg_ref[...].astype(BF16)
        wb_ref[:, tn:2 * tn] = wv_ref[...].astype(BF16)
        hs = _dot(xs_ref[...], wb_ref[...])
        sg_ref[...] = hs[:, 0:tn]
        sv_ref[...] = hs[:, tn:2 * tn]

    @pl.when(i % tiles_per_seq == 0)
    def _():
        carry_ref[...] = jnp.zeros_like(carry_ref)

    tm = x_ref.shape[0]
    rc = min(tm, FFN_ROW_CHUNK)
    prev = carry_ref[...]
    for r in range(tm // rc):
        h = _dot(x_ref[r * rc:(r + 1) * rc, :], wb_ref[...])
        g = _conv_rows(h[:, 0:tn], prev[:, 0:tn], cg_ref[...], bg_ref[...])
        v = _conv_rows(h[:, tn:2 * tn], prev[:, tn:2 * tn], cv_ref[...], bv_ref[...])
        act_ref[r * rc:(r + 1) * rc, :] = (g * _sigmoid(g) * v).astype(act_ref.dtype)
        prev = h[rc - 8:rc, :]
    carry_ref[...] = prev

    @pl.when(i % tiles_per_seq == tiles_per_seq - 1)
    def _():
        tg_ref[0] = prev[:, 0:tn]
        tv_ref[0] = prev[:, tn:2 * tn]


def _ffn_up(x, xs, w_up, w_fc, b_fc, layer, seq_len, tm=1024, tn=256):
    m, kdim = x.shape
    ms = xs.shape[0]
    f = w_up.shape[2] // 2
    nf = f // tn
    tm = min(tm, seq_len)
    assert m % tm == 0 and seq_len % tm == 0 and f % tn == 0
    tiles_per_seq = seq_len // tm
    nseq = m // seq_len
    return pl.pallas_call(
        functools.partial(_ffn_up_kernel, tn=tn, tiles_per_seq=tiles_per_seq),
        grid=(nf, m // tm),
        in_specs=[pl.BlockSpec((tm, kdim), lambda j, i: (i, 0)),
                  pl.BlockSpec((ms, kdim), lambda j, i: (0, 0)),
                  pl.BlockSpec((None, kdim, tn), lambda j, i: (layer, 0, j)),
                  pl.BlockSpec((None, kdim, tn), lambda j, i: (layer, 0, j + nf)),
                  pl.BlockSpec((None, FF_CONV, tn), lambda j, i: (layer, 0, j)),
                  pl.BlockSpec((None, FF_CONV, tn), lambda j, i: (layer, 0, j + nf)),
                  pl.BlockSpec((None, 1, tn), lambda j, i: (layer, 0, j)),
                  pl.BlockSpec((None, 1, tn), lambda j, i: (layer, 0, j + nf))],
        out_specs=(pl.BlockSpec((tm, tn), lambda j, i: (i, j)),
                   pl.BlockSpec((1, 8, tn), lambda j, i: (i // tiles_per_seq, 0, j)),
                   pl.BlockSpec((1, 8, tn), lambda j, i: (i // tiles_per_seq, 0, j)),
                   pl.BlockSpec((ms, tn), lambda j, i: (0, j)),
                   pl.BlockSpec((ms, tn), lambda j, i: (0, j))),
        out_shape=(jax.ShapeDtypeStruct((m, f), BF16),
                   jax.ShapeDtypeStruct((nseq, 8, f), F32), jax.ShapeDtypeStruct((nseq, 8, f), F32),
                   jax.ShapeDtypeStruct((ms, f), F32), jax.ShapeDtypeStruct((ms, f), F32)),
        scratch_shapes=[pltpu.VMEM((kdim, 2 * tn), BF16), pltpu.VMEM((8, 2 * tn), F32)],
        compiler_params=pltpu.CompilerParams(dimension_semantics=("parallel", "arbitrary"),
                                             vmem_limit_bytes=WS_VMEM_LIMIT_BYTES),
        name="ffn_up",
    )(x, xs, w_up, w_up, w_fc, w_fc, b_fc, b_fc)


def _ffn_act_step_kernel(hg_ref, hv_ref, bufg_ref, bufv_ref, wg_ref, wv_ref, bg_ref, bv_ref, o_ref):
    def conv(h_ref, buf_ref, w_ref, b_ref):
        w = w_ref[...]
        y = w[FF_CONV - 1:FF_CONV, :] * h_ref[...] + b_ref[...]
        for s in range(FF_CONV - 1):
            y = y + w[s:s + 1, :] * buf_ref[s]
        return y

    g = conv(hg_ref, bufg_ref, wg_ref, bg_ref)
    o_ref[...] = g * _sigmoid(g) * conv(hv_ref, bufv_ref, wv_ref, bv_ref)


def _ffn_act_step(h_g, h_v, buf, w_fc, b_fc):
    b, f = h_g.shape
    return pl.pallas_call(
        _ffn_act_step_kernel,
        out_shape=jax.ShapeDtypeStruct((b, f), F32),
        compiler_params=pltpu.CompilerParams(vmem_limit_bytes=VMEM_LIMIT_BYTES),
        name="ffn_act_step",
    )(h_g, h_v, buf[:, :, :f], buf[:, :, f:], w_fc[:, :f], w_fc[:, f:],
      b_fc[:, :f], b_fc[:, f:])


def _scan_lanes(x, op, fill):
    n = x.shape[1]
    lane = lax.broadcasted_iota(I32, x.shape, 1)
    d = 1
    while d < n:
        x = op(x, jnp.where(lane >= d, pltpu.roll(x, d, 1), fill))
        d *= 2
    return x


def _log_sigmoid(x):
    return jnp.minimum(x, 0.0) - jnp.log(1.0 + jnp.exp(-jnp.abs(x)))


def _gate_kernel(i_ref, f_ref, bi_ref, bf_ref, a_ref, g_ref, m_ref):
    ig = i_ref[...] + bi_ref[...]
    lf = _log_sigmoid(f_ref[...] + bf_ref[...])
    bsum = _scan_lanes(lf, jnp.add, 0.0)
    a = ig - bsum
    g = jnp.maximum(_scan_lanes(a, jnp.maximum, NEG_INF), 0.0)
    a_ref[...] = a
    g_ref[...] = g
    m_ref[...] = bsum + g


def _mlstm_gates(i_t, f_t, b_i, b_f):
    shp = jax.ShapeDtypeStruct(i_t.shape, F32)
    return pl.pallas_call(
        _gate_kernel, out_shape=(shp, shp, shp),
        compiler_params=pltpu.CompilerParams(vmem_limit_bytes=VMEM_LIMIT_BYTES),
        name="mlstm_gates",
    )(i_t, f_t, b_i, b_f)


def _mlstm_kernel(q_ref, k_ref, v_ref, o_ref, a_ref, g_ref, m_ref, gh_ref, out_ref, num_sc, den_sc, *, tq, hp):
    qi = pl.program_id(2)
    off_q = pl.multiple_of(qi * tq, tq)
    row = lax.broadcasted_iota(I32, (tq, tq), 0)
    col = lax.broadcasted_iota(I32, (tq, tq), 1)
    qs = [(q_ref[0, :, hh * A_DQK:(hh + 1) * A_DQK] * (A_DQK ** -0.5)).astype(BF16) for hh in range(hp)]
    g_cs = [_col(g_ref[0, hh, :, pl.ds(off_q, tq)]) for hh in range(hp)]
    num_sc[...] = jnp.zeros_like(num_sc)
    den_sc[...] = jnp.zeros_like(den_sc)

    def body(kb, carry):
        off = pl.multiple_of(kb * tq, tq)
        causal = (col + kb * tq) <= (row + qi * tq)
        for hh in range(hp):
            k = k_ref[0, pl.ds(off, tq), hh * A_DQK:(hh + 1) * A_DQK]
            v = v_ref[0, pl.ds(off, tq), hh * A_DV:(hh + 1) * A_DV]
            a = a_ref[0, hh, :, pl.ds(off, tq)]
            w = jnp.exp(jnp.where(causal, a - g_cs[hh], NEG_INF))
            s = _dot_nt(qs[hh], k) * w
            num_sc[hh] += _dot(s.astype(BF16), v)
            den_sc[hh] += jnp.sum(s, axis=1, keepdims=True)
        return carry

    lax.fori_loop(0, qi + 1, body, 0)
    for hh in range(hp):
        m_c = _col(m_ref[0, hh, :, pl.ds(off_q, tq)])
        h = num_sc[hh] / jnp.maximum(jnp.abs(den_sc[hh]), jnp.exp(-m_c))
        h = h * lax.rsqrt(jnp.mean(h * h, axis=-1, keepdims=True) + EPS) * gh_ref[hh]
        cols = slice(hh * A_DV, (hh + 1) * A_DV)
        out_ref[0, :, cols] = (_sigmoid(o_ref[0, :, cols]) * h).astype(out_ref.dtype)


def _mlstm_prompt(p3, pb3, a, g, m, g_head):
    b, t, _ = p3.shape
    tq = 256
    hp = 4
    row_spec = pl.BlockSpec((1, hp, 1, t), lambda i, h, j: (i, h, 0, 0))
    wq, wv = hp * A_DQK, hp * A_DV
    return pl.pallas_call(
        functools.partial(_mlstm_kernel, tq=tq, hp=hp),
        grid=(b, A_HEADS // hp, t // tq),
        in_specs=[pl.BlockSpec((1, tq, wq), lambda i, h, j: (i, j, _AQ // wq + h)),
                  pl.BlockSpec((1, t, wq), lambda i, h, j: (i, 0, _AK // wq + h)),
                  pl.BlockSpec((1, t, wv), lambda i, h, j: (i, 0, _AV // wv + h)),
                  pl.BlockSpec((1, tq, wv), lambda i, h, j: (i, j, _AO // wv + h)),
                  row_spec, row_spec, row_spec,
                  pl.BlockSpec((hp, 1, A_DV), lambda i, h, j: (h, 0, 0))],
        out_specs=pl.BlockSpec((1, tq, wv), lambda i, h, j: (i, j, h)),
        out_shape=jax.ShapeDtypeStruct((b, t, A_HEADS * A_DV), BF16),
        scratch_shapes=[pltpu.VMEM((hp, tq, A_DV), F32), pltpu.VMEM((hp, tq, 1), F32)],
        compiler_params=_params(("parallel", "parallel", "arbitrary")),
        name="mlstm_prompt",
    )(p3, pb3, pb3, p3, a, g, m, g_head)


def _mlstm_state_kernel(k_ref, v_ref, a_ref, g_ref, c_ref, n_ref, *, t, tc):
    g_last = g_ref[0, 0, :, t - 1:t]

    def body(cb, carry):
        c_acc, n_acc = carry
        off = pl.multiple_of(cb * tc, tc)
        w = _col(jnp.exp(a_ref[0, 0, :, pl.ds(off, tc)] - g_last))
        kw = k_ref[0, pl.ds(off, tc), :] * w
        vt = v_ref[0, pl.ds(off, tc), :].T.astype(BF16)
        return c_acc + _dot(vt, kw.astype(BF16)), n_acc + jnp.sum(kw, axis=0, keepdims=True)

    c_acc, n_acc = lax.fori_loop(0, t // tc, body,
                                 (jnp.zeros((A_DV, A_DQK), F32), jnp.zeros((1, A_DQK), F32)))
    c_ref[0, 0] = c_acc
    n_ref[0, 0] = n_acc


def _mlstm_prompt_state(p3, a, g):
    b, t, _ = p3.shape
    row_spec = pl.BlockSpec((1, 1, 1, t), lambda i, h: (i, h, 0, 0))
    return pl.pallas_call(
        functools.partial(_mlstm_state_kernel, t=t, tc=256),
        grid=(b, A_HEADS),
        in_specs=[pl.BlockSpec((1, t, A_DQK), lambda i, h: (i, 0, _AK // A_DQK + h)),
                  pl.BlockSpec((1, t, A_DV), lambda i, h: (i, 0, _AV // A_DV + h)),
                  row_spec, row_spec],
        out_specs=(pl.BlockSpec((1, 1, A_DV, A_DQK), lambda i, h: (i, h, 0, 0)),
                   pl.BlockSpec((1, 1, 1, A_DQK), lambda i, h: (i, h, 0, 0))),
        out_shape=(jax.ShapeDtypeStruct((b, A_HEADS, A_DV, A_DQK), F32),
                   jax.ShapeDtypeStruct((b, A_HEADS, 1, A_DQK), F32)),
        compiler_params=_params(("parallel", "parallel")),
        name="mlstm_prompt_state",
    )(p3, p3, a, g)


def _mlstm_step_kernel(q_ref, k_ref, v_ref, o_ref, gh_ref, ig_ref, fp_ref, c0_ref, n0_ref, m0_ref,
                       h_ref, c_ref, n_ref, m_ref):
    q = q_ref[0, 0] * (A_DQK ** -0.5)
    k = k_ref[0, 0]
    v = v_ref[0, 0]
    ig = ig_ref[0, 0]
    lf = _log_sigmoid(fp_ref[0, 0])
    c0 = c0_ref[0, 0]
    n0 = n0_ref[0, 0]
    m0 = m0_ref[0, 0]
    log_inter = lf + m0
    m_t = jnp.maximum(log_inter, ig)
    w_intra = jnp.exp(ig - m_t)
    w_inter = jnp.exp(log_inter - m_t)
    s = jnp.sum(q * k, axis=1, keepdims=True) * w_intra
    num = s * v + w_inter * jnp.sum(c0 * q, axis=1, keepdims=True)
    den = s + w_inter * jnp.sum(n0 * q, axis=1, keepdims=True)
    h = num / jnp.maximum(jnp.abs(den), jnp.exp(-m_t))
    h = h * lax.rsqrt(jnp.mean(h * h, axis=0, keepdims=True) + EPS) * gh_ref[0]
    h_ref[0, 0] = _sigmoid(o_ref[0, 0]) * h
    c_ref[0, 0] = w_inter * c0 + w_intra * (v * k)
    n_ref[0, 0] = w_inter * n0 + w_intra * k
    m_ref[0, 0] = m_t


def _mlstm_step(q, k, v_col, o_col, gh_col, ig, f_pre, c0, n0, m0):
    b = q.shape[0]
    h = A_HEADS
    row = pl.BlockSpec((1, 1, 1, A_DQK), lambda i, j: (i, j, 0, 0))
    colv = pl.BlockSpec((1, 1, A_DV, 1), lambda i, j: (i, j, 0, 0))
    one = pl.BlockSpec((1, 1, 1, 1), lambda i, j: (i, j, 0, 0))
    mat = pl.BlockSpec((1, 1, A_DV, A_DQK), lambda i, j: (i, j, 0, 0))
    return pl.pallas_call(
        _mlstm_step_kernel,
        grid=(b, h),
        in_specs=[row, row, colv, colv, pl.BlockSpec((1, A_DV, 1), lambda i, j: (j, 0, 0)),
                  one, one, mat, row, one],
        out_specs=(colv, mat, row, one),
        out_shape=(jax.ShapeDtypeStruct((b, h, A_DV, 1), F32),
                   jax.ShapeDtypeStruct((b, h, A_DV, A_DQK), F32),
                   jax.ShapeDtypeStruct((b, h, 1, A_DQK), F32),
                   jax.ShapeDtypeStruct((b, h, 1, 1), F32)),
        compiler_params=_params(("parallel", "parallel")),
        name="mlstm_step",
    )(q, k, v_col, o_col, gh_col, ig, f_pre, c0, n0, m0)


def _sort_key(x):
    x = jnp.where(x == 0.0, 0.0, x)
    bits = pltpu.bitcast(x, I32)
    return jnp.where(bits >= 0, bits, bits ^ jnp.int32(0x7FFFFFFF))


def _kth_largest_key(key, k):
    def count_ge(c):
        return jnp.sum(jnp.where(key >= c, 1.0, 0.0), axis=1, keepdims=True)

    kf = jnp.float32(k)
    thr = jnp.where(count_ge(jnp.int32(0)) >= kf, jnp.int32(0), jnp.int32(INT_MIN))
    thr = jnp.broadcast_to(thr, (key.shape[0], 1))

    def body(i, thr):
        cand = thr + lax.shift_left(jnp.int32(1), jnp.int32(30) - i)
        return jnp.where(count_ge(cand) >= kf, cand, thr)

    return lax.fori_loop(0, 31, body, thr)


def _tri_ones(n, dtype):
    r = lax.broadcasted_iota(I32, (n, n), 0)
    c = lax.broadcasted_iota(I32, (n, n), 1)
    return jnp.where(r <= c, 1.0, 0.0).astype(dtype)


def _select_topk(scores, visible, k):
    key = _sort_key(jnp.where(visible, scores, NEG_INF))
    thr = _kth_largest_key(key, k)
    gt = key > thr
    eq = jnp.logical_and(key == thr, visible)
    need = jnp.float32(k) - jnp.sum(jnp.where(gt, 1.0, 0.0), axis=1, keepdims=True)
    tri = _tri_ones(LANE, BF16)
    off = jnp.zeros((scores.shape[0], 1), F32)
    rank = []
    eqf = jnp.where(eq, 1.0, 0.0)
    for c in range(scores.shape[1] // LANE):
        p = _dot(eqf[:, c * LANE:(c + 1) * LANE].astype(BF16), tri) + off
        rank.append(p)
        off = p[:, LANE - 1:LANE]
    rank = jnp.concatenate(rank, axis=1)
    return jnp.logical_or(jnp.logical_and(gt, visible), jnp.logical_and(eq, rank <= need))


def _indexer_scores(qi_head, wi, ki):
    acc = None
    for h in range(IDX_HEADS):
        s = _dot_nt(qi_head(h).astype(BF16), ki) * (IDX_DIM ** -0.5)
        term = jnp.maximum(s, 0.0) * wi[:, h:h + 1]
        acc = term if acc is None else acc + term
    return acc * (IDX_HEADS ** -0.5)


def _dsa_prompt_kernel(q_ref, k_ref, v_ref, qi_ref, sm_ref, smq_ref, o_ref, sc_ref, *, tq, topk, extents):
    j = pl.program_id(1)
    group = B_HEADS // B_KV_HEADS
    lane = lax.broadcasted_iota(I32, (tq, LANE), 1)

    def attend(e):
        ki = sm_ref[0, 0:e, _S_IK:_S_IK + IDX_DIM].astype(BF16)
        sc_ref[:, 0:e] = jnp.zeros((tq, e), F32)

        def idx_pair(hp, carry):
            qpair = qi_ref[0, :, pl.ds(pl.multiple_of(hp * LANE, LANE), LANE)]
            for half in range(LANE // IDX_DIM):
                s = _dot_nt(qpair[:, half * IDX_DIM:(half + 1) * IDX_DIM], ki) * (IDX_DIM ** -0.5)
                head = hp * (LANE // IDX_DIM) + half
                w = jnp.sum(jnp.where(lane == _S_IW + head, smq_ref[0], 0.0), axis=1, keepdims=True)
                sc_ref[:, 0:e] += jnp.maximum(s, 0.0) * w
            return carry

        lax.fori_loop(0, IDX_HEADS * IDX_DIM // LANE, idx_pair, 0)
        scores = sc_ref[:, 0:e] * (IDX_HEADS ** -0.5)
        qpos = lax.broadcasted_iota(I32, (tq, e), 0) + j * tq
        kpos = lax.broadcasted_iota(I32, (tq, e), 1)
        sel = _select_topk(scores, kpos <= qpos, topk)
        sc_ref[:, 0:e] = jnp.where(sel, 0.0, NEG_INF)

        def kv_head(n, carry):
            kv_off = pl.multiple_of(n * B_DH, B_DH)
            kn = k_ref[0, 0:e, pl.ds(kv_off, B_DH)]
            vn = v_ref[0, 0:e, pl.ds(kv_off, B_DH)]
            for gidx in range(group):
                q_off = pl.multiple_of((n * group + gidx) * B_DH, B_DH)
                s = _dot_nt(q_ref[0, :, pl.ds(q_off, B_DH)], kn) * (B_DH ** -0.5) + sc_ref[:, 0:e]
                p = jnp.exp(s - jnp.max(s, axis=1, keepdims=True))
                l = jnp.sum(p, axis=1, keepdims=True)
                o_ref[0, :, pl.ds(q_off, B_DH)] = (_dot(p.astype(BF16), vn) / l).astype(o_ref.dtype)
            return carry

        lax.fori_loop(0, B_KV_HEADS, kv_head, 0)

    last = (j + 1) * tq
    for idx, e in enumerate(extents):
        lo = extents[idx - 1] if idx else 0
        pl.when(jnp.logical_and(last > lo, last <= e))(functools.partial(attend, e))


def _dsa_prompt(pb3, small3):
    b, t, _ = pb3.shape
    tq = 256
    topk = min(TOPK_MAX, t // 4)
    n_ext = min(4, t // tq)
    extents = tuple(t * (i + 1) // n_ext for i in range(n_ext))
    assert all(e % tq == 0 for e in extents) and extents[0] >= topk
    w_q, w_kv, w_qi = B_HEADS * B_DH, B_KV_HEADS * B_DH, IDX_HEADS * IDX_DIM
    return pl.pallas_call(
        functools.partial(_dsa_prompt_kernel, tq=tq, topk=topk, extents=extents),
        grid=(b, t // tq),
        in_specs=[pl.BlockSpec((1, tq, w_q), lambda i, j: (i, j, _BQ // w_q)),
                  pl.BlockSpec((1, t, w_kv), lambda i, j: (i, 0, _BK // w_kv)),
                  pl.BlockSpec((1, t, w_kv), lambda i, j: (i, 0, _BV // w_kv)),
                  pl.BlockSpec((1, tq, w_qi), lambda i, j: (i, j, _IQ // w_qi)),
                  pl.BlockSpec((1, t, LANE), lambda i, j: (i, 0, 0)),
                  pl.BlockSpec((1, tq, LANE), lambda i, j: (i, j, 0))],
        out_specs=pl.BlockSpec((1, tq, w_q), lambda i, j: (i, j, 0)),
        out_shape=jax.ShapeDtypeStruct((b, t, w_q), BF16),
        scratch_shapes=[pltpu.VMEM((tq, t), F32)],
        compiler_params=_params(("parallel", "parallel")),
        name="dsa_prompt",
    )(pb3, pb3, pb3, pb3, small3, small3)


def _page_specs(block, layer, pages_per_step):
    def spec(r):
        zeros = (0,) * (len(block) - 2)
        return pl.BlockSpec(block, lambda i, p, pt: (layer, pt[i, p * pages_per_step + r]) + zeros)
    return [spec(r) for r in range(pages_per_step)]


def _dsa_step_scores_kernel(pt_ref, qi_ref, wi_ref, *refs):
    del pt_ref
    ki_refs, o_ref = refs[:-1], refs[-1]
    qi = qi_ref[0].astype(BF16)
    ki = jnp.concatenate([r[0] for r in ki_refs], axis=0).astype(BF16)
    s = _dot_nt(qi, ki) * (IDX_DIM ** -0.5)
    o_ref[0] = jnp.sum(jnp.maximum(s, 0.0) * wi_ref[0], axis=0, keepdims=True) * (IDX_HEADS ** -0.5)


def _dsa_step_scores(page_table, qi, wi_col, cache_ki, layer):
    b, n_pages = page_table.shape
    g = math.gcd(n_pages, DSA_PAGES_PER_STEP)
    grid_spec = pltpu.PrefetchScalarGridSpec(
        num_scalar_prefetch=1,
        grid=(b, n_pages // g),
        in_specs=[pl.BlockSpec((1, IDX_HEADS, IDX_DIM), lambda i, p, pt: (i, 0, 0)),
                  pl.BlockSpec((1, IDX_HEADS, 1), lambda i, p, pt: (i, 0, 0))]
                 + _page_specs((None, 1, PAGE, IDX_DIM), layer, g),
        out_specs=pl.BlockSpec((1, 1, g * PAGE), lambda i, p, pt: (i, 0, p)),
    )
    return pl.pallas_call(
        _dsa_step_scores_kernel,
        grid_spec=grid_spec,
        out_shape=jax.ShapeDtypeStruct((b, 1, n_pages * PAGE), F32),
        compiler_params=_params(("parallel", "arbitrary")),
        name="dsa_step_scores",
    )(page_table, qi, wi_col, *([cache_ki] * g))


def _dsa_step_select_kernel(sc_ref, self_ref, sel_ref, selself_ref, *, topk):
    sc = sc_ref[...]
    sself = self_ref[...]
    key = _sort_key(sc)
    key_self = _sort_key(sself)
    kf = jnp.float32(topk)

    def count_ge(c):
        return (jnp.sum(jnp.where(key >= c, 1.0, 0.0), axis=1, keepdims=True)
                + jnp.where(key_self >= c, 1.0, 0.0))

    thr = jnp.where(count_ge(jnp.int32(0)) >= kf, jnp.int32(0), jnp.int32(INT_MIN))

    def body(i, thr):
        cand = thr + lax.shift_left(jnp.int32(1), jnp.int32(30) - i)
        return jnp.where(count_ge(cand) >= kf, cand, thr)

    thr = lax.fori_loop(0, 31, body, thr)
    gt = key > thr
    eq = key == thr
    n_gt = jnp.sum(jnp.where(gt, 1.0, 0.0), axis=1, keepdims=True) + jnp.where(key_self > thr, 1.0, 0.0)
    need = kf - n_gt
    rank = _scan_lanes(jnp.where(eq, 1.0, 0.0), jnp.add, 0.0)
    sel_ref[...] = jnp.where(jnp.logical_or(gt, jnp.logical_and(eq, rank <= need)), 1.0, 0.0)
    n_eq = rank[:, -1:]
    selself_ref[...] = jnp.where(
        jnp.logical_or(key_self > thr, jnp.logical_and(key_self == thr, n_eq < need)), 1.0, 0.0)


def _dsa_step_select(scores, self_score, topk):
    b, s = scores.shape
    return pl.pallas_call(
        functools.partial(_dsa_step_select_kernel, topk=topk),
        out_shape=(jax.ShapeDtypeStruct((b, s), F32), jax.ShapeDtypeStruct((b, 1), F32)),
        compiler_params=pltpu.CompilerParams(vmem_limit_bytes=VMEM_LIMIT_BYTES),
        name="dsa_step_select",
    )(scores, self_score)


def _dsa_step_attend_kernel(pt_ref, q_ref, sel_ref, selself_ref, kn_ref, vn_ref, *refs, pages_per_step):
    del pt_ref
    g = pages_per_step
    k_refs, v_refs = refs[:g], refs[g:2 * g]
    o_ref, m_sc, l_sc, acc_sc = refs[2 * g:]
    p_idx = pl.program_id(1)
    group = B_HEADS // B_KV_HEADS

    @pl.when(p_idx == 0)
    def _():
        m_sc[...] = jnp.full_like(m_sc, NEG_INF)
        l_sc[...] = jnp.zeros_like(l_sc)
        acc_sc[...] = jnp.zeros_like(acc_sc)

    q = q_ref[0].astype(BF16)
    kv_of_row = lax.broadcasted_iota(I32, (B_HEADS, 1), 0) // group

    def head_rows(page_refs, n):
        return jnp.concatenate([r[0, :, n, :] for r in page_refs], axis=0).astype(BF16)

    def update(k_pages, v_pages, sel_row):
        s = jnp.zeros((B_HEADS, sel_row.shape[1]), F32)
        for n in range(B_KV_HEADS):
            s = jnp.where(kv_of_row == n, _dot_nt(q, head_rows(k_pages, n)), s)
        s = s * (B_DH ** -0.5) + jnp.where(sel_row > 0.0, 0.0, NEG_INF)
        m_old = m_sc[...]
        m_new = jnp.maximum(m_old, jnp.max(s, axis=1, keepdims=True))
        m_safe = jnp.where(m_new == NEG_INF, 0.0, m_new)
        alpha = jnp.exp(m_old - m_safe)
        p = jnp.exp(s - m_safe)
        pb = p.astype(BF16)
        pv = jnp.zeros((B_HEADS, B_DH), F32)
        for n in range(B_KV_HEADS):
            pv = jnp.where(kv_of_row == n, _dot(pb, head_rows(v_pages, n)), pv)
        l_sc[...] = alpha * l_sc[...] + jnp.sum(p, axis=1, keepdims=True)
        acc_sc[...] = alpha * acc_sc[...] + pv
        m_sc[...] = m_new

    update(k_refs, v_refs, sel_ref[0])

    @pl.when(p_idx == pl.num_programs(1) - 1)
    def _():
        update([kn_ref], [vn_ref], selself_ref[0])
        o_ref[0] = acc_sc[...] / l_sc[...]


def _dsa_step_attend(page_table, q, sel, sel_self, k_new, v_new, cache_k, cache_v, layer):
    b, n_pages = page_table.shape
    g = math.gcd(n_pages, DSA_PAGES_PER_STEP)
    own = pl.BlockSpec((1, PAGE, B_KV_HEADS, B_DH), lambda i, p, pt: (i, 0, 0, 0))
    page_block = (None, 1, PAGE, B_KV_HEADS, B_DH)
    grid_spec = pltpu.PrefetchScalarGridSpec(
        num_scalar_prefetch=1,
        grid=(b, n_pages // g),
        in_specs=[pl.BlockSpec((1, B_HEADS, B_DH), lambda i, p, pt: (i, 0, 0)),
                  pl.BlockSpec((1, 1, g * PAGE), lambda i, p, pt: (i, 0, p)),
                  pl.BlockSpec((1, 1, PAGE), lambda i, p, pt: (i, 0, 0)),
                  own, own]
                 + _page_specs(page_block, layer, g) + _page_specs(page_block, layer, g),
        out_specs=pl.BlockSpec((1, B_HEADS, B_DH), lambda i, p, pt: (i, 0, 0)),
        scratch_shapes=[pltpu.VMEM((B_HEADS, 1), F32), pltpu.VMEM((B_HEADS, 1), F32),
                        pltpu.VMEM((B_HEADS, B_DH), F32)],
    )
    return pl.pallas_call(
        functools.partial(_dsa_step_attend_kernel, pages_per_step=g),
        grid_spec=grid_spec,
        out_shape=jax.ShapeDtypeStruct((b, B_HEADS, B_DH), F32),
        compiler_params=_params(("parallel", "arbitrary")),
        name="dsa_step_attend",
    )(page_table, q, sel, sel_self, k_new, v_new, *([cache_k] * g), *([cache_v] * g))


def _layernorm(x, g, b):
    xc = x - jnp.mean(x, axis=-1, keepdims=True)
    return xc * lax.rsqrt(jnp.mean(xc * xc, axis=-1, keepdims=True) + EPS) * g + b


def _odd_mix_kernel(u_ref, v_ref, a_ref, gt_ref, gv_ref, bv_ref, wsp_ref, bsp_ref, wdc_ref, bdc_ref,
                    gdn_ref, bdn_ref, o_ref, tail_ref, xbuf, slab, *, tt, width, halo):
    j = pl.program_id(1)
    nj = pl.num_programs(1)

    @pl.when(j == 0)
    def _():
        xbuf[0:halo, :] = jnp.zeros((halo, width), F32)
        xbuf[halo + tt:halo + tt + 8, :] = jnp.zeros((8, width), F32)

    vn = _layernorm(v_ref[0], gv_ref[...], bv_ref[...]).astype(BF16)
    r = lax.broadcasted_iota(I32, (tt, tt), 0)
    c = lax.broadcasted_iota(I32, (tt, tt), 1)
    gw = width // C_GROUPS
    for gi in range(C_GROUPS):
        wm = jnp.where(c <= r, wsp_ref[gi], 0.0).astype(BF16)
        sg = _dot(wm, vn[:, gi * gw:(gi + 1) * gw]) + bsp_ref[gi]
        o_ref[0, :, gi * gw:(gi + 1) * gw] = (u_ref[0, :, gi * gw:(gi + 1) * gw] * sg).astype(o_ref.dtype)

    glu = a_ref[0] * _sigmoid(gt_ref[0])
    xbuf[halo:halo + tt, :] = glu
    wdc = wdc_ref[...]
    dc = jnp.zeros((tt, width), F32) + bdc_ref[...]
    base = halo - (D_CONV - 1)
    for b8 in range(8):
        taps = [kk for kk in range(D_CONV) if (base + kk) % 8 == b8]
        src = xbuf
        if b8:
            slab[...] = xbuf[b8:b8 + tt + halo, :]
            src = slab
        for kk in taps:
            a8 = (base + kk) - b8
            dc = dc + wdc[kk:kk + 1, :] * src[a8:a8 + tt, :]
    d = _layernorm(dc, gdn_ref[...], bdn_ref[...])
    o_ref[0, :, width:2 * width] = (d * _sigmoid(d)).astype(o_ref.dtype)

    xbuf[0:halo, :] = glu[tt - halo:tt, :]

    @pl.when(j == nj - 1)
    def _():
        tail_ref[0] = glu[tt - halo:tt, :]


def _odd_mix(p3, g_v, b_v, w_sp, b_sp, w_dc, b_dc, g_dn, b_dn):
    b, t, w4 = p3.shape
    width = w4 // 4
    tt = C_CHUNK
    halo = 32
    vec = lambda x: x.reshape(1, width)
    vspec = pl.BlockSpec((1, width), lambda i, j: (0, 0))
    return pl.pallas_call(
        functools.partial(_odd_mix_kernel, tt=tt, width=width, halo=halo),
        grid=(b, t // tt),
        in_specs=[pl.BlockSpec((1, tt, width), lambda i, j: (i, j, 0)),
                  pl.BlockSpec((1, tt, width), lambda i, j: (i, j, 1)),
                  pl.BlockSpec((1, tt, width), lambda i, j: (i, j, 2)),
                  pl.BlockSpec((1, tt, width), lambda i, j: (i, j, 3)),
                  vspec, vspec,
                  pl.BlockSpec((C_GROUPS, tt, tt), lambda i, j: (0, 0, 0)),
                  pl.BlockSpec((C_GROUPS, tt, 1), lambda i, j: (0, 0, 0)),
                  pl.BlockSpec((D_CONV, width), lambda i, j: (0, 0)),
                  vspec, vspec, vspec],
        out_specs=(pl.BlockSpec((1, tt, 2 * width), lambda i, j: (i, j, 0)),
                   pl.BlockSpec((1, halo, width), lambda i, j: (i, 0, 0))),
        out_shape=(jax.ShapeDtypeStruct((b, t, 2 * width), BF16),
                   jax.ShapeDtypeStruct((b, halo, width), F32)),
        scratch_shapes=[pltpu.VMEM((halo + tt + 8, width), F32), pltpu.VMEM((halo + tt, width), F32)],
        compiler_params=_params(("parallel", "arbitrary")),
        name="odd_mix",
    )(p3, p3, p3, p3, vec(g_v), vec(b_v), w_sp, b_sp.reshape(C_GROUPS, tt, 1), w_dc, vec(b_dc),
      vec(g_dn), vec(b_dn))


def _odd_mix_step_kernel(p_ref, buf_ref, gv_ref, bv_ref, wsp_ref, bsp_ref, wdc_ref, bdc_ref,
                         gdn_ref, bdn_ref, o_ref, vn_ref, glu_ref, *, width):
    u = p_ref[:, 0:width]
    v = p_ref[:, width:2 * width]
    a = p_ref[:, 2 * width:3 * width]
    gt = p_ref[:, 3 * width:4 * width]
    vn = _layernorm(v, gv_ref[...], bv_ref[...])
    vn_ref[...] = vn
    o_ref[:, 0:width] = u * (wsp_ref[...] * vn + bsp_ref[...])
    glu = a * _sigmoid(gt)
    glu_ref[...] = glu
    wdc = wdc_ref[...]
    dc = wdc[D_CONV - 1:D_CONV, :] * glu + bdc_ref[...]
    for kk in range(D_CONV - 1):
        dc = dc + wdc[kk:kk + 1, :] * buf_ref[kk]
    d = _layernorm(dc, gdn_ref[...], bdn_ref[...])
    o_ref[:, width:2 * width] = d * _sigmoid(d)


def _odd_mix_step(p, buf, g_v, b_v, w_sp_vec, b_sp_vec, w_dc, b_dc, g_dn, b_dn):
    b, w4 = p.shape
    width = w4 // 4
    vec = lambda x: x.reshape(1, width)
    return pl.pallas_call(
        functools.partial(_odd_mix_step_kernel, width=width),
        out_shape=(jax.ShapeDtypeStruct((b, 2 * width), F32), jax.ShapeDtypeStruct((b, width), F32),
                   jax.ShapeDtypeStruct((b, width), F32)),
        compiler_params=pltpu.CompilerParams(vmem_limit_bytes=VMEM_LIMIT_BYTES),
        name="odd_mix_step",
    )(p, buf, vec(g_v), vec(b_v), vec(w_sp_vec), vec(b_sp_vec), w_dc, vec(b_dc), vec(g_dn), vec(b_dn))


def _pad_rows(x, rows):
    return jnp.pad(x, ((0, rows - x.shape[0]), (0, 0)))


def _split_even_weights(w):
    sizes = (1024, 1024, 2048, 2048, 8, 8, 2048, 512, 512, 1024, 16, 64)
    offs = [0]
    for s in sizes:
        offs.append(offs[-1] + s)
    col = lambda i: w[:, offs[i]:offs[i + 1]]
    return jnp.concatenate([col(0), col(1), col(2), col(3), col(6), col(7), col(8), col(9),
                            col(4), col(5), col(10), col(11),
                            jnp.zeros((w.shape[0], LANE - 96), w.dtype)], axis=1)


def _conv_ffn(h_p, h_s, tp, g, layer, w_up, w_fc, b_fc, w_down_b, buf_s):
    f = w_up.shape[2] // 2
    nb = buf_s.shape[0]
    xn = _rmsnorm(h_p, g, BF16)
    xs = _rmsnorm(h_s, g, BF16)
    act, tail_g, tail_v, hs_g, hs_v = _ffn_up(xn, xs, w_up, w_fc, b_fc.reshape(b_fc.shape[0], 1, 2 * f),
                                              layer, tp)
    h_p_new = _matmul_xres(act, w_down_b, layer, h_p)
    ff_p = jnp.concatenate([tail_g, tail_v], axis=-1)[:, 8 - (FF_CONV - 1):, :]
    act_s = _ffn_act_step(hs_g[:nb], hs_v[:nb], buf_s.transpose(1, 0, 2), w_fc[layer], b_fc[layer].reshape(1, 2 * f))
    h_s_new = _matmul(_pad_rows(act_s.astype(BF16), h_s.shape[0]), w_down_b, layer, res=h_s, tn=1024, tk=f // 2)
    ff_s = jnp.concatenate([buf_s[:, 1:, :], jnp.concatenate([hs_g[:nb], hs_v[:nb]], axis=-1)[:, None, :]], axis=1)
    return h_p_new, h_s_new, ff_p, ff_s


def kernel(x_prompt, x_sample, cache_k, cache_v, cache_kidx, state_mlstm_C, state_mlstm_n, state_mlstm_m,
           state_conv_d, state_ffn_conv, page_table, w_norm_mix, w_norm_ffn, w_in_even, b_gate_even,
           g_head_even, w_out_even, w_in_odd, g_vnorm, b_vnorm, w_spatial, b_spatial, w_dconv, b_dconv,
           g_dnorm, b_dnorm, w_out_odd, w_up, w_fconv, b_fconv, w_down, w_norm_final):
    bp, tp, d = x_prompt.shape
    bs = x_sample.shape[0]
    assert x_sample.shape[1] == 1
    ms = 16
    h_p = x_prompt.reshape(bp * tp, d)
    h_s = _pad_rows(x_sample.reshape(bs, d), ms)
    n_pages = page_table.shape[1]
    past = n_pages * PAGE

    w_even = _split_even_weights(w_in_even[0])[None]
    w_down_b = w_down.astype(BF16)
    b_i = b_gate_even[0, :A_HEADS]
    b_f = b_gate_even[0, A_HEADS:]
    g_head = g_head_even[0]

    xn = _rmsnorm(h_p, w_norm_mix[0], BF16)
    xs = _rmsnorm(h_s, w_norm_mix[0], BF16)
    proj, proj_s, proj_b = _matmul_ws(xn, xs, w_even, 0, 0, _N_MAIN, emit_bf16=True)
    small, small_s = _matmul_ws(xn, xs, w_even, 0, _N_MAIN, LANE, tn=LANE)
    proj_s = proj_s[:bs]
    small_s = small_s[:bs]
    p3 = proj.reshape(bp, tp, _N_MAIN)
    pb3 = proj_b.reshape(bp, tp, _N_MAIN)
    small3 = small.reshape(bp, tp, LANE)

    def heads_t(x):
        return x.transpose(0, 2, 1).reshape(bp * A_HEADS, tp)

    a_g, g_g, m_g = _mlstm_gates(heads_t(small3[:, :, _S_AI:_S_AI + A_HEADS]),
                                 heads_t(small3[:, :, _S_AF:_S_AF + A_HEADS]),
                                 jnp.tile(b_i, bp).reshape(-1, 1), jnp.tile(b_f, bp).reshape(-1, 1))
    a4 = a_g.reshape(bp, A_HEADS, 1, tp)
    g4 = g_g.reshape(bp, A_HEADS, 1, tp)
    m4 = m_g.reshape(bp, A_HEADS, 1, tp)
    h_a = _mlstm_prompt(p3, pb3, a4, g4, m4, g_head.reshape(A_HEADS, 1, A_DV))
    c_p, n_p = _mlstm_prompt_state(p3, a4, g4)
    m_p = m_g.reshape(bp, A_HEADS, tp)[:, :, -1]
    h_b = _dsa_prompt(pb3, small3)
    mix = jnp.concatenate([h_a, h_b], axis=-1).reshape(bp * tp, -1)

    k_p = p3[:, :, _BK:_BK + B_KV_HEADS * B_DH].reshape(bp, tp, B_KV_HEADS, B_DH)
    v_p = p3[:, :, _BV:_BV + B_KV_HEADS * B_DH].reshape(bp, tp, B_KV_HEADS, B_DH)
    ki_p = small3[:, :, _S_IK:_S_IK + IDX_DIM]

    q_s = proj_s[:, _AQ:_AQ + 1024].reshape(bs, A_HEADS, 1, A_DQK)
    kk_s = proj_s[:, _AK:_AK + 1024].reshape(bs, A_HEADS, 1, A_DQK)
    v_s = proj_s[:, _AV:_AV + 2048].reshape(bs, A_HEADS, A_DV, 1)
    o_s = proj_s[:, _AO:_AO + 2048].reshape(bs, A_HEADS, A_DV, 1)
    ig_s = (small_s[:, _S_AI:_S_AI + A_HEADS] + b_i).reshape(bs, A_HEADS, 1, 1)
    fp_s = (small_s[:, _S_AF:_S_AF + A_HEADS] + b_f).reshape(bs, A_HEADS, 1, 1)
    hcol_s, c_s, n_s, m_s = _mlstm_step(
        q_s, kk_s, v_s, o_s, g_head.reshape(A_HEADS, A_DV, 1), ig_s, fp_s,
        state_mlstm_C[0], state_mlstm_n[0].reshape(bs, A_HEADS, 1, A_DQK),
        state_mlstm_m[0].reshape(bs, A_HEADS, 1, 1))
    h_a_s = hcol_s.reshape(bs, A_HEADS * A_DV)

    bq_s = proj_s[:, _BQ:_BQ + 2048].reshape(bs, B_HEADS, B_DH)
    bk_s = proj_s[:, _BK:_BK + 512]
    bv_s = proj_s[:, _BV:_BV + 512]
    iq_s = proj_s[:, _IQ:_IQ + 1024].reshape(bs, IDX_HEADS, IDX_DIM)
    iw_s = small_s[:, _S_IW:_S_IW + IDX_HEADS]
    ik_s = small_s[:, _S_IK:_S_IK + IDX_DIM]
    sc_past = _dsa_step_scores(page_table, iq_s, iw_s.reshape(bs, IDX_HEADS, 1), cache_kidx, 0)
    own_page = lambda x: jnp.pad(x[:, None], ((0, 0), (0, PAGE - 1)) + ((0, 0),) * (x.ndim - 1))
    sc_self = _dsa_step_scores(jnp.arange(bs, dtype=I32).reshape(bs, 1), iq_s, iw_s.reshape(bs, IDX_HEADS, 1),
                               own_page(ik_s)[None], 0)[:, 0, 0:1]
    topk_s = min(TOPK_MAX, (past + 1) // 4)
    sel, sel_self = _dsa_step_select(sc_past.reshape(bs, past), sc_self, topk_s)
    h_b_s = _dsa_step_attend(page_table, bq_s, sel.reshape(bs, 1, past),
                             jnp.pad(sel_self, ((0, 0), (0, PAGE - 1))).reshape(bs, 1, PAGE),
                             own_page(bk_s.reshape(bs, B_KV_HEADS, B_DH)), own_page(bv_s.reshape(bs, B_KV_HEADS, B_DH)),
                             cache_k, cache_v, 0)
    mix_s = jnp.concatenate([h_a_s, h_b_s.reshape(bs, B_HEADS * B_DH)], axis=-1).astype(BF16)
    h_p, h_s = _matmul_ws(mix, _pad_rows(mix_s, ms), w_out_even, 0, 0, d, res=h_p, res_s=h_s)

    h_p, h_s, ff_p0, ff_s0 = _conv_ffn(h_p, h_s, tp, w_norm_ffn[0], 0, w_up, w_fconv, b_fconv, w_down_b,
                                       state_ffn_conv[0])

    width = w_in_odd.shape[2] // 4
    xn = _rmsnorm(h_p, w_norm_mix[1], BF16)
    xs = _rmsnorm(h_s, w_norm_mix[1], BF16)
    proj1, proj1_s = _matmul_ws(xn, xs, w_in_odd, 0, 0, 4 * width)
    proj1_s = proj1_s[:bs]
    mix1, tail = _odd_mix(proj1.reshape(bp, tp, 4 * width), g_vnorm[0], b_vnorm[0], w_spatial[0], b_spatial[0],
                          w_dconv[0], b_dconv[0], g_dnorm[0], b_dnorm[0])
    cd_p = tail[:, tail.shape[1] - (D_CONV - 1):, :]
    gw = width // C_GROUPS
    mix1_s, vn_s, glu_s = _odd_mix_step(
        proj1_s, state_conv_d[0].transpose(1, 0, 2), g_vnorm[0], b_vnorm[0],
        jnp.repeat(w_spatial[0, :, 0, 0], gw), jnp.repeat(b_spatial[0, :, 0], gw),
        w_dconv[0], b_dconv[0], g_dnorm[0], b_dnorm[0])
    h_p, h_s = _matmul_ws(mix1.reshape(bp * tp, 2 * width), _pad_rows(mix1_s.astype(BF16), ms), w_out_odd, 0, 0, d,
                          res=h_p, res_s=h_s)
    cd_s = jnp.concatenate([state_conv_d[0][:, 1:, :], glu_s[:, None, :]], axis=1)

    h_p, h_s, ff_p1, ff_s1 = _conv_ffn(h_p, h_s, tp, w_norm_ffn[1], 1, w_up, w_fconv, b_fconv, w_down_b,
                                       state_ffn_conv[1])

    y_p = _rmsnorm(h_p, w_norm_final, F32).reshape(bp, tp, d)
    y_s = _rmsnorm(h_s, w_norm_final, F32)[:bs].reshape(bs, 1, d)

    return (y_p, y_s,
            k_p[None], v_p[None], ki_p[None],
            bk_s.reshape(1, bs, 1, B_KV_HEADS, B_DH), bv_s.reshape(1, bs, 1, B_KV_HEADS, B_DH),
            ik_s.reshape(1, bs, 1, IDX_DIM),
            c_p[None], n_p.reshape(1, bp, A_HEADS, A_DQK), m_p[None],
            c_s[None], n_s.reshape(1, bs, A_HEADS, A_DQK), m_s.reshape(1, bs, A_HEADS),
            vn_s.reshape(1, bs, 1, width),
            cd_p[None], cd_s[None],
            jnp.stack([ff_p0, ff_p1]), jnp.stack([ff_s0, ff_s1]))
```

```python
import functools
import math

import jax
import jax.numpy as jnp
from jax import lax
from jax.experimental import pallas as pl
from jax.experimental.pallas import tpu as pltpu

F32 = jnp.float32
BF16 = jnp.bfloat16
I32 = jnp.int32
EPS = 1e-6
NEG_INF = float("-inf")
LOG2_E = math.log2(math.e)
INT_MIN = -(2 ** 31)

A_HEADS = 8
A_DQK = 128
A_DV = 256
B_HEADS = 16
B_KV_HEADS = 4
B_DH = 128
IDX_HEADS = 16
IDX_DIM = 64
TOPK_MAX = 256
PAGE = 128
C_GROUPS = 8
C_CHUNK = 128
D_CONV = 31
FF_CONV = 3
LANE = 128
VMEM_LIMIT_BYTES = 48 * 1024 * 1024
WS_VMEM_LIMIT_BYTES = 56 * 1024 * 1024
FFN_ROW_CHUNK = 1024
DSA_PAGES_PER_STEP = 8

_AQ, _AK, _AV, _AO = 0, 1024, 2048, 4096
_N_A = 6144
_BQ, _BK, _BV, _IQ = 0, 2048, 2560, 3072
_N_B = 4096
_S_AI, _S_AF, _S_IW, _S_IK = 0, 8, 16, 32


def _params(sem):
    return pltpu.CompilerParams(dimension_semantics=sem, vmem_limit_bytes=VMEM_LIMIT_BYTES)


def _dot(a, b):
    return jnp.dot(a, b, preferred_element_type=F32)


def _dot_nt(a, b):
    return lax.dot_general(a, b, (((1,), (1,)), ((), ())), preferred_element_type=F32)


def _sigmoid(x):
    return 1.0 / (1.0 + jnp.exp(-x))


def _col(row):
    n = row.shape[1]
    r = lax.broadcasted_iota(I32, (n, n), 0)
    c = lax.broadcasted_iota(I32, (n, n), 1)
    return jnp.sum(jnp.where(r == c, row, 0.0), axis=1, keepdims=True)


def _rms_kernel(x_ref, g_ref, o_ref):
    x = x_ref[...]
    y = x * lax.rsqrt(jnp.mean(x * x, axis=-1, keepdims=True) + EPS)
    o_ref[...] = (y * g_ref[...]).astype(o_ref.dtype)


def _rmsnorm(x, g, out_dtype):
    m, d = x.shape
    tm = min(m, 256)
    return pl.pallas_call(
        _rms_kernel,
        grid=(m // tm,),
        in_specs=[pl.BlockSpec((tm, d), lambda i: (i, 0)), pl.BlockSpec((1, d), lambda i: (0, 0))],
        out_specs=pl.BlockSpec((tm, d), lambda i: (i, 0)),
        out_shape=jax.ShapeDtypeStruct((m, d), out_dtype),
        compiler_params=_params(("parallel",)),
        name="rmsnorm",
    )(x, g.reshape(1, d))


def _mm_kernel(*refs, nk, has_res):
    if has_res:
        x_ref, w_ref, r_ref, o_ref = refs[:4]
        scratch = refs[4:]
    else:
        x_ref, w_ref, o_ref = refs[:3]
        r_ref = None
        scratch = refs[3:]
    if nk == 1:
        acc = _dot(x_ref[...], w_ref[...])
        if has_res:
            acc = acc + r_ref[...]
        o_ref[...] = acc.astype(o_ref.dtype)
        return
    acc_ref, = scratch
    k = pl.program_id(2)

    @pl.when(k == 0)
    def _():
        acc_ref[...] = jnp.zeros_like(acc_ref)

    acc_ref[...] += _dot(x_ref[...], w_ref[...])

    @pl.when(k == nk - 1)
    def _():
        acc = acc_ref[...]
        if has_res:
            acc = acc + r_ref[...]
        o_ref[...] = acc.astype(o_ref.dtype)


def _matmul(x, w, layer=0, res=None, out_dtype=F32, tm=1024, tn=512, tk=None):
    m, kdim = x.shape
    n = w.shape[2]
    tm = min(tm, m)
    tn = min(tn, n)
    tk = kdim if tk is None else tk
    assert m % tm == 0 and n % tn == 0 and kdim % tk == 0
    nk = kdim // tk
    in_specs = [pl.BlockSpec((tm, tk), lambda i, j, k: (i, k)),
                pl.BlockSpec((None, tk, tn), lambda i, j, k: (layer, k, j))]
    args = [x, w]
    if res is not None:
        in_specs.append(pl.BlockSpec((tm, tn), lambda i, j, k: (i, j)))
        args.append(res)
    return pl.pallas_call(
        functools.partial(_mm_kernel, nk=nk, has_res=res is not None),
        grid=(m // tm, n // tn, nk),
        in_specs=in_specs,
        out_specs=pl.BlockSpec((tm, tn), lambda i, j, k: (i, j)),
        out_shape=jax.ShapeDtypeStruct((m, n), out_dtype),
        scratch_shapes=[pltpu.VMEM((tm, tn), F32)] if nk > 1 else [],
        compiler_params=_params(("parallel", "parallel", "arbitrary")),
        name="matmul",
    )(*args)


def _mm_ws_kernel(*refs, has_res, emit_bf16):
    refs = list(refs)
    x_ref, xs_ref, w_ref = refs[:3]
    r_ref, rs_ref = refs[3:5] if has_res else (None, None)
    outs = refs[5:] if has_res else refs[3:]
    o_ref, os_ref = outs[:2]
    ob_ref = outs[2] if emit_bf16 else None
    wb_ref = outs[-1]

    @pl.when(pl.program_id(1) == 0)
    def _():
        wb_ref[...] = w_ref[...].astype(BF16)
        acc_s = _dot(xs_ref[...], wb_ref[...])
        os_ref[...] = acc_s + rs_ref[...] if has_res else acc_s

    acc = _dot(x_ref[...], wb_ref[...])
    if has_res:
        acc = acc + r_ref[...]
    o_ref[...] = acc
    if emit_bf16:
        ob_ref[...] = acc.astype(BF16)


def _matmul_ws(x, xs, w, layer, col0, n, res=None, res_s=None, emit_bf16=False, tm=1024, tn=512):
    m, kdim = x.shape
    ms = xs.shape[0]
    assert m % tm == 0 and n % tn == 0 and col0 % tn == 0
    c0 = col0 // tn
    has_res = res is not None
    out_specs = [pl.BlockSpec((tm, tn), lambda j, i: (i, j)), pl.BlockSpec((ms, tn), lambda j, i: (0, j))]
    out_shape = [jax.ShapeDtypeStruct((m, n), F32), jax.ShapeDtypeStruct((ms, n), F32)]
    if emit_bf16:
        out_specs.append(pl.BlockSpec((tm, tn), lambda j, i: (i, j)))
        out_shape.append(jax.ShapeDtypeStruct((m, n), BF16))
    in_specs = [pl.BlockSpec((tm, kdim), lambda j, i: (i, 0)),
                pl.BlockSpec((ms, kdim), lambda j, i: (0, 0)),
                pl.BlockSpec((None, kdim, tn), lambda j, i: (layer, 0, c0 + j))]
    args = [x, xs, w]
    if has_res:
        in_specs += [pl.BlockSpec((tm, tn), lambda j, i: (i, j)), pl.BlockSpec((ms, tn), lambda j, i: (0, j))]
        args += [res, res_s]
    return pl.pallas_call(
        functools.partial(_mm_ws_kernel, has_res=has_res, emit_bf16=emit_bf16),
        grid=(n // tn, m // tm),
        in_specs=in_specs,
        out_specs=tuple(out_specs),
        out_shape=tuple(out_shape),
        scratch_shapes=[pltpu.VMEM((kdim, tn), BF16)],
        compiler_params=pltpu.CompilerParams(dimension_semantics=("parallel", "arbitrary"),
                                             vmem_limit_bytes=WS_VMEM_LIMIT_BYTES),
        name="matmul_ws",
    )(*args)


def _matmul_xres(x, w, layer, res, tm=512, tn=512):
    m, kdim = x.shape
    n = w.shape[2]
    assert m % tm == 0 and n % tn == 0
    return pl.pallas_call(
        functools.partial(_mm_kernel, nk=1, has_res=True),
        grid=(m // tm, n // tn),
        in_specs=[pl.BlockSpec((tm, kdim), lambda i, j: (i, 0)),
                  pl.BlockSpec((None, kdim, tn), lambda i, j: (layer, 0, j)),
                  pl.BlockSpec((tm, tn), lambda i, j: (i, j))],
        out_specs=pl.BlockSpec((tm, tn), lambda i, j: (i, j)),
        out_shape=jax.ShapeDtypeStruct((m, n), F32),
        compiler_params=_params(("parallel", "arbitrary")),
        name="matmul_xres",
    )(x, w, res)


def _conv_rows(h, prev, w, b):
    def taps(x, shifted):
        y = w[FF_CONV - 1:FF_CONV, :] * x + b
        for s in range(1, FF_CONV):
            y = y + w[FF_CONV - 1 - s:FF_CONV - s, :] * shifted(s)
        return y

    bulk = taps(h, lambda s: pltpu.roll(h, s, 0))
    h8 = h[0:8, :]
    row = lax.broadcasted_iota(I32, h8.shape, 0)

    def head_shift(s):
        x = pltpu.roll(h8, s, 0)
        for r in range(s):
            x = jnp.where(row == r, prev[8 - s + r:8 - s + r + 1, :], x)
        return x

    return jnp.concatenate([taps(h8, head_shift), bulk[8:, :]], axis=0)


def _ffn_up_kernel(x_ref, xs_ref, wg_ref, wv_ref, cg_ref, cv_ref, bg_ref, bv_ref,
                   act_ref, tg_ref, tv_ref, sg_ref, sv_ref, wb_ref, carry_ref, *, tn, tiles_per_seq):
    i = pl.program_id(1)

    @pl.when(i == 0)
    def _():
        wb_ref[:, 0:tn] = wg_ref[...].astype(BF16)
        wb_ref[:, tn:2 * tn] = wv_ref[...].astype(BF16)
        hs = _dot(xs_ref[...], wb_ref[...])
        sg_ref[...] = hs[:, 0:tn]
        sv_ref[...] = hs[:, tn:2 * tn]

    @pl.when(i % tiles_per_seq == 0)
    def _():
        carry_ref[...] = jnp.zeros_like(carry_ref)

    tm = x_ref.shape[0]
    rc = min(tm, FFN_ROW_CHUNK)
    prev = carry_ref[...]
    for r in range(tm // rc):
        h = _dot(x_ref[r * rc:(r + 1) * rc, :], wb_ref[...])
        g = _conv_rows(h[:, 0:tn], prev[:, 0:tn], cg_ref[...], bg_ref[...])
        v = _conv_rows(h[:, tn:2 * tn], prev[:, tn:2 * tn], cv_ref[...], bv_ref[...])
        act_ref[r * rc:(r + 1) * rc, :] = (g * _sigmoid(g) * v).astype(act_ref.dtype)
        prev = h[rc - 8:rc, :]
    carry_ref[...] = prev

    @pl.when(i % tiles_per_seq == tiles_per_seq - 1)
    def _():
        tg_ref[0] = prev[:, 0:tn]
        tv_ref[0] = prev[:, tn:2 * tn]


def _ffn_up(x, xs, w_up, w_fc, b_fc, layer, seq_len, tm=1024, tn=256):
    m, kdim = x.shape
    ms = xs.shape[0]
    f = w_up.shape[2] // 2
    nf = f // tn
    tm = min(tm, seq_len)
    assert m % tm == 0 and seq_len % tm == 0 and f % tn == 0
    tiles_per_seq = seq_len // tm
    nseq = m // seq_len
    return pl.pallas_call(
        functools.partial(_ffn_up_kernel, tn=tn, tiles_per_seq=tiles_per_seq),
        grid=(nf, m // tm),
        in_specs=[pl.BlockSpec((tm, kdim), lambda j, i: (i, 0)),
                  pl.BlockSpec((ms, kdim), lambda j, i: (0, 0)),
                  pl.BlockSpec((None, kdim, tn), lambda j, i: (layer, 0, j)),
                  pl.BlockSpec((None, kdim, tn), lambda j, i: (layer, 0, j + nf)),
                  pl.BlockSpec((None, FF_CONV, tn), lambda j, i: (layer, 0, j)),
                  pl.BlockSpec((None, FF_CONV, tn), lambda j, i: (layer, 0, j + nf)),
                  pl.BlockSpec((None, 1, tn), lambda j, i: (layer, 0, j)),
                  pl.BlockSpec((None, 1, tn), lambda j, i: (layer, 0, j + nf))],
        out_specs=(pl.BlockSpec((tm, tn), lambda j, i: (i, j)),
                   pl.BlockSpec((1, 8, tn), lambda j, i: (i // tiles_per_seq, 0, j)),
                   pl.BlockSpec((1, 8, tn), lambda j, i: (i // tiles_per_seq, 0, j)),
                   pl.BlockSpec((ms, tn), lambda j, i: (0, j)),
                   pl.BlockSpec((ms, tn), lambda j, i: (0, j))),
        out_shape=(jax.ShapeDtypeStruct((m, f), BF16),
                   jax.ShapeDtypeStruct((nseq, 8, f), F32), jax.ShapeDtypeStruct((nseq, 8, f), F32),
                   jax.ShapeDtypeStruct((ms, f), F32), jax.ShapeDtypeStruct((ms, f), F32)),
        scratch_shapes=[pltpu.VMEM((kdim, 2 * tn), BF16), pltpu.VMEM((8, 2 * tn), F32)],
        compiler_params=pltpu.CompilerParams(dimension_semantics=("parallel", "arbitrary"),
                                             vmem_limit_bytes=WS_VMEM_LIMIT_BYTES),
        name="ffn_up",
    )(x, xs, w_up, w_up, w_fc, w_fc, b_fc, b_fc)


def _ffn_act_step_kernel(hg_ref, hv_ref, bufg_ref, bufv_ref, wg_ref, wv_ref, bg_ref, bv_ref, o_ref):
    def conv(h_ref, buf_ref, w_ref, b_ref):
        w = w_ref[...]
        y = w[FF_CONV - 1:FF_CONV, :] * h_ref[...] + b_ref[...]
        for s in range(FF_CONV - 1):
            y = y + w[s:s + 1, :] * buf_ref[s]
        return y

    g = conv(hg_ref, bufg_ref, wg_ref, bg_ref)
    o_ref[...] = g * _sigmoid(g) * conv(hv_ref, bufv_ref, wv_ref, bv_ref)


def _ffn_act_step(h_g, h_v, buf, w_fc, b_fc):
    b, f = h_g.shape
    return pl.pallas_call(
        _ffn_act_step_kernel,
        out_shape=jax.ShapeDtypeStruct((b, f), F32),
        compiler_params=pltpu.CompilerParams(vmem_limit_bytes=VMEM_LIMIT_BYTES),
        name="ffn_act_step",
    )(h_g, h_v, buf[:, :, :f], buf[:, :, f:], w_fc[:, :f], w_fc[:, f:],
      b_fc[:, :f], b_fc[:, f:])


def _scan_lanes(x, op, fill):
    n = x.shape[1]
    lane = lax.broadcasted_iota(I32, x.shape, 1)
    d = 1
    while d < n:
        x = op(x, jnp.where(lane >= d, pltpu.roll(x, d, 1), fill))
        d *= 2
    return x


def _log_sigmoid(x):
    return jnp.minimum(x, 0.0) - jnp.log(1.0 + jnp.exp(-jnp.abs(x)))


def _gate_kernel(i_ref, f_ref, bi_ref, bf_ref, a_ref, g_ref, m_ref):
    ig = i_ref[...] + bi_ref[...]
    lf = _log_sigmoid(f_ref[...] + bf_ref[...])
    bsum = _scan_lanes(lf, jnp.add, 0.0)
    a = ig - bsum
    g = jnp.maximum(_scan_lanes(a, jnp.maximum, NEG_INF), 0.0)
    a_ref[...] = a
    g_ref[...] = g
    m_ref[...] = bsum + g


def _mlstm_gates(i_t, f_t, b_i, b_f):
    shp = jax.ShapeDtypeStruct(i_t.shape, F32)
    return pl.pallas_call(
        _gate_kernel, out_shape=(shp, shp, shp),
        compiler_params=pltpu.CompilerParams(vmem_limit_bytes=VMEM_LIMIT_BYTES),
        name="mlstm_gates",
    )(i_t, f_t, b_i, b_f)


def _mlstm_kernel(q_ref, k_ref, v_ref, o_ref, a_ref, g_ref, m_ref, gh_ref, out_ref, num_sc, den_sc, *, tq, hp):
    qi = pl.program_id(2)
    off_q = pl.multiple_of(qi * tq, tq)
    row = lax.broadcasted_iota(I32, (tq, tq), 0)
    col = lax.broadcasted_iota(I32, (tq, tq), 1)
    qs = [(q_ref[0, :, hh * A_DQK:(hh + 1) * A_DQK] * (A_DQK ** -0.5)).astype(BF16) for hh in range(hp)]
    g_cs = [_col(g_ref[0, hh, :, pl.ds(off_q, tq)]) for hh in range(hp)]
    num_sc[...] = jnp.zeros_like(num_sc)
    den_sc[...] = jnp.zeros_like(den_sc)

    def body(kb, carry):
        off = pl.multiple_of(kb * tq, tq)
        causal = (col + kb * tq) <= (row + qi * tq)
        for hh in range(hp):
            k = k_ref[0, pl.ds(off, tq), hh * A_DQK:(hh + 1) * A_DQK]
            v = v_ref[0, pl.ds(off, tq), hh * A_DV:(hh + 1) * A_DV]
            a = a_ref[0, hh, :, pl.ds(off, tq)]
            w = jnp.exp(jnp.where(causal, a - g_cs[hh], NEG_INF))
            s = _dot_nt(qs[hh], k) * w
            num_sc[hh] += _dot(s.astype(BF16), v)
            den_sc[hh] += jnp.sum(s, axis=1, keepdims=True)
        return carry

    lax.fori_loop(0, qi + 1, body, 0)
    for hh in range(hp):
        m_c = _col(m_ref[0, hh, :, pl.ds(off_q, tq)])
        h = num_sc[hh] / jnp.maximum(jnp.abs(den_sc[hh]), jnp.exp(-m_c))
        h = h * lax.rsqrt(jnp.mean(h * h, axis=-1, keepdims=True) + EPS) * gh_ref[hh]
        cols = slice(hh * A_DV, (hh + 1) * A_DV)
        out_ref[0, :, cols] = (_sigmoid(o_ref[0, :, cols]) * h).astype(out_ref.dtype)


def _mlstm_prompt(p3, pb3, a, g, m, g_head):
    b, t, _ = p3.shape
    tq = 256
    hp = 4
    row_spec = pl.BlockSpec((1, hp, 1, t), lambda i, h, j: (i, h, 0, 0))
    wq, wv = hp * A_DQK, hp * A_DV
    return pl.pallas_call(
        functools.partial(_mlstm_kernel, tq=tq, hp=hp),
        grid=(b, A_HEADS // hp, t // tq),
        in_specs=[pl.BlockSpec((1, tq, wq), lambda i, h, j: (i, j, _AQ // wq + h)),
                  pl.BlockSpec((1, t, wq), lambda i, h, j: (i, 0, _AK // wq + h)),
                  pl.BlockSpec((1, t, wv), lambda i, h, j: (i, 0, _AV // wv + h)),
                  pl.BlockSpec((1, tq, wv), lambda i, h, j: (i, j, _AO // wv + h)),
                  row_spec, row_spec, row_spec,
                  pl.BlockSpec((hp, 1, A_DV), lambda i, h, j: (h, 0, 0))],
        out_specs=pl.BlockSpec((1, tq, wv), lambda i, h, j: (i, j, h)),
        out_shape=jax.ShapeDtypeStruct((b, t, A_HEADS * A_DV), BF16),
        scratch_shapes=[pltpu.VMEM((hp, tq, A_DV), F32), pltpu.VMEM((hp, tq, 1), F32)],
        compiler_params=_params(("parallel", "parallel", "arbitrary")),
        name="mlstm_prompt",
    )(p3, pb3, pb3, p3, a, g, m, g_head)


def _mlstm_state_kernel(k_ref, v_ref, a_ref, g_ref, c_ref, n_ref, *, t, tc):
    g_last = g_ref[0, 0, :, t - 1:t]

    def body(cb, carry):
        c_acc, n_acc = carry
        off = pl.multiple_of(cb * tc, tc)
        w = _col(jnp.exp(a_ref[0, 0, :, pl.ds(off, tc)] - g_last))
        kw = k_ref[0, pl.ds(off, tc), :] * w
        vt = v_ref[0, pl.ds(off, tc), :].T.astype(BF16)
        return c_acc + _dot(vt, kw.astype(BF16)), n_acc + jnp.sum(kw, axis=0, keepdims=True)

    c_acc, n_acc = lax.fori_loop(0, t // tc, body,
                                 (jnp.zeros((A_DV, A_DQK), F32), jnp.zeros((1, A_DQK), F32)))
    c_ref[0, 0] = c_acc
    n_ref[0, 0] = n_acc


def _mlstm_prompt_state(p3, a, g):
    b, t, _ = p3.shape
    row_spec = pl.BlockSpec((1, 1, 1, t), lambda i, h: (i, h, 0, 0))
    return pl.pallas_call(
        functools.partial(_mlstm_state_kernel, t=t, tc=256),
        grid=(b, A_HEADS),
        in_specs=[pl.BlockSpec((1, t, A_DQK), lambda i, h: (i, 0, _AK // A_DQK + h)),
                  pl.BlockSpec((1, t, A_DV), lambda i, h: (i, 0, _AV // A_DV + h)),
                  row_spec, row_spec],
        out_specs=(pl.BlockSpec((1, 1, A_DV, A_DQK), lambda i, h: (i, h, 0, 0)),
                   pl.BlockSpec((1, 1, 1, A_DQK), lambda i, h: (i, h, 0, 0))),
        out_shape=(jax.ShapeDtypeStruct((b, A_HEADS, A_DV, A_DQK), F32),
                   jax.ShapeDtypeStruct((b, A_HEADS, 1, A_DQK), F32)),
        compiler_params=_params(("parallel", "parallel")),
        name="mlstm_prompt_state",
    )(p3, p3, a, g)


def _mlstm_step_kernel(q_ref, k_ref, v_ref, o_ref, gh_ref, ig_ref, fp_ref, c0_ref, n0_ref, m0_ref,
                       h_ref, c_ref, n_ref, m_ref):
    q = q_ref[0, 0] * (A_DQK ** -0.5)
    k = k_ref[0, 0]
    v = v_ref[0, 0]
    ig = ig_ref[0, 0]
    lf = _log_sigmoid(fp_ref[0, 0])
    c0 = c0_ref[0, 0]
    n0 = n0_ref[0, 0]
    m0 = m0_ref[0, 0]
    log_inter = lf + m0
    m_t = jnp.maximum(log_inter, ig)
    w_intra = jnp.exp(ig - m_t)
    w_inter = jnp.exp(log_inter - m_t)
    s = jnp.sum(q * k, axis=1, keepdims=True) * w_intra
    num = s * v + w_inter * jnp.sum(c0 * q, axis=1, keepdims=True)
    den = s + w_inter * jnp.sum(n0 * q, axis=1, keepdims=True)
    h = num / jnp.maximum(jnp.abs(den), jnp.exp(-m_t))
    h = h * lax.rsqrt(jnp.mean(h * h, axis=0, keepdims=True) + EPS) * gh_ref[0]
    h_ref[0, 0] = _sigmoid(o_ref[0, 0]) * h
    c_ref[0, 0] = w_inter * c0 + w_intra * (v * k)
    n_ref[0, 0] = w_inter * n0 + w_intra * k
    m_ref[0, 0] = m_t


def _mlstm_step(q, k, v_col, o_col, gh_col, ig, f_pre, c0, n0, m0):
    b = q.shape[0]
    h = A_HEADS
    row = pl.BlockSpec((1, 1, 1, A_DQK), lambda i, j: (i, j, 0, 0))
    colv = pl.BlockSpec((1, 1, A_DV, 1), lambda i, j: (i, j, 0, 0))
    one = pl.BlockSpec((1, 1, 1, 1), lambda i, j: (i, j, 0, 0))
    mat = pl.BlockSpec((1, 1, A_DV, A_DQK), lambda i, j: (i, j, 0, 0))
    return pl.pallas_call(
        _mlstm_step_kernel,
        grid=(b, h),
        in_specs=[row, row, colv, colv, pl.BlockSpec((1, A_DV, 1), lambda i, j: (j, 0, 0)),
                  one, one, mat, row, one],
        out_specs=(colv, mat, row, one),
        out_shape=(jax.ShapeDtypeStruct((b, h, A_DV, 1), F32),
                   jax.ShapeDtypeStruct((b, h, A_DV, A_DQK), F32),
                   jax.ShapeDtypeStruct((b, h, 1, A_DQK), F32),
                   jax.ShapeDtypeStruct((b, h, 1, 1), F32)),
        compiler_params=_params(("parallel", "parallel")),
        name="mlstm_step",
    )(q, k, v_col, o_col, gh_col, ig, f_pre, c0, n0, m0)


def _sort_key(x):
    x = jnp.where(x == 0.0, 0.0, x)
    bits = pltpu.bitcast(x, I32)
    return jnp.where(bits >= 0, bits, bits ^ jnp.int32(0x7FFFFFFF))


def _kth_largest_key(key, k):
    def count_ge(c):
        return jnp.sum(jnp.where(key >= c, 1.0, 0.0), axis=1, keepdims=True)

    kf = jnp.float32(k)
    thr = jnp.where(count_ge(jnp.int32(0)) >= kf, jnp.int32(0), jnp.int32(INT_MIN))
    thr = jnp.broadcast_to(thr, (key.shape[0], 1))

    def body(i, thr):
        cand = thr + lax.shift_left(jnp.int32(1), jnp.int32(30) - i)
        return jnp.where(count_ge(cand) >= kf, cand, thr)

    return lax.fori_loop(0, 31, body, thr)


def _tri_ones(n, dtype):
    r = lax.broadcasted_iota(I32, (n, n), 0)
    c = lax.broadcasted_iota(I32, (n, n), 1)
    return jnp.where(r <= c, 1.0, 0.0).astype(dtype)


def _select_topk(scores, visible, k):
    key = _sort_key(jnp.where(visible, scores, NEG_INF))
    thr = _kth_largest_key(key, k)
    gt = key > thr
    eq = jnp.logical_and(key == thr, visible)
    need = jnp.float32(k) - jnp.sum(jnp.where(gt, 1.0, 0.0), axis=1, keepdims=True)
    tri = _tri_ones(LANE, BF16)
    off = jnp.zeros((scores.shape[0], 1), F32)
    rank = []
    eqf = jnp.where(eq, 1.0, 0.0)
    for c in range(scores.shape[1] // LANE):
        p = _dot(eqf[:, c * LANE:(c + 1) * LANE].astype(BF16), tri) + off
        rank.append(p)
        off = p[:, LANE - 1:LANE]
    rank = jnp.concatenate(rank, axis=1)
    return jnp.logical_or(jnp.logical_and(gt, visible), jnp.logical_and(eq, rank <= need))


def _indexer_scores(qi_head, wi, ki):
    acc = None
    for h in range(IDX_HEADS):
        s = _dot_nt(qi_head(h).astype(BF16), ki) * (IDX_DIM ** -0.5)
        term = jnp.maximum(s, 0.0) * wi[:, h:h + 1]
        acc = term if acc is None else acc + term
    return acc * (IDX_HEADS ** -0.5)


def _dsa_prompt_kernel(q_ref, k_ref, v_ref, qi_ref, sm_ref, smq_ref, o_ref, sc_ref, *, tq, topk, extents):
    j = pl.program_id(1)
    group = B_HEADS // B_KV_HEADS
    lane = lax.broadcasted_iota(I32, (tq, LANE), 1)

    def attend(e):
        ki = sm_ref[0, 0:e, _S_IK:_S_IK + IDX_DIM].astype(BF16)
        sc_ref[:, 0:e] = jnp.zeros((tq, e), F32)

        def idx_pair(hp, carry):
            qpair = qi_ref[0, :, pl.ds(pl.multiple_of(hp * LANE, LANE), LANE)]
            for half in range(LANE // IDX_DIM):
                s = _dot_nt(qpair[:, half * IDX_DIM:(half + 1) * IDX_DIM], ki) * (IDX_DIM ** -0.5)
                head = hp * (LANE // IDX_DIM) + half
                w = jnp.sum(jnp.where(lane == _S_IW + head, smq_ref[0], 0.0), axis=1, keepdims=True)
                sc_ref[:, 0:e] += jnp.maximum(s, 0.0) * w
            return carry

        lax.fori_loop(0, IDX_HEADS * IDX_DIM // LANE, idx_pair, 0)
        scores = sc_ref[:, 0:e] * (IDX_HEADS ** -0.5)
        qpos = lax.broadcasted_iota(I32, (tq, e), 0) + j * tq
        kpos = lax.broadcasted_iota(I32, (tq, e), 1)
        sel = _select_topk(scores, kpos <= qpos, topk)
        sc_ref[:, 0:e] = jnp.where(sel, 0.0, NEG_INF)

        def kv_head(n, carry):
            kv_off = pl.multiple_of(n * B_DH, B_DH)
            kn = k_ref[0, 0:e, pl.ds(kv_off, B_DH)]
            vn = v_ref[0, 0:e, pl.ds(kv_off, B_DH)]
            for gidx in range(group):
                q_off = pl.multiple_of((n * group + gidx) * B_DH, B_DH)
                s = _dot_nt(q_ref[0, :, pl.ds(q_off, B_DH)], kn) + sc_ref[:, 0:e]
                p = jnp.exp2((s - jnp.max(s, axis=1, keepdims=True)) * (B_DH ** -0.5 * LOG2_E))
                l = jnp.sum(p, axis=1, keepdims=True)
                o_ref[0, :, pl.ds(q_off, B_DH)] = (_dot(p.astype(BF16), vn) / l).astype(o_ref.dtype)
            return carry

        lax.fori_loop(0, B_KV_HEADS, kv_head, 0)

    last = (j + 1) * tq
    for idx, e in enumerate(extents):
        lo = extents[idx - 1] if idx else 0
        pl.when(jnp.logical_and(last > lo, last <= e))(functools.partial(attend, e))


def _dsa_prompt(pb3, small3):
    b, t, _ = pb3.shape
    tq = 256
    topk = min(TOPK_MAX, t // 4)
    n_ext = min(4, t // tq)
    extents = tuple(t * (i + 1) // n_ext for i in range(n_ext))
    assert all(e % tq == 0 for e in extents) and extents[0] >= topk
    w_q, w_kv, w_qi = B_HEADS * B_DH, B_KV_HEADS * B_DH, IDX_HEADS * IDX_DIM
    return pl.pallas_call(
        functools.partial(_dsa_prompt_kernel, tq=tq, topk=topk, extents=extents),
        grid=(b, t // tq),
        in_specs=[pl.BlockSpec((1, tq, w_q), lambda i, j: (i, j, _BQ // w_q)),
                  pl.BlockSpec((1, t, w_kv), lambda i, j: (i, 0, _BK // w_kv)),
                  pl.BlockSpec((1, t, w_kv), lambda i, j: (i, 0, _BV // w_kv)),
                  pl.BlockSpec((1, tq, w_qi), lambda i, j: (i, j, _IQ // w_qi)),
                  pl.BlockSpec((1, t, LANE), lambda i, j: (i, 0, 0)),
                  pl.BlockSpec((1, tq, LANE), lambda i, j: (i, j, 0))],
        out_specs=pl.BlockSpec((1, tq, w_q), lambda i, j: (i, j, 0)),
        out_shape=jax.ShapeDtypeStruct((b, t, w_q), BF16),
        scratch_shapes=[pltpu.VMEM((tq, t), F32)],
        compiler_params=_params(("parallel", "parallel")),
        name="dsa_prompt",
    )(pb3, pb3, pb3, pb3, small3, small3)


def _page_specs(block, layer, pages_per_step):
    def spec(r):
        zeros = (0,) * (len(block) - 2)
        return pl.BlockSpec(block, lambda i, p, pt: (layer, pt[i, p * pages_per_step + r]) + zeros)
    return [spec(r) for r in range(pages_per_step)]


def _dsa_step_scores_kernel(pt_ref, qi_ref, wi_ref, *refs):
    del pt_ref
    ki_refs, o_ref = refs[:-1], refs[-1]
    qi = qi_ref[0].astype(BF16)
    ki = jnp.concatenate([r[0] for r in ki_refs], axis=0).astype(BF16)
    s = _dot_nt(qi, ki) * (IDX_DIM ** -0.5)
    o_ref[0] = jnp.sum(jnp.maximum(s, 0.0) * wi_ref[0], axis=0, keepdims=True) * (IDX_HEADS ** -0.5)


def _dsa_step_scores(page_table, qi, wi_col, cache_ki, layer):
    b, n_pages = page_table.shape
    g = math.gcd(n_pages, DSA_PAGES_PER_STEP)
    grid_spec = pltpu.PrefetchScalarGridSpec(
        num_scalar_prefetch=1,
        grid=(b, n_pages // g),
        in_specs=[pl.BlockSpec((1, IDX_HEADS, IDX_DIM), lambda i, p, pt: (i, 0, 0)),
                  pl.BlockSpec((1, IDX_HEADS, 1), lambda i, p, pt: (i, 0, 0))]
                 + _page_specs((None, 1, PAGE, IDX_DIM), layer, g),
        out_specs=pl.BlockSpec((1, 1, g * PAGE), lambda i, p, pt: (i, 0, p)),
    )
    return pl.pallas_call(
        _dsa_step_scores_kernel,
        grid_spec=grid_spec,
        out_shape=jax.ShapeDtypeStruct((b, 1, n_pages * PAGE), F32),
        compiler_params=_params(("parallel", "arbitrary")),
        name="dsa_step_scores",
    )(page_table, qi, wi_col, *([cache_ki] * g))


def _dsa_step_select_kernel(sc_ref, self_ref, sel_ref, selself_ref, *, topk):
    sc = sc_ref[...]
    sself = self_ref[...]
    key = _sort_key(sc)
    key_self = _sort_key(sself)
    kf = jnp.float32(topk)

    def count_ge(c):
        return (jnp.sum(jnp.where(key >= c, 1.0, 0.0), axis=1, keepdims=True)
                + jnp.where(key_self >= c, 1.0, 0.0))

    thr = jnp.where(count_ge(jnp.int32(0)) >= kf, jnp.int32(0), jnp.int32(INT_MIN))

    def body(i, thr):
        cand = thr + lax.shift_left(jnp.int32(1), jnp.int32(30) - i)
        return jnp.where(count_ge(cand) >= kf, cand, thr)

    thr = lax.fori_loop(0, 31, body, thr)
    gt = key > thr
    eq = key == thr
    n_gt = jnp.sum(jnp.where(gt, 1.0, 0.0), axis=1, keepdims=True) + jnp.where(key_self > thr, 1.0, 0.0)
    need = kf - n_gt
    rank = _scan_lanes(jnp.where(eq, 1.0, 0.0), jnp.add, 0.0)
    sel_ref[...] = jnp.where(jnp.logical_or(gt, jnp.logical_and(eq, rank <= need)), 1.0, 0.0)
    n_eq = rank[:, -1:]
    selself_ref[...] = jnp.where(
        jnp.logical_or(key_self > thr, jnp.logical_and(key_self == thr, n_eq < need)), 1.0, 0.0)


def _dsa_step_select(scores, self_score, topk):
    b, s = scores.shape
    return pl.pallas_call(
        functools.partial(_dsa_step_select_kernel, topk=topk),
        out_shape=(jax.ShapeDtypeStruct((b, s), F32), jax.ShapeDtypeStruct((b, 1), F32)),
        compiler_params=pltpu.CompilerParams(vmem_limit_bytes=VMEM_LIMIT_BYTES),
        name="dsa_step_select",
    )(scores, self_score)


def _dsa_step_attend_kernel(pt_ref, q_ref, sel_ref, selself_ref, kn_ref, vn_ref, *refs, pages_per_step):
    del pt_ref
    g = pages_per_step
    k_refs, v_refs = refs[:g], refs[g:2 * g]
    o_ref, m_sc, l_sc, acc_sc = refs[2 * g:]
    p_idx = pl.program_id(1)
    group = B_HEADS // B_KV_HEADS

    @pl.when(p_idx == 0)
    def _():
        m_sc[...] = jnp.full_like(m_sc, NEG_INF)
        l_sc[...] = jnp.zeros_like(l_sc)
        acc_sc[...] = jnp.zeros_like(acc_sc)

    q = q_ref[0].astype(BF16)
    kv_of_row = lax.broadcasted_iota(I32, (B_HEADS, 1), 0) // group

    def head_rows(page_refs, n):
        return jnp.concatenate([r[0, :, n, :] for r in page_refs], axis=0).astype(BF16)

    def update(k_pages, v_pages, sel_row):
        s = jnp.zeros((B_HEADS, sel_row.shape[1]), F32)
        for n in range(B_KV_HEADS):
            s = jnp.where(kv_of_row == n, _dot_nt(q, head_rows(k_pages, n)), s)
        s = s * (B_DH ** -0.5) + jnp.where(sel_row > 0.0, 0.0, NEG_INF)
        m_old = m_sc[...]
        m_new = jnp.maximum(m_old, jnp.max(s, axis=1, keepdims=True))
        m_safe = jnp.where(m_new == NEG_INF, 0.0, m_new)
        alpha = jnp.exp(m_old - m_safe)
        p = jnp.exp(s - m_safe)
        pb = p.astype(BF16)
        pv = jnp.zeros((B_HEADS, B_DH), F32)
        for n in range(B_KV_HEADS):
            pv = jnp.where(kv_of_row == n, _dot(pb, head_rows(v_pages, n)), pv)
        l_sc[...] = alpha * l_sc[...] + jnp.sum(p, axis=1, keepdims=True)
        acc_sc[...] = alpha * acc_sc[...] + pv
        m_sc[...] = m_new

    update(k_refs, v_refs, sel_ref[0])

    @pl.when(p_idx == pl.num_programs(1) - 1)
    def _():
        update([kn_ref], [vn_ref], selself_ref[0])
        o_ref[0] = acc_sc[...] / l_sc[...]


def _dsa_step_attend(page_table, q, sel, sel_self, k_new, v_new, cache_k, cache_v, layer):
    b, n_pages = page_table.shape
    g = math.gcd(n_pages, DSA_PAGES_PER_STEP)
    own = pl.BlockSpec((1, PAGE, B_KV_HEADS, B_DH), lambda i, p, pt: (i, 0, 0, 0))
    page_block = (None, 1, PAGE, B_KV_HEADS, B_DH)
    grid_spec = pltpu.PrefetchScalarGridSpec(
        num_scalar_prefetch=1,
        grid=(b, n_pages // g),
        in_specs=[pl.BlockSpec((1, B_HEADS, B_DH), lambda i, p, pt: (i, 0, 0)),
                  pl.BlockSpec((1, 1, g * PAGE), lambda i, p, pt: (i, 0, p)),
                  pl.BlockSpec((1, 1, PAGE), lambda i, p, pt: (i, 0, 0)),
                  own, own]
                 + _page_specs(page_block, layer, g) + _page_specs(page_block, layer, g),
        out_specs=pl.BlockSpec((1, B_HEADS, B_DH), lambda i, p, pt: (i, 0, 0)),
        scratch_shapes=[pltpu.VMEM((B_HEADS, 1), F32), pltpu.VMEM((B_HEADS, 1), F32),
                        pltpu.VMEM((B_HEADS, B_DH), F32)],
    )
    return pl.pallas_call(
        functools.partial(_dsa_step_attend_kernel, pages_per_step=g),
        grid_spec=grid_spec,
        out_shape=jax.ShapeDtypeStruct((b, B_HEADS, B_DH), F32),
        compiler_params=_params(("parallel", "arbitrary")),
        name="dsa_step_attend",
    )(page_table, q, sel, sel_self, k_new, v_new, *([cache_k] * g), *([cache_v] * g))


def _layernorm(x, g, b):
    xc = x - jnp.mean(x, axis=-1, keepdims=True)
    return xc * lax.rsqrt(jnp.mean(xc * xc, axis=-1, keepdims=True) + EPS) * g + b


def _odd_mix_kernel(u_ref, v_ref, a_ref, gt_ref, gv_ref, bv_ref, wsp_ref, bsp_ref, wdc_ref, bdc_ref,
                    gdn_ref, bdn_ref, o_ref, tail_ref, xbuf, slab, *, tt, width, halo):
    j = pl.program_id(1)
    nj = pl.num_programs(1)

    @pl.when(j == 0)
    def _():
        xbuf[0:halo, :] = jnp.zeros((halo, width), F32)
        xbuf[halo + tt:halo + tt + 8, :] = jnp.zeros((8, width), F32)

    vn = _layernorm(v_ref[0], gv_ref[...], bv_ref[...]).astype(BF16)
    r = lax.broadcasted_iota(I32, (tt, tt), 0)
    c = lax.broadcasted_iota(I32, (tt, tt), 1)
    gw = width // C_GROUPS
    for gi in range(C_GROUPS):
        wm = jnp.where(c <= r, wsp_ref[gi], 0.0).astype(BF16)
        sg = _dot(wm, vn[:, gi * gw:(gi + 1) * gw]) + bsp_ref[gi]
        o_ref[0, :, gi * gw:(gi + 1) * gw] = (u_ref[0, :, gi * gw:(gi + 1) * gw] * sg).astype(o_ref.dtype)

    glu = a_ref[0] * _sigmoid(gt_ref[0])
    xbuf[halo:halo + tt, :] = glu
    wdc = wdc_ref[...]
    dc = jnp.zeros((tt, width), F32) + bdc_ref[...]
    base = halo - (D_CONV - 1)
    for b8 in range(8):
        taps = [kk for kk in range(D_CONV) if (base + kk) % 8 == b8]
        src = xbuf
        if b8:
            slab[...] = xbuf[b8:b8 + tt + halo, :]
            src = slab
        for kk in taps:
            a8 = (base + kk) - b8
            dc = dc + wdc[kk:kk + 1, :] * src[a8:a8 + tt, :]
    d = _layernorm(dc, gdn_ref[...], bdn_ref[...])
    o_ref[0, :, width:2 * width] = (d * _sigmoid(d)).astype(o_ref.dtype)

    xbuf[0:halo, :] = glu[tt - halo:tt, :]

    @pl.when(j == nj - 1)
    def _():
        tail_ref[0] = glu[tt - halo:tt, :]


def _odd_mix(p3, g_v, b_v, w_sp, b_sp, w_dc, b_dc, g_dn, b_dn):
    b, t, w4 = p3.shape
    width = w4 // 4
    tt = C_CHUNK
    halo = 32
    vec = lambda x: x.reshape(1, width)
    vspec = pl.BlockSpec((1, width), lambda i, j: (0, 0))
    return pl.pallas_call(
        functools.partial(_odd_mix_kernel, tt=tt, width=width, halo=halo),
        grid=(b, t // tt),
        in_specs=[pl.BlockSpec((1, tt, width), lambda i, j: (i, j, 0)),
                  pl.BlockSpec((1, tt, width), lambda i, j: (i, j, 1)),
                  pl.BlockSpec((1, tt, width), lambda i, j: (i, j, 2)),
                  pl.BlockSpec((1, tt, width), lambda i, j: (i, j, 3)),
                  vspec, vspec,
                  pl.BlockSpec((C_GROUPS, tt, tt), lambda i, j: (0, 0, 0)),
                  pl.BlockSpec((C_GROUPS, tt, 1), lambda i, j: (0, 0, 0)),
                  pl.BlockSpec((D_CONV, width), lambda i, j: (0, 0)),
                  vspec, vspec, vspec],
        out_specs=(pl.BlockSpec((1, tt, 2 * width), lambda i, j: (i, j, 0)),
                   pl.BlockSpec((1, halo, width), lambda i, j: (i, 0, 0))),
        out_shape=(jax.ShapeDtypeStruct((b, t, 2 * width), BF16),
                   jax.ShapeDtypeStruct((b, halo, width), F32)),
        scratch_shapes=[pltpu.VMEM((halo + tt + 8, width), F32), pltpu.VMEM((halo + tt, width), F32)],
        compiler_params=_params(("parallel", "arbitrary")),
        name="odd_mix",
    )(p3, p3, p3, p3, vec(g_v), vec(b_v), w_sp, b_sp.reshape(C_GROUPS, tt, 1), w_dc, vec(b_dc),
      vec(g_dn), vec(b_dn))


def _odd_mix_step_kernel(p_ref, buf_ref, gv_ref, bv_ref, wsp_ref, bsp_ref, wdc_ref, bdc_ref,
                         gdn_ref, bdn_ref, o_ref, vn_ref, glu_ref, *, width):
    u = p_ref[:, 0:width]
    v = p_ref[:, width:2 * width]
    a = p_ref[:, 2 * width:3 * width]
    gt = p_ref[:, 3 * width:4 * width]
    vn = _layernorm(v, gv_ref[...], bv_ref[...])
    vn_ref[...] = vn
    o_ref[:, 0:width] = u * (wsp_ref[...] * vn + bsp_ref[...])
    glu = a * _sigmoid(gt)
    glu_ref[...] = glu
    wdc = wdc_ref[...]
    dc = wdc[D_CONV - 1:D_CONV, :] * glu + bdc_ref[...]
    for kk in range(D_CONV - 1):
        dc = dc + wdc[kk:kk + 1, :] * buf_ref[kk]
    d = _layernorm(dc, gdn_ref[...], bdn_ref[...])
    o_ref[:, width:2 * width] = d * _sigmoid(d)


def _odd_mix_step(p, buf, g_v, b_v, w_sp_vec, b_sp_vec, w_dc, b_dc, g_dn, b_dn):
    b, w4 = p.shape
    width = w4 // 4
    vec = lambda x: x.reshape(1, width)
    return pl.pallas_call(
        functools.partial(_odd_mix_step_kernel, width=width),
        out_shape=(jax.ShapeDtypeStruct((b, 2 * width), F32), jax.ShapeDtypeStruct((b, width), F32),
                   jax.ShapeDtypeStruct((b, width), F32)),
        compiler_params=pltpu.CompilerParams(vmem_limit_bytes=VMEM_LIMIT_BYTES),
        name="odd_mix_step",
    )(p, buf, vec(g_v), vec(b_v), vec(w_sp_vec), vec(b_sp_vec), w_dc, vec(b_dc), vec(g_dn), vec(b_dn))


def _pad_rows(x, rows):
    return jnp.pad(x, ((0, rows - x.shape[0]), (0, 0)))


def _split_even_weights(w):
    sizes = (1024, 1024, 2048, 2048, 8, 8, 2048, 512, 512, 1024, 16, 64)
    offs = [0]
    for s in sizes:
        offs.append(offs[-1] + s)
    col = lambda i: w[:, offs[i]:offs[i + 1]]
    return jnp.concatenate([col(6), col(7), col(8), col(9), col(4), col(5), col(10), col(11),
                            jnp.zeros((w.shape[0], LANE - 96), w.dtype)], axis=1)


def _conv_ffn(h_p, h_s, tp, g, layer, w_up, w_fc, b_fc, w_down_b, buf_s):
    f = w_up.shape[2] // 2
    nb = buf_s.shape[0]
    xn = _rmsnorm(h_p, g, BF16)
    xs = _rmsnorm(h_s, g, BF16)
    act, tail_g, tail_v, hs_g, hs_v = _ffn_up(xn, xs, w_up, w_fc, b_fc.reshape(b_fc.shape[0], 1, 2 * f),
                                              layer, tp)
    h_p_new = _matmul_xres(act, w_down_b, layer, h_p)
    ff_p = jnp.concatenate([tail_g, tail_v], axis=-1)[:, 8 - (FF_CONV - 1):, :]
    act_s = _ffn_act_step(hs_g[:nb], hs_v[:nb], buf_s.transpose(1, 0, 2), w_fc[layer], b_fc[layer].reshape(1, 2 * f))
    h_s_new = _matmul(_pad_rows(act_s.astype(BF16), h_s.shape[0]), w_down_b, layer, res=h_s, tn=1024, tk=f // 2)
    ff_s = jnp.concatenate([buf_s[:, 1:, :], jnp.concatenate([hs_g[:nb], hs_v[:nb]], axis=-1)[:, None, :]], axis=1)
    return h_p_new, h_s_new, ff_p, ff_s


def kernel(x_prompt, x_sample, cache_k, cache_v, cache_kidx, state_mlstm_C, state_mlstm_n, state_mlstm_m,
           state_conv_d, state_ffn_conv, page_table, w_norm_mix, w_norm_ffn, w_in_even, b_gate_even,
           g_head_even, w_out_even, w_in_odd, g_vnorm, b_vnorm, w_spatial, b_spatial, w_dconv, b_dconv,
           g_dnorm, b_dnorm, w_out_odd, w_up, w_fconv, b_fconv, w_down, w_norm_final):
    bp, tp, d = x_prompt.shape
    bs = x_sample.shape[0]
    assert x_sample.shape[1] == 1
    ms = 16
    h_p = x_prompt.reshape(bp * tp, d)
    h_s = _pad_rows(x_sample.reshape(bs, d), ms)
    n_pages = page_table.shape[1]
    past = n_pages * PAGE

    w_even = _split_even_weights(w_in_even[0])[None]
    w_down_b = w_down.astype(BF16)
    b_i = b_gate_even[0, :A_HEADS]
    b_f = b_gate_even[0, A_HEADS:]
    g_head = g_head_even[0]

    xn = _rmsnorm(h_p, w_norm_mix[0], BF16)
    xs = _rmsnorm(h_s, w_norm_mix[0], BF16)
    pa, pa_s, pa_b = _matmul_ws(xn, xs, w_in_even, 0, 0, _N_A, emit_bf16=True)
    pb, pb_s, pb_b = _matmul_ws(xn, xs, w_even, 0, 0, _N_B, emit_bf16=True)
    small, small_s = _matmul_ws(xn, xs, w_even, 0, _N_B, LANE, tn=LANE)
    pa_s = pa_s[:bs]
    pb_s = pb_s[:bs]
    small_s = small_s[:bs]
    pa3 = pa.reshape(bp, tp, _N_A)
    pab3 = pa_b.reshape(bp, tp, _N_A)
    pb3 = pb.reshape(bp, tp, _N_B)
    pbb3 = pb_b.reshape(bp, tp, _N_B)
    small3 = small.reshape(bp, tp, LANE)

    def heads_t(x):
        return x.transpose(0, 2, 1).reshape(bp * A_HEADS, tp)

    a_g, g_g, m_g = _mlstm_gates(heads_t(small3[:, :, _S_AI:_S_AI + A_HEADS]),
                                 heads_t(small3[:, :, _S_AF:_S_AF + A_HEADS]),
                                 jnp.tile(b_i, bp).reshape(-1, 1), jnp.tile(b_f, bp).reshape(-1, 1))
    a4 = a_g.reshape(bp, A_HEADS, 1, tp)
    g4 = g_g.reshape(bp, A_HEADS, 1, tp)
    m4 = m_g.reshape(bp, A_HEADS, 1, tp)
    h_a = _mlstm_prompt(pa3, pab3, a4, g4, m4, g_head.reshape(A_HEADS, 1, A_DV))
    c_p, n_p = _mlstm_prompt_state(pa3, a4, g4)
    m_p = m_g.reshape(bp, A_HEADS, tp)[:, :, -1]
    h_b = _dsa_prompt(pbb3, small3)
    mix = jnp.concatenate([h_a, h_b], axis=-1).reshape(bp * tp, -1)

    k_p = pb3[:, :, _BK:_BK + B_KV_HEADS * B_DH].reshape(bp, tp, B_KV_HEADS, B_DH)
    v_p = pb3[:, :, _BV:_BV + B_KV_HEADS * B_DH].reshape(bp, tp, B_KV_HEADS, B_DH)
    ki_p = small3[:, :, _S_IK:_S_IK + IDX_DIM]

    q_s = pa_s[:, _AQ:_AQ + 1024].reshape(bs, A_HEADS, 1, A_DQK)
    kk_s = pa_s[:, _AK:_AK + 1024].reshape(bs, A_HEADS, 1, A_DQK)
    v_s = pa_s[:, _AV:_AV + 2048].reshape(bs, A_HEADS, A_DV, 1)
    o_s = pa_s[:, _AO:_AO + 2048].reshape(bs, A_HEADS, A_DV, 1)
    ig_s = (small_s[:, _S_AI:_S_AI + A_HEADS] + b_i).reshape(bs, A_HEADS, 1, 1)
    fp_s = (small_s[:, _S_AF:_S_AF + A_HEADS] + b_f).reshape(bs, A_HEADS, 1, 1)
    hcol_s, c_s, n_s, m_s = _mlstm_step(
        q_s, kk_s, v_s, o_s, g_head.reshape(A_HEADS, A_DV, 1), ig_s, fp_s,
        state_mlstm_C[0], state_mlstm_n[0].reshape(bs, A_HEADS, 1, A_DQK),
        state_mlstm_m[0].reshape(bs, A_HEADS, 1, 1))
    h_a_s = hcol_s.reshape(bs, A_HEADS * A_DV)

    bq_s = pb_s[:, _BQ:_BQ + 2048].reshape(bs, B_HEADS, B_DH)
    bk_s = pb_s[:, _BK:_BK + 512]
    bv_s = pb_s[:, _BV:_BV + 512]
    iq_s = pb_s[:, _IQ:_IQ + 1024].reshape(bs, IDX_HEADS, IDX_DIM)
    iw_s = small_s[:, _S_IW:_S_IW + IDX_HEADS]
    ik_s = small_s[:, _S_IK:_S_IK + IDX_DIM]
    sc_past = _dsa_step_scores(page_table, iq_s, iw_s.reshape(bs, IDX_HEADS, 1), cache_kidx, 0)
    own_page = lambda x: jnp.pad(x[:, None], ((0, 0), (0, PAGE - 1)) + ((0, 0),) * (x.ndim - 1))
    sc_self = _dsa_step_scores(jnp.arange(bs, dtype=I32).reshape(bs, 1), iq_s, iw_s.reshape(bs, IDX_HEADS, 1),
                               own_page(ik_s)[None], 0)[:, 0, 0:1]
    topk_s = min(TOPK_MAX, (past + 1) // 4)
    sel, sel_self = _dsa_step_select(sc_past.reshape(bs, past), sc_self, topk_s)
    h_b_s = _dsa_step_attend(page_table, bq_s, sel.reshape(bs, 1, past),
                             jnp.pad(sel_self, ((0, 0), (0, PAGE - 1))).reshape(bs, 1, PAGE),
                             own_page(bk_s.reshape(bs, B_KV_HEADS, B_DH)), own_page(bv_s.reshape(bs, B_KV_HEADS, B_DH)),
                             cache_k, cache_v, 0)
    mix_s = jnp.concatenate([h_a_s, h_b_s.reshape(bs, B_HEADS * B_DH)], axis=-1).astype(BF16)
    h_p, h_s = _matmul_ws(mix, _pad_rows(mix_s, ms), w_out_even, 0, 0, d, res=h_p, res_s=h_s)

    h_p, h_s, ff_p0, ff_s0 = _conv_ffn(h_p, h_s, tp, w_norm_ffn[0], 0, w_up, w_fconv, b_fconv, w_down_b,
                                       state_ffn_conv[0])

    width = w_in_odd.shape[2] // 4
    xn = _rmsnorm(h_p, w_norm_mix[1], BF16)
    xs = _rmsnorm(h_s, w_norm_mix[1], BF16)
    proj1, proj1_s = _matmul_ws(xn, xs, w_in_odd, 0, 0, 4 * width)
    proj1_s = proj1_s[:bs]
    mix1, tail = _odd_mix(proj1.reshape(bp, tp, 4 * width), g_vnorm[0], b_vnorm[0], w_spatial[0], b_spatial[0],
                          w_dconv[0], b_dconv[0], g_dnorm[0], b_dnorm[0])
    cd_p = tail[:, tail.shape[1] - (D_CONV - 1):, :]
    gw = width // C_GROUPS
    mix1_s, vn_s, glu_s = _odd_mix_step(
        proj1_s, state_conv_d[0].transpose(1, 0, 2), g_vnorm[0], b_vnorm[0],
        jnp.repeat(w_spatial[0, :, 0, 0], gw), jnp.repeat(b_spatial[0, :, 0], gw),
        w_dconv[0], b_dconv[0], g_dnorm[0], b_dnorm[0])
    h_p, h_s = _matmul_ws(mix1.reshape(bp * tp, 2 * width), _pad_rows(mix1_s.astype(BF16), ms), w_out_odd, 0, 0, d,
                          res=h_p, res_s=h_s)
    cd_s = jnp.concatenate([state_conv_d[0][:, 1:, :], glu_s[:, None, :]], axis=1)

    h_p, h_s, ff_p1, ff_s1 = _conv_ffn(h_p, h_s, tp, w_norm_ffn[1], 1, w_up, w_fconv, b_fconv, w_down_b,
                                       state_ffn_conv[1])

    y_p = _rmsnorm(h_p, w_norm_final, F32).reshape(bp, tp, d)
    y_s = _rmsnorm(h_s, w_norm_final, F32)[:bs].reshape(bs, 1, d)

    return (y_p, y_s,
            k_p[None], v_p[None], ki_p[None],
            bk_s.reshape(1, bs, 1, B_KV_HEADS, B_DH), bv_s.reshape(1, bs, 1, B_KV_HEADS, B_DH),
            ik_s.reshape(1, bs, 1, IDX_DIM),
            c_p[None], n_p.reshape(1, bp, A_HEADS, A_DQK), m_p[None],
            c_s[None], n_s.reshape(1, bs, A_HEADS, A_DQK), m_s.reshape(1, bs, A_HEADS),
            vn_s.reshape(1, bs, 1, width),
            cd_p[None], cd_s[None],
            jnp.stack([ff_p0, ff_p1]), jnp.stack([ff_s0, ff_s1]))
```

```python
import functools
import math

import jax
import jax.numpy as jnp
from jax import lax
from jax.experimental import pallas as pl
from jax.experimental.pallas import tpu as pltpu

F32 = jnp.float32
BF16 = jnp.bfloat16
I32 = jnp.int32
EPS = 1e-6
NEG_INF = float("-inf")
LOG2_E = math.log2(math.e)
INT_MIN = -(2 ** 31)

A_HEADS = 8
A_DQK = 128
A_DV = 256
B_HEADS = 16
B_KV_HEADS = 4
B_DH = 128
IDX_HEADS = 16
IDX_DIM = 64
TOPK_MAX = 256
PAGE = 128
C_GROUPS = 8
C_CHUNK = 128
D_CONV = 31
FF_CONV = 3
LANE = 128
VMEM_LIMIT_BYTES = 48 * 1024 * 1024
WS_VMEM_LIMIT_BYTES = 56 * 1024 * 1024
FFN_ROW_CHUNK = 1024
DSA_PAGES_PER_STEP = 8

_AQ, _AK, _AV, _AO = 0, 1024, 2048, 4096
_N_A = 6144
_BQ, _BK, _BV, _IQ = 0, 2048, 2560, 3072
_N_B = 4096
_S_AI, _S_AF, _S_IW, _S_IK = 0, 8, 16, 32


def _params(sem):
    return pltpu.CompilerParams(dimension_semantics=sem, vmem_limit_bytes=VMEM_LIMIT_BYTES)


def _dot(a, b):
    return jnp.dot(a, b, preferred_element_type=F32)


def _dot_nt(a, b):
    return lax.dot_general(a, b, (((1,), (1,)), ((), ())), preferred_element_type=F32)


def _sigmoid(x):
    return 1.0 / (1.0 + jnp.exp(-x))


def _col(row):
    n = row.shape[1]
    r = lax.broadcasted_iota(I32, (n, n), 0)
    c = lax.broadcasted_iota(I32, (n, n), 1)
    return jnp.sum(jnp.where(r == c, row, 0.0), axis=1, keepdims=True)


def _rms_kernel(x_ref, g_ref, o_ref):
    x = x_ref[...]
    y = x * lax.rsqrt(jnp.mean(x * x, axis=-1, keepdims=True) + EPS)
    o_ref[...] = (y * g_ref[...]).astype(o_ref.dtype)


def _rmsnorm(x, g, out_dtype):
    m, d = x.shape
    tm = min(m, 256)
    return pl.pallas_call(
        _rms_kernel,
        grid=(m // tm,),
        in_specs=[pl.BlockSpec((tm, d), lambda i: (i, 0)), pl.BlockSpec((1, d), lambda i: (0, 0))],
        out_specs=pl.BlockSpec((tm, d), lambda i: (i, 0)),
        out_shape=jax.ShapeDtypeStruct((m, d), out_dtype),
        compiler_params=_params(("parallel",)),
        name="rmsnorm",
    )(x, g.reshape(1, d))


def _mm_kernel(*refs, nk, has_res):
    if has_res:
        x_ref, w_ref, r_ref, o_ref = refs[:4]
        scratch = refs[4:]
    else:
        x_ref, w_ref, o_ref = refs[:3]
        r_ref = None
        scratch = refs[3:]
    if nk == 1:
        acc = _dot(x_ref[...], w_ref[...])
        if has_res:
            acc = acc + r_ref[...]
        o_ref[...] = acc.astype(o_ref.dtype)
        return
    acc_ref, = scratch
    k = pl.program_id(2)

    @pl.when(k == 0)
    def _():
        acc_ref[...] = jnp.zeros_like(acc_ref)

    acc_ref[...] += _dot(x_ref[...], w_ref[...])

    @pl.when(k == nk - 1)
    def _():
        acc = acc_ref[...]
        if has_res:
            acc = acc + r_ref[...]
        o_ref[...] = acc.astype(o_ref.dtype)


def _matmul(x, w, layer=0, res=None, out_dtype=F32, tm=1024, tn=512, tk=None):
    m, kdim = x.shape
    n = w.shape[2]
    tm = min(tm, m)
    tn = min(tn, n)
    tk = kdim if tk is None else tk
    assert m % tm == 0 and n % tn == 0 and kdim % tk == 0
    nk = kdim // tk
    in_specs = [pl.BlockSpec((tm, tk), lambda i, j, k: (i, k)),
                pl.BlockSpec((None, tk, tn), lambda i, j, k: (layer, k, j))]
    args = [x, w]
    if res is not None:
        in_specs.append(pl.BlockSpec((tm, tn), lambda i, j, k: (i, j)))
        args.append(res)
    return pl.pallas_call(
        functools.partial(_mm_kernel, nk=nk, has_res=res is not None),
        grid=(m // tm, n // tn, nk),
        in_specs=in_specs,
        out_specs=pl.BlockSpec((tm, tn), lambda i, j, k: (i, j)),
        out_shape=jax.ShapeDtypeStruct((m, n), out_dtype),
        scratch_shapes=[pltpu.VMEM((tm, tn), F32)] if nk > 1 else [],
        compiler_params=_params(("parallel", "parallel", "arbitrary")),
        name="matmul",
    )(*args)


def _mm_ws_kernel(*refs, has_res, emit_bf16, w_transposed):
    refs = list(refs)
    x_ref, xs_ref, w_ref = refs[:3]
    r_ref, rs_ref = refs[3:5] if has_res else (None, None)
    outs = refs[5:] if has_res else refs[3:]
    o_ref, os_ref = outs[:2]
    ob_ref = outs[2] if emit_bf16 else None
    wb_ref = outs[-1]

    @pl.when(pl.program_id(1) == 0)
    def _():
        wb_ref[...] = (w_ref[...].T if w_transposed else w_ref[...]).astype(BF16)
        acc_s = _dot(xs_ref[...], wb_ref[...])
        os_ref[...] = acc_s + rs_ref[...] if has_res else acc_s

    acc = _dot(x_ref[...], wb_ref[...])
    if has_res:
        acc = acc + r_ref[...]
    o_ref[...] = acc
    if emit_bf16:
        ob_ref[...] = acc.astype(BF16)


def _matmul_ws(x, xs, w, layer, col0, n, res=None, res_s=None, emit_bf16=False, w_transposed=False,
               tm=1024, tn=512):
    m, kdim = x.shape
    ms = xs.shape[0]
    assert m % tm == 0 and n % tn == 0 and col0 % tn == 0
    c0 = col0 // tn
    has_res = res is not None
    out_specs = [pl.BlockSpec((tm, tn), lambda j, i: (i, j)), pl.BlockSpec((ms, tn), lambda j, i: (0, j))]
    out_shape = [jax.ShapeDtypeStruct((m, n), F32), jax.ShapeDtypeStruct((ms, n), F32)]
    if emit_bf16:
        out_specs.append(pl.BlockSpec((tm, tn), lambda j, i: (i, j)))
        out_shape.append(jax.ShapeDtypeStruct((m, n), BF16))
    w_spec = (pl.BlockSpec((None, tn, kdim), lambda j, i: (layer, c0 + j, 0)) if w_transposed else
              pl.BlockSpec((None, kdim, tn), lambda j, i: (layer, 0, c0 + j)))
    in_specs = [pl.BlockSpec((tm, kdim), lambda j, i: (i, 0)),
                pl.BlockSpec((ms, kdim), lambda j, i: (0, 0)),
                w_spec]
    args = [x, xs, w]
    if has_res:
        in_specs += [pl.BlockSpec((tm, tn), lambda j, i: (i, j)), pl.BlockSpec((ms, tn), lambda j, i: (0, j))]
        args += [res, res_s]
    return pl.pallas_call(
        functools.partial(_mm_ws_kernel, has_res=has_res, emit_bf16=emit_bf16, w_transposed=w_transposed),
        grid=(n // tn, m // tm),
        in_specs=in_specs,
        out_specs=tuple(out_specs),
        out_shape=tuple(out_shape),
        scratch_shapes=[pltpu.VMEM((kdim, tn), BF16)],
        compiler_params=pltpu.CompilerParams(dimension_semantics=("parallel", "arbitrary"),
                                             vmem_limit_bytes=WS_VMEM_LIMIT_BYTES),
        name="matmul_ws",
    )(*args)


def _matmul_xres(x, w, layer, res, tm=512, tn=512):
    m, kdim = x.shape
    n = w.shape[2]
    assert m % tm == 0 and n % tn == 0
    return pl.pallas_call(
        functools.partial(_mm_kernel, nk=1, has_res=True),
        grid=(m // tm, n // tn),
        in_specs=[pl.BlockSpec((tm, kdim), lambda i, j: (i, 0)),
                  pl.BlockSpec((None, kdim, tn), lambda i, j: (layer, 0, j)),
                  pl.BlockSpec((tm, tn), lambda i, j: (i, j))],
        out_specs=pl.BlockSpec((tm, tn), lambda i, j: (i, j)),
        out_shape=jax.ShapeDtypeStruct((m, n), F32),
        compiler_params=_params(("parallel", "arbitrary")),
        name="matmul_xres",
    )(x, w, res)


def _conv_rows(h, prev, w, b):
    def taps(x, shifted):
        y = w[FF_CONV - 1:FF_CONV, :] * x + b
        for s in range(1, FF_CONV):
            y = y + w[FF_CONV - 1 - s:FF_CONV - s, :] * shifted(s)
        return y

    bulk = taps(h, lambda s: pltpu.roll(h, s, 0))
    h8 = h[0:8, :]
    row = lax.broadcasted_iota(I32, h8.shape, 0)

    def head_shift(s):
        x = pltpu.roll(h8, s, 0)
        for r in range(s):
            x = jnp.where(row == r, prev[8 - s + r:8 - s + r + 1, :], x)
        return x

    return jnp.concatenate([taps(h8, head_shift), bulk[8:, :]], axis=0)


def _ffn_up_kernel(x_ref, xs_ref, wg_ref, wv_ref, cg_ref, cv_ref, bg_ref, bv_ref,
                   act_ref, tg_ref, tv_ref, sg_ref, sv_ref, wb_ref, carry_ref, *, tn, tiles_per_seq):
    i = pl.program_id(1)

    @pl.when(i == 0)
    def _():
        wb_ref[:, 0:tn] = wg_ref[...].astype(BF16)
        wb_ref[:, tn:2 * tn] = wv_ref[...].astype(BF16)
        hs = _dot(xs_ref[...], wb_ref[...])
        sg_ref[...] = hs[:, 0:tn]
        sv_ref[...] = hs[:, tn:2 * tn]

    @pl.when(i % tiles_per_seq == 0)
    def _():
        carry_ref[...] = jnp.zeros_like(carry_ref)

    tm = x_ref.shape[0]
    rc = min(tm, FFN_ROW_CHUNK)
    prev = carry_ref[...]
    for r in range(tm // rc):
        h = _dot(x_ref[r * rc:(r + 1) * rc, :], wb_ref[...])
        g = _conv_rows(h[:, 0:tn], prev[:, 0:tn], cg_ref[...], bg_ref[...])
        v = _conv_rows(h[:, tn:2 * tn], prev[:, tn:2 * tn], cv_ref[...], bv_ref[...])
        act_ref[r * rc:(r + 1) * rc, :] = (g * _sigmoid(g) * v).astype(act_ref.dtype)
        prev = h[rc - 8:rc, :]
    carry_ref[...] = prev

    @pl.when(i % tiles_per_seq == tiles_per_seq - 1)
    def _():
        tg_ref[0] = prev[:, 0:tn]
        tv_ref[0] = prev[:, tn:2 * tn]


def _ffn_up(x, xs, w_up, w_fc, b_fc, layer, seq_len, tm=1024, tn=256):
    m, kdim = x.shape
    ms = xs.shape[0]
    f = w_up.shape[2] // 2
    nf = f // tn
    tm = min(tm, seq_len)
    assert m % tm == 0 and seq_len % tm == 0 and f % tn == 0
    tiles_per_seq = seq_len // tm
    nseq = m // seq_len
    return pl.pallas_call(
        functools.partial(_ffn_up_kernel, tn=tn, tiles_per_seq=tiles_per_seq),
        grid=(nf, m // tm),
        in_specs=[pl.BlockSpec((tm, kdim), lambda j, i: (i, 0)),
                  pl.BlockSpec((ms, kdim), lambda j, i: (0, 0)),
                  pl.BlockSpec((None, kdim, tn), lambda j, i: (layer, 0, j)),
                  pl.BlockSpec((None, kdim, tn), lambda j, i: (layer, 0, j + nf)),
                  pl.BlockSpec((None, FF_CONV, tn), lambda j, i: (layer, 0, j)),
                  pl.BlockSpec((None, FF_CONV, tn), lambda j, i: (layer, 0, j + nf)),
                  pl.BlockSpec((None, 1, tn), lambda j, i: (layer, 0, j)),
                  pl.BlockSpec((None, 1, tn), lambda j, i: (layer, 0, j + nf))],
        out_specs=(pl.BlockSpec((tm, tn), lambda j, i: (i, j)),
                   pl.BlockSpec((1, 8, tn), lambda j, i: (i // tiles_per_seq, 0, j)),
                   pl.BlockSpec((1, 8, tn), lambda j, i: (i // tiles_per_seq, 0, j)),
                   pl.BlockSpec((ms, tn), lambda j, i: (0, j)),
                   pl.BlockSpec((ms, tn), lambda j, i: (0, j))),
        out_shape=(jax.ShapeDtypeStruct((m, f), BF16),
                   jax.ShapeDtypeStruct((nseq, 8, f), F32), jax.ShapeDtypeStruct((nseq, 8, f), F32),
                   jax.ShapeDtypeStruct((ms, f), F32), jax.ShapeDtypeStruct((ms, f), F32)),
        scratch_shapes=[pltpu.VMEM((kdim, 2 * tn), BF16), pltpu.VMEM((8, 2 * tn), F32)],
        compiler_params=pltpu.CompilerParams(dimension_semantics=("parallel", "arbitrary"),
                                             vmem_limit_bytes=WS_VMEM_LIMIT_BYTES),
        name="ffn_up",
    )(x, xs, w_up, w_up, w_fc, w_fc, b_fc, b_fc)


def _ffn_act_step_kernel(hg_ref, hv_ref, bufg_ref, bufv_ref, wg_ref, wv_ref, bg_ref, bv_ref, o_ref):
    def conv(h_ref, buf_ref, w_ref, b_ref):
        w = w_ref[...]
        y = w[FF_CONV - 1:FF_CONV, :] * h_ref[...] + b_ref[...]
        for s in range(FF_CONV - 1):
            y = y + w[s:s + 1, :] * buf_ref[s]
        return y

    g = conv(hg_ref, bufg_ref, wg_ref, bg_ref)
    o_ref[...] = g * _sigmoid(g) * conv(hv_ref, bufv_ref, wv_ref, bv_ref)


def _ffn_act_step(h_g, h_v, buf, w_fc, b_fc):
    b, f = h_g.shape
    return pl.pallas_call(
        _ffn_act_step_kernel,
        out_shape=jax.ShapeDtypeStruct((b, f), F32),
        compiler_params=pltpu.CompilerParams(vmem_limit_bytes=VMEM_LIMIT_BYTES),
        name="ffn_act_step",
    )(h_g, h_v, buf[:, :, :f], buf[:, :, f:], w_fc[:, :f], w_fc[:, f:],
      b_fc[:, :f], b_fc[:, f:])


def _scan_lanes(x, op, fill):
    n = x.shape[1]
    lane = lax.broadcasted_iota(I32, x.shape, 1)
    d = 1
    while d < n:
        x = op(x, jnp.where(lane >= d, pltpu.roll(x, d, 1), fill))
        d *= 2
    return x


def _log_sigmoid(x):
    return jnp.minimum(x, 0.0) - jnp.log(1.0 + jnp.exp(-jnp.abs(x)))


def _gate_kernel(i_ref, f_ref, bi_ref, bf_ref, a_ref, g_ref, m_ref):
    ig = i_ref[...] + bi_ref[...]
    lf = _log_sigmoid(f_ref[...] + bf_ref[...])
    bsum = _scan_lanes(lf, jnp.add, 0.0)
    a = ig - bsum
    g = jnp.maximum(_scan_lanes(a, jnp.maximum, NEG_INF), 0.0)
    a_ref[...] = a
    g_ref[...] = g
    m_ref[...] = bsum + g


def _mlstm_gates(i_t, f_t, b_i, b_f):
    shp = jax.ShapeDtypeStruct(i_t.shape, F32)
    return pl.pallas_call(
        _gate_kernel, out_shape=(shp, shp, shp),
        compiler_params=pltpu.CompilerParams(vmem_limit_bytes=VMEM_LIMIT_BYTES),
        name="mlstm_gates",
    )(i_t, f_t, b_i, b_f)


def _mlstm_kernel(q_ref, k_ref, v_ref, o_ref, a_ref, g_ref, m_ref, gh_ref, out_ref, num_sc, den_sc, *, tq, hp):
    qi = pl.program_id(2)
    off_q = pl.multiple_of(qi * tq, tq)
    row = lax.broadcasted_iota(I32, (tq, tq), 0)
    col = lax.broadcasted_iota(I32, (tq, tq), 1)
    qs = [(q_ref[0, :, hh * A_DQK:(hh + 1) * A_DQK] * (A_DQK ** -0.5)).astype(BF16) for hh in range(hp)]
    g_cs = [_col(g_ref[0, hh, :, pl.ds(off_q, tq)]) for hh in range(hp)]
    num_sc[...] = jnp.zeros_like(num_sc)
    den_sc[...] = jnp.zeros_like(den_sc)

    def body(kb, carry):
        off = pl.multiple_of(kb * tq, tq)
        causal = (col + kb * tq) <= (row + qi * tq)
        for hh in range(hp):
            k = k_ref[0, pl.ds(off, tq), hh * A_DQK:(hh + 1) * A_DQK]
            v = v_ref[0, pl.ds(off, tq), hh * A_DV:(hh + 1) * A_DV]
            a = a_ref[0, hh, :, pl.ds(off, tq)]
            w = jnp.exp(jnp.where(causal, a - g_cs[hh], NEG_INF))
            s = _dot_nt(qs[hh], k) * w
            num_sc[hh] += _dot(s.astype(BF16), v)
            den_sc[hh] += jnp.sum(s, axis=1, keepdims=True)
        return carry

    lax.fori_loop(0, qi + 1, body, 0)
    for hh in range(hp):
        m_c = _col(m_ref[0, hh, :, pl.ds(off_q, tq)])
        h = num_sc[hh] / jnp.maximum(jnp.abs(den_sc[hh]), jnp.exp(-m_c))
        h = h * lax.rsqrt(jnp.mean(h * h, axis=-1, keepdims=True) + EPS) * gh_ref[hh]
        cols = slice(hh * A_DV, (hh + 1) * A_DV)
        out_ref[0, :, cols] = (_sigmoid(o_ref[0, :, cols]) * h).astype(out_ref.dtype)


def _mlstm_prompt(p3, pb3, a, g, m, g_head):
    b, t, _ = p3.shape
    tq = 256
    hp = 4
    row_spec = pl.BlockSpec((1, hp, 1, t), lambda i, h, j: (i, h, 0, 0))
    wq, wv = hp * A_DQK, hp * A_DV
    return pl.pallas_call(
        functools.partial(_mlstm_kernel, tq=tq, hp=hp),
        grid=(b, A_HEADS // hp, t // tq),
        in_specs=[pl.BlockSpec((1, tq, wq), lambda i, h, j: (i, j, _AQ // wq + h)),
                  pl.BlockSpec((1, t, wq), lambda i, h, j: (i, 0, _AK // wq + h)),
                  pl.BlockSpec((1, t, wv), lambda i, h, j: (i, 0, _AV // wv + h)),
                  pl.BlockSpec((1, tq, wv), lambda i, h, j: (i, j, _AO // wv + h)),
                  row_spec, row_spec, row_spec,
                  pl.BlockSpec((hp, 1, A_DV), lambda i, h, j: (h, 0, 0))],
        out_specs=pl.BlockSpec((1, tq, wv), lambda i, h, j: (i, j, h)),
        out_shape=jax.ShapeDtypeStruct((b, t, A_HEADS * A_DV), BF16),
        scratch_shapes=[pltpu.VMEM((hp, tq, A_DV), F32), pltpu.VMEM((hp, tq, 1), F32)],
        compiler_params=_params(("parallel", "parallel", "arbitrary")),
        name="mlstm_prompt",
    )(p3, pb3, pb3, p3, a, g, m, g_head)


def _mlstm_state_kernel(k_ref, v_ref, a_ref, g_ref, c_ref, n_ref, *, t, tc):
    g_last = g_ref[0, 0, :, t - 1:t]

    def body(cb, carry):
        c_acc, n_acc = carry
        off = pl.multiple_of(cb * tc, tc)
        w = _col(jnp.exp(a_ref[0, 0, :, pl.ds(off, tc)] - g_last))
        kw = k_ref[0, pl.ds(off, tc), :] * w
        vt = v_ref[0, pl.ds(off, tc), :].T.astype(BF16)
        return c_acc + _dot(vt, kw.astype(BF16)), n_acc + jnp.sum(kw, axis=0, keepdims=True)

    c_acc, n_acc = lax.fori_loop(0, t // tc, body,
                                 (jnp.zeros((A_DV, A_DQK), F32), jnp.zeros((1, A_DQK), F32)))
    c_ref[0, 0] = c_acc
    n_ref[0, 0] = n_acc


def _mlstm_prompt_state(p3, a, g):
    b, t, _ = p3.shape
    row_spec = pl.BlockSpec((1, 1, 1, t), lambda i, h: (i, h, 0, 0))
    return pl.pallas_call(
        functools.partial(_mlstm_state_kernel, t=t, tc=256),
        grid=(b, A_HEADS),
        in_specs=[pl.BlockSpec((1, t, A_DQK), lambda i, h: (i, 0, _AK // A_DQK + h)),
                  pl.BlockSpec((1, t, A_DV), lambda i, h: (i, 0, _AV // A_DV + h)),
                  row_spec, row_spec],
        out_specs=(pl.BlockSpec((1, 1, A_DV, A_DQK), lambda i, h: (i, h, 0, 0)),
                   pl.BlockSpec((1, 1, 1, A_DQK), lambda i, h: (i, h, 0, 0))),
        out_shape=(jax.ShapeDtypeStruct((b, A_HEADS, A_DV, A_DQK), F32),
                   jax.ShapeDtypeStruct((b, A_HEADS, 1, A_DQK), F32)),
        compiler_params=_params(("parallel", "parallel")),
        name="mlstm_prompt_state",
    )(p3, p3, a, g)


def _mlstm_step_kernel(q_ref, k_ref, v_ref, o_ref, gh_ref, ig_ref, fp_ref, c0_ref, n0_ref, m0_ref,
                       h_ref, c_ref, n_ref, m_ref):
    q = q_ref[0, 0] * (A_DQK ** -0.5)
    k = k_ref[0, 0]
    v = v_ref[0, 0]
    ig = ig_ref[0, 0]
    lf = _log_sigmoid(fp_ref[0, 0])
    c0 = c0_ref[0, 0]
    n0 = n0_ref[0, 0]
    m0 = m0_ref[0, 0]
    log_inter = lf + m0
    m_t = jnp.maximum(log_inter, ig)
    w_intra = jnp.exp(ig - m_t)
    w_inter = jnp.exp(log_inter - m_t)
    s = jnp.sum(q * k, axis=1, keepdims=True) * w_intra
    num = s * v + w_inter * jnp.sum(c0 * q, axis=1, keepdims=True)
    den = s + w_inter * jnp.sum(n0 * q, axis=1, keepdims=True)
    h = num / jnp.maximum(jnp.abs(den), jnp.exp(-m_t))
    h = h * lax.rsqrt(jnp.mean(h * h, axis=0, keepdims=True) + EPS) * gh_ref[0]
    h_ref[0, 0] = _sigmoid(o_ref[0, 0]) * h
    c_ref[0, 0] = w_inter * c0 + w_intra * (v * k)
    n_ref[0, 0] = w_inter * n0 + w_intra * k
    m_ref[0, 0] = m_t


def _mlstm_step(q, k, v_col, o_col, gh_col, ig, f_pre, c0, n0, m0):
    b = q.shape[0]
    h = A_HEADS
    row = pl.BlockSpec((1, 1, 1, A_DQK), lambda i, j: (i, j, 0, 0))
    colv = pl.BlockSpec((1, 1, A_DV, 1), lambda i, j: (i, j, 0, 0))
    one = pl.BlockSpec((1, 1, 1, 1), lambda i, j: (i, j, 0, 0))
    mat = pl.BlockSpec((1, 1, A_DV, A_DQK), lambda i, j: (i, j, 0, 0))
    return pl.pallas_call(
        _mlstm_step_kernel,
        grid=(b, h),
        in_specs=[row, row, colv, colv, pl.BlockSpec((1, A_DV, 1), lambda i, j: (j, 0, 0)),
                  one, one, mat, row, one],
        out_specs=(colv, mat, row, one),
        out_shape=(jax.ShapeDtypeStruct((b, h, A_DV, 1), F32),
                   jax.ShapeDtypeStruct((b, h, A_DV, A_DQK), F32),
                   jax.ShapeDtypeStruct((b, h, 1, A_DQK), F32),
                   jax.ShapeDtypeStruct((b, h, 1, 1), F32)),
        compiler_params=_params(("parallel", "parallel")),
        name="mlstm_step",
    )(q, k, v_col, o_col, gh_col, ig, f_pre, c0, n0, m0)


def _sort_key(x):
    x = jnp.where(x == 0.0, 0.0, x)
    bits = pltpu.bitcast(x, I32)
    return jnp.where(bits >= 0, bits, bits ^ jnp.int32(0x7FFFFFFF))


def _kth_largest_key(key, k):
    def count_ge(c):
        return jnp.sum(jnp.where(key >= c, 1.0, 0.0), axis=1, keepdims=True)

    kf = jnp.float32(k)
    thr = jnp.where(count_ge(jnp.int32(0)) >= kf, jnp.int32(0), jnp.int32(INT_MIN))
    thr = jnp.broadcast_to(thr, (key.shape[0], 1))

    def body(i, thr):
        cand = thr + lax.shift_left(jnp.int32(1), jnp.int32(30) - i)
        return jnp.where(count_ge(cand) >= kf, cand, thr)

    return lax.fori_loop(0, 31, body, thr)


def _tri_ones(n, dtype):
    r = lax.broadcasted_iota(I32, (n, n), 0)
    c = lax.broadcasted_iota(I32, (n, n), 1)
    return jnp.where(r <= c, 1.0, 0.0).astype(dtype)


def _select_topk(scores, visible, k):
    key = _sort_key(jnp.where(visible, scores, NEG_INF))
    thr = _kth_largest_key(key, k)
    gt = key > thr
    eq = jnp.logical_and(key == thr, visible)
    need = jnp.float32(k) - jnp.sum(jnp.where(gt, 1.0, 0.0), axis=1, keepdims=True)
    tri = _tri_ones(LANE, BF16)
    off = jnp.zeros((scores.shape[0], 1), F32)
    rank = []
    eqf = jnp.where(eq, 1.0, 0.0)
    for c in range(scores.shape[1] // LANE):
        p = _dot(eqf[:, c * LANE:(c + 1) * LANE].astype(BF16), tri) + off
        rank.append(p)
        off = p[:, LANE - 1:LANE]
    rank = jnp.concatenate(rank, axis=1)
    return jnp.logical_or(jnp.logical_and(gt, visible), jnp.logical_and(eq, rank <= need))


def _indexer_scores(qi_head, wi, ki):
    acc = None
    for h in range(IDX_HEADS):
        s = _dot_nt(qi_head(h).astype(BF16), ki) * (IDX_DIM ** -0.5)
        term = jnp.maximum(s, 0.0) * wi[:, h:h + 1]
        acc = term if acc is None else acc + term
    return acc * (IDX_HEADS ** -0.5)


def _dsa_prompt_kernel(q_ref, k_ref, v_ref, qi_ref, sm_ref, smq_ref, o_ref, sc_ref, *, tq, topk, extents):
    j = pl.program_id(1)
    group = B_HEADS // B_KV_HEADS
    lane = lax.broadcasted_iota(I32, (tq, LANE), 1)

    def attend(e):
        ki = sm_ref[0, 0:e, _S_IK:_S_IK + IDX_DIM].astype(BF16)
        sc_ref[:, 0:e] = jnp.zeros((tq, e), F32)

        def idx_pair(hp, carry):
            qpair = qi_ref[0, :, pl.ds(pl.multiple_of(hp * LANE, LANE), LANE)]
            for half in range(LANE // IDX_DIM):
                s = _dot_nt(qpair[:, half * IDX_DIM:(half + 1) * IDX_DIM], ki) * (IDX_DIM ** -0.5)
                head = hp * (LANE // IDX_DIM) + half
                w = jnp.sum(jnp.where(lane == _S_IW + head, smq_ref[0], 0.0), axis=1, keepdims=True)
                sc_ref[:, 0:e] += jnp.maximum(s, 0.0) * w
            return carry

        lax.fori_loop(0, IDX_HEADS * IDX_DIM // LANE, idx_pair, 0)
        scores = sc_ref[:, 0:e] * (IDX_HEADS ** -0.5)
        qpos = lax.broadcasted_iota(I32, (tq, e), 0) + j * tq
        kpos = lax.broadcasted_iota(I32, (tq, e), 1)
        sel = _select_topk(scores, kpos <= qpos, topk)
        sc_ref[:, 0:e] = jnp.where(sel, 0.0, NEG_INF)

        def kv_head(n, carry):
            kv_off = pl.multiple_of(n * B_DH, B_DH)
            kn = k_ref[0, 0:e, pl.ds(kv_off, B_DH)]
            vn = v_ref[0, 0:e, pl.ds(kv_off, B_DH)]
            for gidx in range(group):
                q_off = pl.multiple_of((n * group + gidx) * B_DH, B_DH)
                s = _dot_nt(q_ref[0, :, pl.ds(q_off, B_DH)], kn) + sc_ref[:, 0:e]
                p = jnp.exp2((s - jnp.max(s, axis=1, keepdims=True)) * (B_DH ** -0.5 * LOG2_E))
                l = jnp.sum(p, axis=1, keepdims=True)
                o_ref[0, :, pl.ds(q_off, B_DH)] = (_dot(p.astype(BF16), vn) / l).astype(o_ref.dtype)
            return carry

        lax.fori_loop(0, B_KV_HEADS, kv_head, 0)

    last = (j + 1) * tq
    for idx, e in enumerate(extents):
        lo = extents[idx - 1] if idx else 0
        pl.when(jnp.logical_and(last > lo, last <= e))(functools.partial(attend, e))


def _dsa_prompt(pb3, small3):
    b, t, _ = pb3.shape
    tq = 256
    topk = min(TOPK_MAX, t // 4)
    n_ext = min(4, t // tq)
    extents = tuple(t * (i + 1) // n_ext for i in range(n_ext))
    assert all(e % tq == 0 for e in extents) and extents[0] >= topk
    w_q, w_kv, w_qi = B_HEADS * B_DH, B_KV_HEADS * B_DH, IDX_HEADS * IDX_DIM
    return pl.pallas_call(
        functools.partial(_dsa_prompt_kernel, tq=tq, topk=topk, extents=extents),
        grid=(b, t // tq),
        in_specs=[pl.BlockSpec((1, tq, w_q), lambda i, j: (i, j, _BQ // w_q)),
                  pl.BlockSpec((1, t, w_kv), lambda i, j: (i, 0, _BK // w_kv)),
                  pl.BlockSpec((1, t, w_kv), lambda i, j: (i, 0, _BV // w_kv)),
                  pl.BlockSpec((1, tq, w_qi), lambda i, j: (i, j, _IQ // w_qi)),
                  pl.BlockSpec((1, t, LANE), lambda i, j: (i, 0, 0)),
                  pl.BlockSpec((1, tq, LANE), lambda i, j: (i, j, 0))],
        out_specs=pl.BlockSpec((1, tq, w_q), lambda i, j: (i, j, 0)),
        out_shape=jax.ShapeDtypeStruct((b, t, w_q), BF16),
        scratch_shapes=[pltpu.VMEM((tq, t), F32)],
        compiler_params=_params(("parallel", "parallel")),
        name="dsa_prompt",
    )(pb3, pb3, pb3, pb3, small3, small3)


def _page_specs(block, layer, pages_per_step):
    def spec(r):
        zeros = (0,) * (len(block) - 2)
        return pl.BlockSpec(block, lambda i, p, pt: (layer, pt[i, p * pages_per_step + r]) + zeros)
    return [spec(r) for r in range(pages_per_step)]


def _dsa_step_scores_kernel(pt_ref, qi_ref, wi_ref, *refs):
    del pt_ref
    ki_refs, o_ref = refs[:-1], refs[-1]
    qi = qi_ref[0].astype(BF16)
    ki_t = jnp.concatenate([r[0] for r in ki_refs], axis=1).astype(BF16)
    s = _dot(qi, ki_t) * (IDX_DIM ** -0.5)
    o_ref[0] = jnp.sum(jnp.maximum(s, 0.0) * wi_ref[0], axis=0, keepdims=True) * (IDX_HEADS ** -0.5)


def _dsa_step_scores(page_table, qi, wi_col, cache_ki, layer):
    b, n_pages = page_table.shape
    g = math.gcd(n_pages, DSA_PAGES_PER_STEP)
    grid_spec = pltpu.PrefetchScalarGridSpec(
        num_scalar_prefetch=1,
        grid=(b, n_pages // g),
        in_specs=[pl.BlockSpec((1, IDX_HEADS, IDX_DIM), lambda i, p, pt: (i, 0, 0)),
                  pl.BlockSpec((1, IDX_HEADS, 1), lambda i, p, pt: (i, 0, 0))]
                 + _page_specs((None, 1, IDX_DIM, PAGE), layer, g),
        out_specs=pl.BlockSpec((1, 1, g * PAGE), lambda i, p, pt: (i, 0, p)),
    )
    return pl.pallas_call(
        _dsa_step_scores_kernel,
        grid_spec=grid_spec,
        out_shape=jax.ShapeDtypeStruct((b, 1, n_pages * PAGE), F32),
        compiler_params=_params(("parallel", "arbitrary")),
        name="dsa_step_scores",
    )(page_table, qi, wi_col, *([cache_ki] * g))


def _dsa_step_select_kernel(sc_ref, self_ref, sel_ref, selself_ref, *, topk):
    sc = sc_ref[...]
    sself = self_ref[...]
    key = _sort_key(sc)
    key_self = _sort_key(sself)
    kf = jnp.float32(topk)

    def count_ge(c):
        return (jnp.sum(jnp.where(key >= c, 1.0, 0.0), axis=1, keepdims=True)
                + jnp.where(key_self >= c, 1.0, 0.0))

    thr = jnp.where(count_ge(jnp.int32(0)) >= kf, jnp.int32(0), jnp.int32(INT_MIN))

    def body(i, thr):
        cand = thr + lax.shift_left(jnp.int32(1), jnp.int32(30) - i)
        return jnp.where(count_ge(cand) >= kf, cand, thr)

    thr = lax.fori_loop(0, 31, body, thr)
    gt = key > thr
    eq = key == thr
    n_gt = jnp.sum(jnp.where(gt, 1.0, 0.0), axis=1, keepdims=True) + jnp.where(key_self > thr, 1.0, 0.0)
    need = kf - n_gt
    rank = _scan_lanes(jnp.where(eq, 1.0, 0.0), jnp.add, 0.0)
    sel_ref[...] = jnp.where(jnp.logical_or(gt, jnp.logical_and(eq, rank <= need)), 1.0, 0.0)
    n_eq = rank[:, -1:]
    selself_ref[...] = jnp.where(
        jnp.logical_or(key_self > thr, jnp.logical_and(key_self == thr, n_eq < need)), 1.0, 0.0)


def _dsa_step_select(scores, self_score, topk):
    b, s = scores.shape
    return pl.pallas_call(
        functools.partial(_dsa_step_select_kernel, topk=topk),
        out_shape=(jax.ShapeDtypeStruct((b, s), F32), jax.ShapeDtypeStruct((b, 1), F32)),
        compiler_params=pltpu.CompilerParams(vmem_limit_bytes=VMEM_LIMIT_BYTES),
        name="dsa_step_select",
    )(scores, self_score)


def _dsa_step_attend_kernel(pt_ref, q_ref, sel_ref, selself_ref, kn_ref, vn_ref, *refs, pages_per_step):
    del pt_ref
    g = pages_per_step
    k_refs, v_refs = refs[:g], refs[g:2 * g]
    o_ref, m_sc, l_sc, acc_sc = refs[2 * g:]
    p_idx = pl.program_id(1)
    group = B_HEADS // B_KV_HEADS

    @pl.when(p_idx == 0)
    def _():
        m_sc[...] = jnp.full_like(m_sc, NEG_INF)
        l_sc[...] = jnp.zeros_like(l_sc)
        acc_sc[...] = jnp.zeros_like(acc_sc)

    q = q_ref[0].astype(BF16)
    kv_of_row = lax.broadcasted_iota(I32, (B_HEADS, 1), 0) // group

    def head_rows(page_refs, n):
        return jnp.concatenate([r[0, :, n, :] for r in page_refs], axis=0).astype(BF16)

    def update(k_pages, v_pages, sel_row):
        s = jnp.zeros((B_HEADS, sel_row.shape[1]), F32)
        for n in range(B_KV_HEADS):
            s = jnp.where(kv_of_row == n, _dot_nt(q, head_rows(k_pages, n)), s)
        s = s * (B_DH ** -0.5) + jnp.where(sel_row > 0.0, 0.0, NEG_INF)
        m_old = m_sc[...]
        m_new = jnp.maximum(m_old, jnp.max(s, axis=1, keepdims=True))
        m_safe = jnp.where(m_new == NEG_INF, 0.0, m_new)
        alpha = jnp.exp(m_old - m_safe)
        p = jnp.exp(s - m_safe)
        pb = p.astype(BF16)
        pv = jnp.zeros((B_HEADS, B_DH), F32)
        for n in range(B_KV_HEADS):
            pv = jnp.where(kv_of_row == n, _dot(pb, head_rows(v_pages, n)), pv)
        l_sc[...] = alpha * l_sc[...] + jnp.sum(p, axis=1, keepdims=True)
        acc_sc[...] = alpha * acc_sc[...] + pv
        m_sc[...] = m_new

    update(k_refs, v_refs, sel_ref[0])

    @pl.when(p_idx == pl.num_programs(1) - 1)
    def _():
        update([kn_ref], [vn_ref], selself_ref[0])
        o_ref[0] = acc_sc[...] / l_sc[...]


def _dsa_step_attend(page_table, q, sel, sel_self, k_new, v_new, cache_k, cache_v, layer):
    b, n_pages = page_table.shape
    g = math.gcd(n_pages, DSA_PAGES_PER_STEP)
    own = pl.BlockSpec((1, PAGE, B_KV_HEADS, B_DH), lambda i, p, pt: (i, 0, 0, 0))
    page_block = (None, 1, PAGE, B_KV_HEADS, B_DH)
    grid_spec = pltpu.PrefetchScalarGridSpec(
        num_scalar_prefetch=1,
        grid=(b, n_pages // g),
        in_specs=[pl.BlockSpec((1, B_HEADS, B_DH), lambda i, p, pt: (i, 0, 0)),
                  pl.BlockSpec((1, 1, g * PAGE), lambda i, p, pt: (i, 0, p)),
                  pl.BlockSpec((1, 1, PAGE), lambda i, p, pt: (i, 0, 0)),
                  own, own]
                 + _page_specs(page_block, layer, g) + _page_specs(page_block, layer, g),
        out_specs=pl.BlockSpec((1, B_HEADS, B_DH), lambda i, p, pt: (i, 0, 0)),
        scratch_shapes=[pltpu.VMEM((B_HEADS, 1), F32), pltpu.VMEM((B_HEADS, 1), F32),
                        pltpu.VMEM((B_HEADS, B_DH), F32)],
    )
    return pl.pallas_call(
        functools.partial(_dsa_step_attend_kernel, pages_per_step=g),
        grid_spec=grid_spec,
        out_shape=jax.ShapeDtypeStruct((b, B_HEADS, B_DH), F32),
        compiler_params=_params(("parallel", "arbitrary")),
        name="dsa_step_attend",
    )(page_table, q, sel, sel_self, k_new, v_new, *([cache_k] * g), *([cache_v] * g))


def _layernorm(x, g, b):
    xc = x - jnp.mean(x, axis=-1, keepdims=True)
    return xc * lax.rsqrt(jnp.mean(xc * xc, axis=-1, keepdims=True) + EPS) * g + b


def _odd_mix_kernel(u_ref, v_ref, a_ref, gt_ref, gv_ref, bv_ref, wsp_ref, bsp_ref, wdc_ref, bdc_ref,
                    gdn_ref, bdn_ref, o_ref, tail_ref, xbuf, slab, *, tt, width, halo):
    j = pl.program_id(1)
    nj = pl.num_programs(1)

    @pl.when(j == 0)
    def _():
        xbuf[0:halo, :] = jnp.zeros((halo, width), F32)
        xbuf[halo + tt:halo + tt + 8, :] = jnp.zeros((8, width), F32)

    vn = _layernorm(v_ref[0], gv_ref[...], bv_ref[...]).astype(BF16)
    r = lax.broadcasted_iota(I32, (tt, tt), 0)
    c = lax.broadcasted_iota(I32, (tt, tt), 1)
    gw = width // C_GROUPS
    for gi in range(C_GROUPS):
        wm = jnp.where(c <= r, wsp_ref[gi], 0.0).astype(BF16)
        sg = _dot(wm, vn[:, gi * gw:(gi + 1) * gw]) + bsp_ref[gi]
        o_ref[0, :, gi * gw:(gi + 1) * gw] = (u_ref[0, :, gi * gw:(gi + 1) * gw] * sg).astype(o_ref.dtype)

    glu = a_ref[0] * _sigmoid(gt_ref[0])
    xbuf[halo:halo + tt, :] = glu
    wdc = wdc_ref[...]
    dc = jnp.zeros((tt, width), F32) + bdc_ref[...]
    base = halo - (D_CONV - 1)
    for b8 in range(8):
        taps = [kk for kk in range(D_CONV) if (base + kk) % 8 == b8]
        src = xbuf
        if b8:
            slab[...] = xbuf[b8:b8 + tt + halo, :]
            src = slab
        for kk in taps:
            a8 = (base + kk) - b8
            dc = dc + wdc[kk:kk + 1, :] * src[a8:a8 + tt, :]
    d = _layernorm(dc, gdn_ref[...], bdn_ref[...])
    o_ref[0, :, width:2 * width] = (d * _sigmoid(d)).astype(o_ref.dtype)

    xbuf[0:halo, :] = glu[tt - halo:tt, :]

    @pl.when(j == nj - 1)
    def _():
        tail_ref[0] = glu[tt - halo:tt, :]


def _odd_mix(p3, g_v, b_v, w_sp, b_sp, w_dc, b_dc, g_dn, b_dn):
    b, t, w4 = p3.shape
    width = w4 // 4
    tt = C_CHUNK
    halo = 32
    vec = lambda x: x.reshape(1, width)
    vspec = pl.BlockSpec((1, width), lambda i, j: (0, 0))
    return pl.pallas_call(
        functools.partial(_odd_mix_kernel, tt=tt, width=width, halo=halo),
        grid=(b, t // tt),
        in_specs=[pl.BlockSpec((1, tt, width), lambda i, j: (i, j, 0)),
                  pl.BlockSpec((1, tt, width), lambda i, j: (i, j, 1)),
                  pl.BlockSpec((1, tt, width), lambda i, j: (i, j, 2)),
                  pl.BlockSpec((1, tt, width), lambda i, j: (i, j, 3)),
                  vspec, vspec,
                  pl.BlockSpec((C_GROUPS, tt, tt), lambda i, j: (0, 0, 0)),
                  pl.BlockSpec((C_GROUPS, tt, 1), lambda i, j: (0, 0, 0)),
                  pl.BlockSpec((D_CONV, width), lambda i, j: (0, 0)),
                  vspec, vspec, vspec],
        out_specs=(pl.BlockSpec((1, tt, 2 * width), lambda i, j: (i, j, 0)),
                   pl.BlockSpec((1, halo, width), lambda i, j: (i, 0, 0))),
        out_shape=(jax.ShapeDtypeStruct((b, t, 2 * width), BF16),
                   jax.ShapeDtypeStruct((b, halo, width), F32)),
        scratch_shapes=[pltpu.VMEM((halo + tt + 8, width), F32), pltpu.VMEM((halo + tt, width), F32)],
        compiler_params=_params(("parallel", "arbitrary")),
        name="odd_mix",
    )(p3, p3, p3, p3, vec(g_v), vec(b_v), w_sp, b_sp.reshape(C_GROUPS, tt, 1), w_dc, vec(b_dc),
      vec(g_dn), vec(b_dn))


def _odd_mix_step_kernel(p_ref, buf_ref, gv_ref, bv_ref, wsp_ref, bsp_ref, wdc_ref, bdc_ref,
                         gdn_ref, bdn_ref, o_ref, vn_ref, glu_ref, *, width):
    u = p_ref[:, 0:width]
    v = p_ref[:, width:2 * width]
    a = p_ref[:, 2 * width:3 * width]
    gt = p_ref[:, 3 * width:4 * width]
    vn = _layernorm(v, gv_ref[...], bv_ref[...])
    vn_ref[...] = vn
    o_ref[:, 0:width] = u * (wsp_ref[...] * vn + bsp_ref[...])
    glu = a * _sigmoid(gt)
    glu_ref[...] = glu
    wdc = wdc_ref[...]
    dc = wdc[D_CONV - 1:D_CONV, :] * glu + bdc_ref[...]
    for kk in range(D_CONV - 1):
        dc = dc + wdc[kk:kk + 1, :] * buf_ref[kk]
    d = _layernorm(dc, gdn_ref[...], bdn_ref[...])
    o_ref[:, width:2 * width] = d * _sigmoid(d)


def _odd_mix_step(p, buf, g_v, b_v, w_sp_vec, b_sp_vec, w_dc, b_dc, g_dn, b_dn):
    b, w4 = p.shape
    width = w4 // 4
    vec = lambda x: x.reshape(1, width)
    return pl.pallas_call(
        functools.partial(_odd_mix_step_kernel, width=width),
        out_shape=(jax.ShapeDtypeStruct((b, 2 * width), F32), jax.ShapeDtypeStruct((b, width), F32),
                   jax.ShapeDtypeStruct((b, width), F32)),
        compiler_params=pltpu.CompilerParams(vmem_limit_bytes=VMEM_LIMIT_BYTES),
        name="odd_mix_step",
    )(p, buf, vec(g_v), vec(b_v), vec(w_sp_vec), vec(b_sp_vec), w_dc, vec(b_dc), vec(g_dn), vec(b_dn))


def _pad_rows(x, rows):
    return jnp.pad(x, ((0, rows - x.shape[0]), (0, 0)))


def _split_even_weights(w_t):
    sizes = (1024, 1024, 2048, 2048, 8, 8, 2048, 512, 512, 1024, 16, 64)
    offs = [0]
    for s in sizes:
        offs.append(offs[-1] + s)
    rows = lambda i: w_t[:, offs[i]:offs[i + 1], :]
    part_b = w_t[:, offs[6]:offs[10], :]
    small = jnp.concatenate([rows(4), rows(5), rows(10), rows(11),
                             jnp.zeros((w_t.shape[0], LANE - 96, w_t.shape[2]), w_t.dtype)], axis=1)
    return part_b, small


def _conv_ffn(h_p, h_s, tp, g, layer, w_up, w_fc, b_fc, w_down_b, buf_s):
    f = w_up.shape[2] // 2
    nb = buf_s.shape[0]
    xn = _rmsnorm(h_p, g, BF16)
    xs = _rmsnorm(h_s, g, BF16)
    act, tail_g, tail_v, hs_g, hs_v = _ffn_up(xn, xs, w_up, w_fc, b_fc.reshape(b_fc.shape[0], 1, 2 * f),
                                              layer, tp)
    h_p_new = _matmul_xres(act, w_down_b, layer, h_p)
    ff_p = jnp.concatenate([tail_g, tail_v], axis=-1)[:, 8 - (FF_CONV - 1):, :]
    act_s = _ffn_act_step(hs_g[:nb], hs_v[:nb], buf_s.transpose(1, 0, 2), w_fc[layer], b_fc[layer].reshape(1, 2 * f))
    h_s_new = _matmul(_pad_rows(act_s.astype(BF16), h_s.shape[0]), w_down_b, layer, res=h_s, tn=1024, tk=f // 2)
    ff_s = jnp.concatenate([buf_s[:, 1:, :], jnp.concatenate([hs_g[:nb], hs_v[:nb]], axis=-1)[:, None, :]], axis=1)
    return h_p_new, h_s_new, ff_p, ff_s


def kernel(x_prompt, x_sample, cache_k, cache_v, cache_kidx, state_mlstm_C, state_mlstm_n, state_mlstm_m,
           state_conv_d, state_ffn_conv, page_table, w_norm_mix, w_norm_ffn, w_in_even, b_gate_even,
           g_head_even, w_out_even, w_in_odd, g_vnorm, b_vnorm, w_spatial, b_spatial, w_dconv, b_dconv,
           g_dnorm, b_dnorm, w_out_odd, w_up, w_fconv, b_fconv, w_down, w_norm_final):
    bp, tp, d = x_prompt.shape
    bs = x_sample.shape[0]
    assert x_sample.shape[1] == 1
    ms = 16
    h_p = x_prompt.reshape(bp * tp, d)
    h_s = _pad_rows(x_sample.reshape(bs, d), ms)
    n_pages = page_table.shape[1]
    past = n_pages * PAGE

    w_even_t = jnp.swapaxes(w_in_even, 1, 2)
    w_b_t, w_small_t = _split_even_weights(w_even_t)
    cache_kidx_t = jnp.swapaxes(cache_kidx, 2, 3)
    w_down_b = w_down.astype(BF16)
    b_i = b_gate_even[0, :A_HEADS]
    b_f = b_gate_even[0, A_HEADS:]
    g_head = g_head_even[0]

    xn = _rmsnorm(h_p, w_norm_mix[0], BF16)
    xs = _rmsnorm(h_s, w_norm_mix[0], BF16)
    pa, pa_s, pa_b = _matmul_ws(xn, xs, w_even_t, 0, 0, _N_A, emit_bf16=True, w_transposed=True)
    pb, pb_s, pb_b = _matmul_ws(xn, xs, w_b_t, 0, 0, _N_B, emit_bf16=True, w_transposed=True)
    small, small_s = _matmul_ws(xn, xs, w_small_t, 0, 0, LANE, w_transposed=True, tn=LANE)
    pa_s = pa_s[:bs]
    pb_s = pb_s[:bs]
    small_s = small_s[:bs]
    pa3 = pa.reshape(bp, tp, _N_A)
    pab3 = pa_b.reshape(bp, tp, _N_A)
    pb3 = pb.reshape(bp, tp, _N_B)
    pbb3 = pb_b.reshape(bp, tp, _N_B)
    small3 = small.reshape(bp, tp, LANE)

    def heads_t(x):
        return x.transpose(0, 2, 1).reshape(bp * A_HEADS, tp)

    a_g, g_g, m_g = _mlstm_gates(heads_t(small3[:, :, _S_AI:_S_AI + A_HEADS]),
                                 heads_t(small3[:, :, _S_AF:_S_AF + A_HEADS]),
                                 jnp.tile(b_i, bp).reshape(-1, 1), jnp.tile(b_f, bp).reshape(-1, 1))
    a4 = a_g.reshape(bp, A_HEADS, 1, tp)
    g4 = g_g.reshape(bp, A_HEADS, 1, tp)
    m4 = m_g.reshape(bp, A_HEADS, 1, tp)
    h_a = _mlstm_prompt(pa3, pab3, a4, g4, m4, g_head.reshape(A_HEADS, 1, A_DV))
    c_p, n_p = _mlstm_prompt_state(pa3, a4, g4)
    m_p = m_g.reshape(bp, A_HEADS, tp)[:, :, -1]
    h_b = _dsa_prompt(pbb3, small3)
    mix = jnp.concatenate([h_a, h_b], axis=-1).reshape(bp * tp, -1)

    k_p = pb3[:, :, _BK:_BK + B_KV_HEADS * B_DH].reshape(bp, tp, B_KV_HEADS, B_DH)
    v_p = pb3[:, :, _BV:_BV + B_KV_HEADS * B_DH].reshape(bp, tp, B_KV_HEADS, B_DH)
    ki_p = small3[:, :, _S_IK:_S_IK + IDX_DIM]

    q_s = pa_s[:, _AQ:_AQ + 1024].reshape(bs, A_HEADS, 1, A_DQK)
    kk_s = pa_s[:, _AK:_AK + 1024].reshape(bs, A_HEADS, 1, A_DQK)
    v_s = pa_s[:, _AV:_AV + 2048].reshape(bs, A_HEADS, A_DV, 1)
    o_s = pa_s[:, _AO:_AO + 2048].reshape(bs, A_HEADS, A_DV, 1)
    ig_s = (small_s[:, _S_AI:_S_AI + A_HEADS] + b_i).reshape(bs, A_HEADS, 1, 1)
    fp_s = (small_s[:, _S_AF:_S_AF + A_HEADS] + b_f).reshape(bs, A_HEADS, 1, 1)
    hcol_s, c_s, n_s, m_s = _mlstm_step(
        q_s, kk_s, v_s, o_s, g_head.reshape(A_HEADS, A_DV, 1), ig_s, fp_s,
        state_mlstm_C[0], state_mlstm_n[0].reshape(bs, A_HEADS, 1, A_DQK),
        state_mlstm_m[0].reshape(bs, A_HEADS, 1, 1))
    h_a_s = hcol_s.reshape(bs, A_HEADS * A_DV)

    bq_s = pb_s[:, _BQ:_BQ + 2048].reshape(bs, B_HEADS, B_DH)
    bk_s = pb_s[:, _BK:_BK + 512]
    bv_s = pb_s[:, _BV:_BV + 512]
    iq_s = pb_s[:, _IQ:_IQ + 1024].reshape(bs, IDX_HEADS, IDX_DIM)
    iw_s = small_s[:, _S_IW:_S_IW + IDX_HEADS]
    ik_s = small_s[:, _S_IK:_S_IK + IDX_DIM]
    sc_past = _dsa_step_scores(page_table, iq_s, iw_s.reshape(bs, IDX_HEADS, 1), cache_kidx_t, 0)
    own_page = lambda x: jnp.pad(x[:, None], ((0, 0), (0, PAGE - 1)) + ((0, 0),) * (x.ndim - 1))
    own_ki_t = jnp.pad(ik_s[:, :, None], ((0, 0), (0, 0), (0, PAGE - 1)))[None]
    sc_self = _dsa_step_scores(jnp.arange(bs, dtype=I32).reshape(bs, 1), iq_s, iw_s.reshape(bs, IDX_HEADS, 1),
                               own_ki_t, 0)[:, 0, 0:1]
    topk_s = min(TOPK_MAX, (past + 1) // 4)
    sel, sel_self = _dsa_step_select(sc_past.reshape(bs, past), sc_self, topk_s)
    h_b_s = _dsa_step_attend(page_table, bq_s, sel.reshape(bs, 1, past),
                             jnp.pad(sel_self, ((0, 0), (0, PAGE - 1))).reshape(bs, 1, PAGE),
                             own_page(bk_s.reshape(bs, B_KV_HEADS, B_DH)), own_page(bv_s.reshape(bs, B_KV_HEADS, B_DH)),
                             cache_k, cache_v, 0)
    mix_s = jnp.concatenate([h_a_s, h_b_s.reshape(bs, B_HEADS * B_DH)], axis=-1).astype(BF16)
    h_p, h_s = _matmul_ws(mix, _pad_rows(mix_s, ms), w_out_even, 0, 0, d, res=h_p, res_s=h_s)

    h_p, h_s, ff_p0, ff_s0 = _conv_ffn(h_p, h_s, tp, w_norm_ffn[0], 0, w_up, w_fconv, b_fconv, w_down_b,
                                       state_ffn_conv[0])

    width = w_in_odd.shape[2] // 4
    xn = _rmsnorm(h_p, w_norm_mix[1], BF16)
    xs = _rmsnorm(h_s, w_norm_mix[1], BF16)
    proj1, proj1_s = _matmul_ws(xn, xs, w_in_odd, 0, 0, 4 * width)
    proj1_s = proj1_s[:bs]
    mix1, tail = _odd_mix(proj1.reshape(bp, tp, 4 * width), g_vnorm[0], b_vnorm[0], w_spatial[0], b_spatial[0],
                          w_dconv[0], b_dconv[0], g_dnorm[0], b_dnorm[0])
    cd_p = tail[:, tail.shape[1] - (D_CONV - 1):, :]
    gw = width // C_GROUPS
    mix1_s, vn_s, glu_s = _odd_mix_step(
        proj1_s, state_conv_d[0].transpose(1, 0, 2), g_vnorm[0], b_vnorm[0],
        jnp.repeat(w_spatial[0, :, 0, 0], gw), jnp.repeat(b_spatial[0, :, 0], gw),
        w_dconv[0], b_dconv[0], g_dnorm[0], b_dnorm[0])
    h_p, h_s = _matmul_ws(mix1.reshape(bp * tp, 2 * width), _pad_rows(mix1_s.astype(BF16), ms), w_out_odd, 0, 0, d,
                          res=h_p, res_s=h_s)
    cd_s = jnp.concatenate([state_conv_d[0][:, 1:, :], glu_s[:, None, :]], axis=1)

    h_p, h_s, ff_p1, ff_s1 = _conv_ffn(h_p, h_s, tp, w_norm_ffn[1], 1, w_up, w_fconv, b_fconv, w_down_b,
                                       state_ffn_conv[1])

    y_p = _rmsnorm(h_p, w_norm_final, F32).reshape(bp, tp, d)
    y_s = _rmsnorm(h_s, w_norm_final, F32)[:bs].reshape(bs, 1, d)

    return (y_p, y_s,
            k_p[None], v_p[None], ki_p[None],
            bk_s.reshape(1, bs, 1, B_KV_HEADS, B_DH), bv_s.reshape(1, bs, 1, B_KV_HEADS, B_DH),
            ik_s.reshape(1, bs, 1, IDX_DIM),
            c_p[None], n_p.reshape(1, bp, A_HEADS, A_DQK), m_p[None],
            c_s[None], n_s.reshape(1, bs, A_HEADS, A_DQK), m_s.reshape(1, bs, A_HEADS),
            vn_s.reshape(1, bs, 1, width),
            cd_p[None], cd_s[None],
            jnp.stack([ff_p0, ff_p1]), jnp.stack([ff_s0, ff_s1]))
```

```python
import functools
import math

import jax
import jax.numpy as jnp
from jax import lax
from jax.experimental import pallas as pl
from jax.experimental.pallas import tpu as pltpu

F32 = jnp.float32
BF16 = jnp.bfloat16
I32 = jnp.int32
EPS = 1e-6
NEG_INF = float("-inf")
LOG2_E = math.log2(math.e)
INT_MIN = -(2 ** 31)

A_HEADS = 8
A_DQK = 128
A_DV = 256
B_HEADS = 16
B_KV_HEADS = 4
B_DH = 128
IDX_HEADS = 16
IDX_DIM = 64
TOPK_MAX = 256
PAGE = 128
C_GROUPS = 8
C_CHUNK = 128
D_CONV = 31
FF_CONV = 3
LANE = 128
VMEM_LIMIT_BYTES = 48 * 1024 * 1024
WS_VMEM_LIMIT_BYTES = 56 * 1024 * 1024
FFN_ROW_CHUNK = 1024
DSA_PAGES_PER_STEP = 16

_AQ, _AK, _AV, _AO = 0, 1024, 2048, 4096
_N_A = 6144
_BQ, _BK, _BV, _IQ = 0, 2048, 2560, 3072
_N_B = 4096
_S_AI, _S_AF, _S_IW, _S_IK = 0, 8, 16, 32


def _params(sem):
    return pltpu.CompilerParams(dimension_semantics=sem, vmem_limit_bytes=VMEM_LIMIT_BYTES)


def _dot(a, b):
    return jnp.dot(a, b, preferred_element_type=F32)


def _dot_nt(a, b):
    return lax.dot_general(a, b, (((1,), (1,)), ((), ())), preferred_element_type=F32)


def _sigmoid(x):
    return 1.0 / (1.0 + jnp.exp(-x))


def _col(row):
    n = row.shape[1]
    r = lax.broadcasted_iota(I32, (n, n), 0)
    c = lax.broadcasted_iota(I32, (n, n), 1)
    return jnp.sum(jnp.where(r == c, row, 0.0), axis=1, keepdims=True)


def _rms_kernel(x_ref, g_ref, o_ref):
    x = x_ref[...]
    y = x * lax.rsqrt(jnp.mean(x * x, axis=-1, keepdims=True) + EPS)
    o_ref[...] = (y * g_ref[...]).astype(o_ref.dtype)


def _rmsnorm(x, g, out_dtype):
    m, d = x.shape
    tm = min(m, 256)
    return pl.pallas_call(
        _rms_kernel,
        grid=(m // tm,),
        in_specs=[pl.BlockSpec((tm, d), lambda i: (i, 0)), pl.BlockSpec((1, d), lambda i: (0, 0))],
        out_specs=pl.BlockSpec((tm, d), lambda i: (i, 0)),
        out_shape=jax.ShapeDtypeStruct((m, d), out_dtype),
        compiler_params=_params(("parallel",)),
        name="rmsnorm",
    )(x, g.reshape(1, d))


def _mm_kernel(*refs, nk, has_res):
    if has_res:
        x_ref, w_ref, r_ref, o_ref = refs[:4]
        scratch = refs[4:]
    else:
        x_ref, w_ref, o_ref = refs[:3]
        r_ref = None
        scratch = refs[3:]
    if nk == 1:
        acc = _dot(x_ref[...], w_ref[...])
        if has_res:
            acc = acc + r_ref[...]
        o_ref[...] = acc.astype(o_ref.dtype)
        return
    acc_ref, = scratch
    k = pl.program_id(2)

    @pl.when(k == 0)
    def _():
        acc_ref[...] = jnp.zeros_like(acc_ref)

    acc_ref[...] += _dot(x_ref[...], w_ref[...])

    @pl.when(k == nk - 1)
    def _():
        acc = acc_ref[...]
        if has_res:
            acc = acc + r_ref[...]
        o_ref[...] = acc.astype(o_ref.dtype)


def _matmul(x, w, layer=0, res=None, out_dtype=F32, tm=1024, tn=512, tk=None):
    m, kdim = x.shape
    n = w.shape[2]
    tm = min(tm, m)
    tn = min(tn, n)
    tk = kdim if tk is None else tk
    assert m % tm == 0 and n % tn == 0 and kdim % tk == 0
    nk = kdim // tk
    in_specs = [pl.BlockSpec((tm, tk), lambda i, j, k: (i, k)),
                pl.BlockSpec((None, tk, tn), lambda i, j, k: (layer, k, j))]
    args = [x, w]
    if res is not None:
        in_specs.append(pl.BlockSpec((tm, tn), lambda i, j, k: (i, j)))
        args.append(res)
    return pl.pallas_call(
        functools.partial(_mm_kernel, nk=nk, has_res=res is not None),
        grid=(m // tm, n // tn, nk),
        in_specs=in_specs,
        out_specs=pl.BlockSpec((tm, tn), lambda i, j, k: (i, j)),
        out_shape=jax.ShapeDtypeStruct((m, n), out_dtype),
        scratch_shapes=[pltpu.VMEM((tm, tn), F32)] if nk > 1 else [],
        compiler_params=_params(("parallel", "parallel", "arbitrary")),
        name="matmul",
    )(*args)


def _mm_ws_kernel(*refs, has_res, emit_bf16, w_transposed):
    refs = list(refs)
    x_ref, xs_ref, w_ref = refs[:3]
    r_ref, rs_ref = refs[3:5] if has_res else (None, None)
    outs = refs[5:] if has_res else refs[3:]
    o_ref, os_ref = outs[:2]
    ob_ref = outs[2] if emit_bf16 else None
    wb_ref = outs[-1]

    @pl.when(pl.program_id(1) == 0)
    def _():
        wb_ref[...] = (w_ref[...].T if w_transposed else w_ref[...]).astype(BF16)
        acc_s = _dot(xs_ref[...], wb_ref[...])
        os_ref[...] = acc_s + rs_ref[...] if has_res else acc_s

    acc = _dot(x_ref[...], wb_ref[...])
    if has_res:
        acc = acc + r_ref[...]
    o_ref[...] = acc
    if emit_bf16:
        ob_ref[...] = acc.astype(BF16)


def _matmul_ws(x, xs, w, layer, col0, n, res=None, res_s=None, emit_bf16=False, w_transposed=False,
               tm=1024, tn=512):
    m, kdim = x.shape
    ms = xs.shape[0]
    assert m % tm == 0 and n % tn == 0 and col0 % tn == 0
    c0 = col0 // tn
    has_res = res is not None
    out_specs = [pl.BlockSpec((tm, tn), lambda j, i: (i, j)), pl.BlockSpec((ms, tn), lambda j, i: (0, j))]
    out_shape = [jax.ShapeDtypeStruct((m, n), F32), jax.ShapeDtypeStruct((ms, n), F32)]
    if emit_bf16:
        out_specs.append(pl.BlockSpec((tm, tn), lambda j, i: (i, j)))
        out_shape.append(jax.ShapeDtypeStruct((m, n), BF16))
    w_spec = (pl.BlockSpec((None, tn, kdim), lambda j, i: (layer, c0 + j, 0)) if w_transposed else
              pl.BlockSpec((None, kdim, tn), lambda j, i: (layer, 0, c0 + j)))
    in_specs = [pl.BlockSpec((tm, kdim), lambda j, i: (i, 0)),
                pl.BlockSpec((ms, kdim), lambda j, i: (0, 0)),
                w_spec]
    args = [x, xs, w]
    if has_res:
        in_specs += [pl.BlockSpec((tm, tn), lambda j, i: (i, j)), pl.BlockSpec((ms, tn), lambda j, i: (0, j))]
        args += [res, res_s]
    return pl.pallas_call(
        functools.partial(_mm_ws_kernel, has_res=has_res, emit_bf16=emit_bf16, w_transposed=w_transposed),
        grid=(n // tn, m // tm),
        in_specs=in_specs,
        out_specs=tuple(out_specs),
        out_shape=tuple(out_shape),
        scratch_shapes=[pltpu.VMEM((kdim, tn), BF16)],
        compiler_params=pltpu.CompilerParams(dimension_semantics=("parallel", "arbitrary"),
                                             vmem_limit_bytes=WS_VMEM_LIMIT_BYTES),
        name="matmul_ws",
    )(*args)


def _matmul_xres(x, w, layer, res, tm=512, tn=512):
    m, kdim = x.shape
    n = w.shape[2]
    assert m % tm == 0 and n % tn == 0
    return pl.pallas_call(
        functools.partial(_mm_kernel, nk=1, has_res=True),
        grid=(m // tm, n // tn),
        in_specs=[pl.BlockSpec((tm, kdim), lambda i, j: (i, 0)),
                  pl.BlockSpec((None, kdim, tn), lambda i, j: (layer, 0, j)),
                  pl.BlockSpec((tm, tn), lambda i, j: (i, j))],
        out_specs=pl.BlockSpec((tm, tn), lambda i, j: (i, j)),
        out_shape=jax.ShapeDtypeStruct((m, n), F32),
        compiler_params=_params(("parallel", "arbitrary")),
        name="matmul_xres",
    )(x, w, res)


def _conv_rows(h, prev, w, b):
    def taps(x, shifted):
        y = w[FF_CONV - 1:FF_CONV, :] * x + b
        for s in range(1, FF_CONV):
            y = y + w[FF_CONV - 1 - s:FF_CONV - s, :] * shifted(s)
        return y

    bulk = taps(h, lambda s: pltpu.roll(h, s, 0))
    h8 = h[0:8, :]
    row = lax.broadcasted_iota(I32, h8.shape, 0)

    def head_shift(s):
        x = pltpu.roll(h8, s, 0)
        for r in range(s):
            x = jnp.where(row == r, prev[8 - s + r:8 - s + r + 1, :], x)
        return x

    return jnp.concatenate([taps(h8, head_shift), bulk[8:, :]], axis=0)


def _ffn_up_kernel(x_ref, xs_ref, wg_ref, wv_ref, cg_ref, cv_ref, bg_ref, bv_ref,
                   act_ref, tg_ref, tv_ref, sg_ref, sv_ref, wb_ref, carry_ref, *, tn, tiles_per_seq):
    i = pl.program_id(1)

    @pl.when(i == 0)
    def _():
        wb_ref[:, 0:tn] = wg_ref[...].astype(BF16)
        wb_ref[:, tn:2 * tn] = wv_ref[...].astype(BF16)
        hs = _dot(xs_ref[...], wb_ref[...])
        sg_ref[...] = hs[:, 0:tn]
        sv_ref[...] = hs[:, tn:2 * tn]

    @pl.when(i % tiles_per_seq == 0)
    def _():
        carry_ref[...] = jnp.zeros_like(carry_ref)

    tm = x_ref.shape[0]
    rc = min(tm, FFN_ROW_CHUNK)
    prev = carry_ref[...]
    for r in range(tm // rc):
        h = _dot(x_ref[r * rc:(r + 1) * rc, :], wb_ref[...])
        g = _conv_rows(h[:, 0:tn], prev[:, 0:tn], cg_ref[...], bg_ref[...])
        v = _conv_rows(h[:, tn:2 * tn], prev[:, tn:2 * tn], cv_ref[...], bv_ref[...])
        act_ref[r * rc:(r + 1) * rc, :] = (g * _sigmoid(g) * v).astype(act_ref.dtype)
        prev = h[rc - 8:rc, :]
    carry_ref[...] = prev

    @pl.when(i % tiles_per_seq == tiles_per_seq - 1)
    def _():
        tg_ref[0] = prev[:, 0:tn]
        tv_ref[0] = prev[:, tn:2 * tn]


def _ffn_up(x, xs, w_up, w_fc, b_fc, layer, seq_len, tm=1024, tn=256):
    m, kdim = x.shape
    ms = xs.shape[0]
    f = w_up.shape[2] // 2
    nf = f // tn
    tm = min(tm, seq_len)
    assert m % tm == 0 and seq_len % tm == 0 and f % tn == 0
    tiles_per_seq = seq_len // tm
    nseq = m // seq_len
    return pl.pallas_call(
        functools.partial(_ffn_up_kernel, tn=tn, tiles_per_seq=tiles_per_seq),
        grid=(nf, m // tm),
        in_specs=[pl.BlockSpec((tm, kdim), lambda j, i: (i, 0)),
                  pl.BlockSpec((ms, kdim), lambda j, i: (0, 0)),
                  pl.BlockSpec((None, kdim, tn), lambda j, i: (layer, 0, j)),
                  pl.BlockSpec((None, kdim, tn), lambda j, i: (layer, 0, j + nf)),
                  pl.BlockSpec((None, FF_CONV, tn), lambda j, i: (layer, 0, j)),
                  pl.BlockSpec((None, FF_CONV, tn), lambda j, i: (layer, 0, j + nf)),
                  pl.BlockSpec((None, 1, tn), lambda j, i: (layer, 0, j)),
                  pl.BlockSpec((None, 1, tn), lambda j, i: (layer, 0, j + nf))],
        out_specs=(pl.BlockSpec((tm, tn), lambda j, i: (i, j)),
                   pl.BlockSpec((1, 8, tn), lambda j, i: (i // tiles_per_seq, 0, j)),
                   pl.BlockSpec((1, 8, tn), lambda j, i: (i // tiles_per_seq, 0, j)),
                   pl.BlockSpec((ms, tn), lambda j, i: (0, j)),
                   pl.BlockSpec((ms, tn), lambda j, i: (0, j))),
        out_shape=(jax.ShapeDtypeStruct((m, f), BF16),
                   jax.ShapeDtypeStruct((nseq, 8, f), F32), jax.ShapeDtypeStruct((nseq, 8, f), F32),
                   jax.ShapeDtypeStruct((ms, f), F32), jax.ShapeDtypeStruct((ms, f), F32)),
        scratch_shapes=[pltpu.VMEM((kdim, 2 * tn), BF16), pltpu.VMEM((8, 2 * tn), F32)],
        compiler_params=pltpu.CompilerParams(dimension_semantics=("parallel", "arbitrary"),
                                             vmem_limit_bytes=WS_VMEM_LIMIT_BYTES),
        name="ffn_up",
    )(x, xs, w_up, w_up, w_fc, w_fc, b_fc, b_fc)


def _ffn_act_step_kernel(hg_ref, hv_ref, bufg_ref, bufv_ref, wg_ref, wv_ref, bg_ref, bv_ref, o_ref):
    def conv(h_ref, buf_ref, w_ref, b_ref):
        w = w_ref[...]
        y = w[FF_CONV - 1:FF_CONV, :] * h_ref[...] + b_ref[...]
        for s in range(FF_CONV - 1):
            y = y + w[s:s + 1, :] * buf_ref[s]
        return y

    g = conv(hg_ref, bufg_ref, wg_ref, bg_ref)
    o_ref[...] = g * _sigmoid(g) * conv(hv_ref, bufv_ref, wv_ref, bv_ref)


def _ffn_act_step(h_g, h_v, buf, w_fc, b_fc):
    b, f = h_g.shape
    return pl.pallas_call(
        _ffn_act_step_kernel,
        out_shape=jax.ShapeDtypeStruct((b, f), F32),
        compiler_params=pltpu.CompilerParams(vmem_limit_bytes=VMEM_LIMIT_BYTES),
        name="ffn_act_step",
    )(h_g, h_v, buf[:, :, :f], buf[:, :, f:], w_fc[:, :f], w_fc[:, f:],
      b_fc[:, :f], b_fc[:, f:])


def _scan_lanes(x, op, fill):
    n = x.shape[1]
    lane = lax.broadcasted_iota(I32, x.shape, 1)
    d = 1
    while d < n:
        x = op(x, jnp.where(lane >= d, pltpu.roll(x, d, 1), fill))
        d *= 2
    return x


def _log_sigmoid(x):
    return jnp.minimum(x, 0.0) - jnp.log(1.0 + jnp.exp(-jnp.abs(x)))


def _gate_kernel(i_ref, f_ref, bi_ref, bf_ref, a_ref, g_ref, m_ref):
    ig = i_ref[...] + bi_ref[...]
    lf = _log_sigmoid(f_ref[...] + bf_ref[...])
    bsum = _scan_lanes(lf, jnp.add, 0.0)
    a = ig - bsum
    g = jnp.maximum(_scan_lanes(a, jnp.maximum, NEG_INF), 0.0)
    a_ref[...] = a
    g_ref[...] = g
    m_ref[...] = bsum + g


def _mlstm_gates(i_t, f_t, b_i, b_f):
    shp = jax.ShapeDtypeStruct(i_t.shape, F32)
    return pl.pallas_call(
        _gate_kernel, out_shape=(shp, shp, shp),
        compiler_params=pltpu.CompilerParams(vmem_limit_bytes=VMEM_LIMIT_BYTES),
        name="mlstm_gates",
    )(i_t, f_t, b_i, b_f)


def _mlstm_kernel(q_ref, k_ref, v_ref, o_ref, a_ref, g_ref, m_ref, gh_ref, out_ref, num_sc, den_sc, *, tq, hp):
    qi = pl.program_id(2)
    off_q = pl.multiple_of(qi * tq, tq)
    row = lax.broadcasted_iota(I32, (tq, tq), 0)
    col = lax.broadcasted_iota(I32, (tq, tq), 1)
    qs = [(q_ref[0, :, hh * A_DQK:(hh + 1) * A_DQK] * (A_DQK ** -0.5)).astype(BF16) for hh in range(hp)]
    g_cs = [_col(g_ref[0, hh, :, pl.ds(off_q, tq)]) for hh in range(hp)]
    num_sc[...] = jnp.zeros_like(num_sc)
    den_sc[...] = jnp.zeros_like(den_sc)

    def body(kb, carry):
        off = pl.multiple_of(kb * tq, tq)
        causal = (col + kb * tq) <= (row + qi * tq)
        for hh in range(hp):
            k = k_ref[0, pl.ds(off, tq), hh * A_DQK:(hh + 1) * A_DQK]
            v = v_ref[0, pl.ds(off, tq), hh * A_DV:(hh + 1) * A_DV]
            a = a_ref[0, hh, :, pl.ds(off, tq)]
            w = jnp.exp(jnp.where(causal, a - g_cs[hh], NEG_INF))
            s = _dot_nt(qs[hh], k) * w
            num_sc[hh] += _dot(s.astype(BF16), v)
            den_sc[hh] += jnp.sum(s, axis=1, keepdims=True)
        return carry

    lax.fori_loop(0, qi + 1, body, 0)
    for hh in range(hp):
        m_c = _col(m_ref[0, hh, :, pl.ds(off_q, tq)])
        h = num_sc[hh] / jnp.maximum(jnp.abs(den_sc[hh]), jnp.exp(-m_c))
        h = h * lax.rsqrt(jnp.mean(h * h, axis=-1, keepdims=True) + EPS) * gh_ref[hh]
        cols = slice(hh * A_DV, (hh + 1) * A_DV)
        out_ref[0, :, cols] = (_sigmoid(o_ref[0, :, cols]) * h).astype(out_ref.dtype)


def _mlstm_prompt(p3, pb3, a, g, m, g_head):
    b, t, _ = p3.shape
    tq = 256
    hp = 4
    row_spec = pl.BlockSpec((1, hp, 1, t), lambda i, h, j: (i, h, 0, 0))
    wq, wv = hp * A_DQK, hp * A_DV
    return pl.pallas_call(
        functools.partial(_mlstm_kernel, tq=tq, hp=hp),
        grid=(b, A_HEADS // hp, t // tq),
        in_specs=[pl.BlockSpec((1, tq, wq), lambda i, h, j: (i, j, _AQ // wq + h)),
                  pl.BlockSpec((1, t, wq), lambda i, h, j: (i, 0, _AK // wq + h)),
                  pl.BlockSpec((1, t, wv), lambda i, h, j: (i, 0, _AV // wv + h)),
                  pl.BlockSpec((1, tq, wv), lambda i, h, j: (i, j, _AO // wv + h)),
                  row_spec, row_spec, row_spec,
                  pl.BlockSpec((hp, 1, A_DV), lambda i, h, j: (h, 0, 0))],
        out_specs=pl.BlockSpec((1, tq, wv), lambda i, h, j: (i, j, h)),
        out_shape=jax.ShapeDtypeStruct((b, t, A_HEADS * A_DV), BF16),
        scratch_shapes=[pltpu.VMEM((hp, tq, A_DV), F32), pltpu.VMEM((hp, tq, 1), F32)],
        compiler_params=_params(("parallel", "parallel", "arbitrary")),
        name="mlstm_prompt",
    )(p3, pb3, pb3, p3, a, g, m, g_head)


def _mlstm_state_kernel(k_ref, v_ref, a_ref, g_ref, c_ref, n_ref, *, t, tc):
    g_last = g_ref[0, 0, :, t - 1:t]

    def body(cb, carry):
        c_acc, n_acc = carry
        off = pl.multiple_of(cb * tc, tc)
        w = _col(jnp.exp(a_ref[0, 0, :, pl.ds(off, tc)] - g_last))
        kw = k_ref[0, pl.ds(off, tc), :] * w
        vt = v_ref[0, pl.ds(off, tc), :].T.astype(BF16)
        return c_acc + _dot(vt, kw.astype(BF16)), n_acc + jnp.sum(kw, axis=0, keepdims=True)

    c_acc, n_acc = lax.fori_loop(0, t // tc, body,
                                 (jnp.zeros((A_DV, A_DQK), F32), jnp.zeros((1, A_DQK), F32)))
    c_ref[0, 0] = c_acc
    n_ref[0, 0] = n_acc


def _mlstm_prompt_state(p3, a, g):
    b, t, _ = p3.shape
    row_spec = pl.BlockSpec((1, 1, 1, t), lambda i, h: (i, h, 0, 0))
    return pl.pallas_call(
        functools.partial(_mlstm_state_kernel, t=t, tc=256),
        grid=(b, A_HEADS),
        in_specs=[pl.BlockSpec((1, t, A_DQK), lambda i, h: (i, 0, _AK // A_DQK + h)),
                  pl.BlockSpec((1, t, A_DV), lambda i, h: (i, 0, _AV // A_DV + h)),
                  row_spec, row_spec],
        out_specs=(pl.BlockSpec((1, 1, A_DV, A_DQK), lambda i, h: (i, h, 0, 0)),
                   pl.BlockSpec((1, 1, 1, A_DQK), lambda i, h: (i, h, 0, 0))),
        out_shape=(jax.ShapeDtypeStruct((b, A_HEADS, A_DV, A_DQK), F32),
                   jax.ShapeDtypeStruct((b, A_HEADS, 1, A_DQK), F32)),
        compiler_params=_params(("parallel", "parallel")),
        name="mlstm_prompt_state",
    )(p3, p3, a, g)


def _mlstm_step_kernel(q_ref, k_ref, v_ref, o_ref, gh_ref, ig_ref, fp_ref, c0_ref, n0_ref, m0_ref,
                       h_ref, c_ref, n_ref, m_ref):
    q = q_ref[0, 0] * (A_DQK ** -0.5)
    k = k_ref[0, 0]
    v = v_ref[0, 0]
    ig = ig_ref[0, 0]
    lf = _log_sigmoid(fp_ref[0, 0])
    c0 = c0_ref[0, 0]
    n0 = n0_ref[0, 0]
    m0 = m0_ref[0, 0]
    log_inter = lf + m0
    m_t = jnp.maximum(log_inter, ig)
    w_intra = jnp.exp(ig - m_t)
    w_inter = jnp.exp(log_inter - m_t)
    s = jnp.sum(q * k, axis=1, keepdims=True) * w_intra
    num = s * v + w_inter * jnp.sum(c0 * q, axis=1, keepdims=True)
    den = s + w_inter * jnp.sum(n0 * q, axis=1, keepdims=True)
    h = num / jnp.maximum(jnp.abs(den), jnp.exp(-m_t))
    h = h * lax.rsqrt(jnp.mean(h * h, axis=0, keepdims=True) + EPS) * gh_ref[0]
    h_ref[0, 0] = _sigmoid(o_ref[0, 0]) * h
    c_ref[0, 0] = w_inter * c0 + w_intra * (v * k)
    n_ref[0, 0] = w_inter * n0 + w_intra * k
    m_ref[0, 0] = m_t


def _mlstm_step(q, k, v_col, o_col, gh_col, ig, f_pre, c0, n0, m0):
    b = q.shape[0]
    h = A_HEADS
    row = pl.BlockSpec((1, 1, 1, A_DQK), lambda i, j: (i, j, 0, 0))
    colv = pl.BlockSpec((1, 1, A_DV, 1), lambda i, j: (i, j, 0, 0))
    one = pl.BlockSpec((1, 1, 1, 1), lambda i, j: (i, j, 0, 0))
    mat = pl.BlockSpec((1, 1, A_DV, A_DQK), lambda i, j: (i, j, 0, 0))
    return pl.pallas_call(
        _mlstm_step_kernel,
        grid=(b, h),
        in_specs=[row, row, colv, colv, pl.BlockSpec((1, A_DV, 1), lambda i, j: (j, 0, 0)),
                  one, one, mat, row, one],
        out_specs=(colv, mat, row, one),
        out_shape=(jax.ShapeDtypeStruct((b, h, A_DV, 1), F32),
                   jax.ShapeDtypeStruct((b, h, A_DV, A_DQK), F32),
                   jax.ShapeDtypeStruct((b, h, 1, A_DQK), F32),
                   jax.ShapeDtypeStruct((b, h, 1, 1), F32)),
        compiler_params=_params(("parallel", "parallel")),
        name="mlstm_step",
    )(q, k, v_col, o_col, gh_col, ig, f_pre, c0, n0, m0)


def _sort_key(x):
    x = jnp.where(x == 0.0, 0.0, x)
    bits = pltpu.bitcast(x, I32)
    return jnp.where(bits >= 0, bits, bits ^ jnp.int32(0x7FFFFFFF))


def _kth_largest_key(key, k):
    def count_ge(c):
        return jnp.sum(jnp.where(key >= c, 1.0, 0.0), axis=1, keepdims=True)

    kf = jnp.float32(k)
    thr = jnp.where(count_ge(jnp.int32(0)) >= kf, jnp.int32(0), jnp.int32(INT_MIN))
    thr = jnp.broadcast_to(thr, (key.shape[0], 1))

    def body(i, thr):
        cand = thr + lax.shift_left(jnp.int32(1), jnp.int32(30) - i)
        return jnp.where(count_ge(cand) >= kf, cand, thr)

    return lax.fori_loop(0, 31, body, thr)


def _tri_ones(n, dtype):
    r = lax.broadcasted_iota(I32, (n, n), 0)
    c = lax.broadcasted_iota(I32, (n, n), 1)
    return jnp.where(r <= c, 1.0, 0.0).astype(dtype)


def _select_topk(scores, visible, k):
    key = _sort_key(jnp.where(visible, scores, NEG_INF))
    thr = _kth_largest_key(key, k)
    gt = key > thr
    eq = jnp.logical_and(key == thr, visible)
    need = jnp.float32(k) - jnp.sum(jnp.where(gt, 1.0, 0.0), axis=1, keepdims=True)
    tri = _tri_ones(LANE, BF16)
    off = jnp.zeros((scores.shape[0], 1), F32)
    rank = []
    eqf = jnp.where(eq, 1.0, 0.0)
    for c in range(scores.shape[1] // LANE):
        p = _dot(eqf[:, c * LANE:(c + 1) * LANE].astype(BF16), tri) + off
        rank.append(p)
        off = p[:, LANE - 1:LANE]
    rank = jnp.concatenate(rank, axis=1)
    return jnp.logical_or(jnp.logical_and(gt, visible), jnp.logical_and(eq, rank <= need))


def _dsa_prompt_kernel(q_ref, k_ref, v_ref, qi_ref, sm_ref, smq_ref, o_ref, sc_ref, *, tq, topk, extents):
    j = pl.program_id(1)
    group = B_HEADS // B_KV_HEADS
    lane = lax.broadcasted_iota(I32, (tq, LANE), 1)

    def attend(e):
        ki = sm_ref[0, 0:e, _S_IK:_S_IK + IDX_DIM].astype(BF16)
        sc_ref[:, 0:e] = jnp.zeros((tq, e), F32)

        def idx_pair(hp, carry):
            qpair = qi_ref[0, :, pl.ds(pl.multiple_of(hp * LANE, LANE), LANE)]
            for half in range(LANE // IDX_DIM):
                s = _dot_nt(qpair[:, half * IDX_DIM:(half + 1) * IDX_DIM], ki) * (IDX_DIM ** -0.5)
                head = hp * (LANE // IDX_DIM) + half
                w = jnp.sum(jnp.where(lane == _S_IW + head, smq_ref[0], 0.0), axis=1, keepdims=True)
                sc_ref[:, 0:e] += jnp.maximum(s, 0.0) * w
            return carry

        lax.fori_loop(0, IDX_HEADS * IDX_DIM // LANE, idx_pair, 0)
        scores = sc_ref[:, 0:e] * (IDX_HEADS ** -0.5)
        qpos = lax.broadcasted_iota(I32, (tq, e), 0) + j * tq
        kpos = lax.broadcasted_iota(I32, (tq, e), 1)
        sel = _select_topk(scores, kpos <= qpos, topk)
        sc_ref[:, 0:e] = jnp.where(sel, 0.0, NEG_INF)

        def kv_head(n, carry):
            kv_off = pl.multiple_of(n * B_DH, B_DH)
            kn = k_ref[0, 0:e, pl.ds(kv_off, B_DH)]
            vn = v_ref[0, 0:e, pl.ds(kv_off, B_DH)]
            for gidx in range(group):
                q_off = pl.multiple_of((n * group + gidx) * B_DH, B_DH)
                s = _dot_nt(q_ref[0, :, pl.ds(q_off, B_DH)], kn) + sc_ref[:, 0:e]
                p = jnp.exp2((s - jnp.max(s, axis=1, keepdims=True)) * (B_DH ** -0.5 * LOG2_E))
                l = jnp.sum(p, axis=1, keepdims=True)
                o_ref[0, :, pl.ds(q_off, B_DH)] = (_dot(p.astype(BF16), vn) / l).astype(o_ref.dtype)
            return carry

        lax.fori_loop(0, B_KV_HEADS, kv_head, 0)

    last = (j + 1) * tq
    for idx, e in enumerate(extents):
        lo = extents[idx - 1] if idx else 0
        pl.when(jnp.logical_and(last > lo, last <= e))(functools.partial(attend, e))


def _dsa_prompt(pb3, small3):
    b, t, _ = pb3.shape
    tq = 256
    topk = min(TOPK_MAX, t // 4)
    n_ext = min(4, t // tq)
    extents = tuple(t * (i + 1) // n_ext for i in range(n_ext))
    assert all(e % tq == 0 for e in extents) and extents[0] >= topk
    w_q, w_kv, w_qi = B_HEADS * B_DH, B_KV_HEADS * B_DH, IDX_HEADS * IDX_DIM
    return pl.pallas_call(
        functools.partial(_dsa_prompt_kernel, tq=tq, topk=topk, extents=extents),
        grid=(b, t // tq),
        in_specs=[pl.BlockSpec((1, tq, w_q), lambda i, j: (i, j, _BQ // w_q)),
                  pl.BlockSpec((1, t, w_kv), lambda i, j: (i, 0, _BK // w_kv)),
                  pl.BlockSpec((1, t, w_kv), lambda i, j: (i, 0, _BV // w_kv)),
                  pl.BlockSpec((1, tq, w_qi), lambda i, j: (i, j, _IQ // w_qi)),
                  pl.BlockSpec((1, t, LANE), lambda i, j: (i, 0, 0)),
                  pl.BlockSpec((1, tq, LANE), lambda i, j: (i, j, 0))],
        out_specs=pl.BlockSpec((1, tq, w_q), lambda i, j: (i, j, 0)),
        out_shape=jax.ShapeDtypeStruct((b, t, w_q), BF16),
        scratch_shapes=[pltpu.VMEM((tq, t), F32)],
        compiler_params=_params(("parallel", "parallel")),
        name="dsa_prompt",
    )(pb3, pb3, pb3, pb3, small3, small3)


def _page_specs(block, layer, pages_per_step):
    def spec(r):
        zeros = (0,) * (len(block) - 2)
        return pl.BlockSpec(block, lambda i, p, pt: (layer, pt[i, p * pages_per_step + r]) + zeros)
    return [spec(r) for r in range(pages_per_step)]


def _dsa_step_scores_kernel(pt_ref, qi_ref, wi_ref, *refs):
    del pt_ref
    ki_refs, o_ref = refs[:-1], refs[-1]
    qi = qi_ref[0].astype(BF16)
    ki_t = jnp.concatenate([r[0] for r in ki_refs], axis=1).astype(BF16)
    s = _dot(qi, ki_t) * (IDX_DIM ** -0.5)
    o_ref[0] = jnp.sum(jnp.maximum(s, 0.0) * wi_ref[0], axis=0, keepdims=True) * (IDX_HEADS ** -0.5)


def _dsa_step_scores(page_table, qi, wi_col, cache_ki, layer):
    b, n_pages = page_table.shape
    g = math.gcd(n_pages, DSA_PAGES_PER_STEP)
    grid_spec = pltpu.PrefetchScalarGridSpec(
        num_scalar_prefetch=1,
        grid=(b, n_pages // g),
        in_specs=[pl.BlockSpec((1, IDX_HEADS, IDX_DIM), lambda i, p, pt: (i, 0, 0)),
                  pl.BlockSpec((1, IDX_HEADS, 1), lambda i, p, pt: (i, 0, 0))]
                 + _page_specs((None, 1, IDX_DIM, PAGE), layer, g),
        out_specs=pl.BlockSpec((1, 1, g * PAGE), lambda i, p, pt: (i, 0, p)),
    )
    return pl.pallas_call(
        _dsa_step_scores_kernel,
        grid_spec=grid_spec,
        out_shape=jax.ShapeDtypeStruct((b, 1, n_pages * PAGE), F32),
        compiler_params=_params(("parallel", "arbitrary")),
        name="dsa_step_scores",
    )(page_table, qi, wi_col, *([cache_ki] * g))


def _dsa_step_select_kernel(sc_ref, self_ref, sel_ref, selself_ref, *, topk):
    sc = sc_ref[...]
    sself = self_ref[...]
    key = _sort_key(sc)
    key_self = _sort_key(sself)
    kf = jnp.float32(topk)

    def count_ge(c):
        return (jnp.sum(jnp.where(key >= c, 1.0, 0.0), axis=1, keepdims=True)
                + jnp.where(key_self >= c, 1.0, 0.0))

    thr = jnp.where(count_ge(jnp.int32(0)) >= kf, jnp.int32(0), jnp.int32(INT_MIN))

    def body(i, thr):
        cand = thr + lax.shift_left(jnp.int32(1), jnp.int32(30) - i)
        return jnp.where(count_ge(cand) >= kf, cand, thr)

    thr = lax.fori_loop(0, 31, body, thr)
    gt = key > thr
    eq = key == thr
    n_gt = jnp.sum(jnp.where(gt, 1.0, 0.0), axis=1, keepdims=True) + jnp.where(key_self > thr, 1.0, 0.0)
    need = kf - n_gt
    rank = _scan_lanes(jnp.where(eq, 1.0, 0.0), jnp.add, 0.0)
    sel_ref[...] = jnp.where(jnp.logical_or(gt, jnp.logical_and(eq, rank <= need)), 1.0, 0.0)
    n_eq = rank[:, -1:]
    selself_ref[...] = jnp.where(
        jnp.logical_or(key_self > thr, jnp.logical_and(key_self == thr, n_eq < need)), 1.0, 0.0)


def _dsa_step_select(scores, self_score, topk):
    b, s = scores.shape
    return pl.pallas_call(
        functools.partial(_dsa_step_select_kernel, topk=topk),
        out_shape=(jax.ShapeDtypeStruct((b, s), F32), jax.ShapeDtypeStruct((b, 1), F32)),
        compiler_params=pltpu.CompilerParams(vmem_limit_bytes=VMEM_LIMIT_BYTES),
        name="dsa_step_select",
    )(scores, self_score)


def _dsa_step_attend_kernel(pt_ref, q_ref, sel_ref, selself_ref, kn_ref, vn_ref, *refs, pages_per_step):
    del pt_ref
    g = pages_per_step
    k_refs, v_refs = refs[:g], refs[g:2 * g]
    o_ref, m_sc, l_sc, acc_sc = refs[2 * g:]
    p_idx = pl.program_id(1)
    group = B_HEADS // B_KV_HEADS

    @pl.when(p_idx == 0)
    def _():
        m_sc[...] = jnp.full_like(m_sc, NEG_INF)
        l_sc[...] = jnp.zeros_like(l_sc)
        acc_sc[...] = jnp.zeros_like(acc_sc)

    q = q_ref[0].astype(BF16)
    kv_of_row = lax.broadcasted_iota(I32, (B_HEADS, 1), 0) // group

    def head_rows(page_refs, n):
        return jnp.concatenate([r[0, :, n, :] for r in page_refs], axis=0).astype(BF16)

    def update(k_pages, v_pages, sel_row):
        s = jnp.zeros((B_HEADS, sel_row.shape[1]), F32)
        for n in range(B_KV_HEADS):
            s = jnp.where(kv_of_row == n, _dot_nt(q, head_rows(k_pages, n)), s)
        s = s * (B_DH ** -0.5) + jnp.where(sel_row > 0.0, 0.0, NEG_INF)
        m_old = m_sc[...]
        m_new = jnp.maximum(m_old, jnp.max(s, axis=1, keepdims=True))
        m_safe = jnp.where(m_new == NEG_INF, 0.0, m_new)
        alpha = jnp.exp(m_old - m_safe)
        p = jnp.exp(s - m_safe)
        pb = p.astype(BF16)
        pv = jnp.zeros((B_HEADS, B_DH), F32)
        for n in range(B_KV_HEADS):
            pv = jnp.where(kv_of_row == n, _dot(pb, head_rows(v_pages, n)), pv)
        l_sc[...] = alpha * l_sc[...] + jnp.sum(p, axis=1, keepdims=True)
        acc_sc[...] = alpha * acc_sc[...] + pv
        m_sc[...] = m_new

    update(k_refs, v_refs, sel_ref[0])

    @pl.when(p_idx == pl.num_programs(1) - 1)
    def _():
        update([kn_ref], [vn_ref], selself_ref[0])
        o_ref[0] = acc_sc[...] / l_sc[...]


def _dsa_step_attend(page_table, q, sel, sel_self, k_new, v_new, cache_k, cache_v, layer):
    b, n_pages = page_table.shape
    g = math.gcd(n_pages, DSA_PAGES_PER_STEP)
    own = pl.BlockSpec((1, PAGE, B_KV_HEADS, B_DH), lambda i, p, pt: (i, 0, 0, 0))
    page_block = (None, 1, PAGE, B_KV_HEADS, B_DH)
    grid_spec = pltpu.PrefetchScalarGridSpec(
        num_scalar_prefetch=1,
        grid=(b, n_pages // g),
        in_specs=[pl.BlockSpec((1, B_HEADS, B_DH), lambda i, p, pt: (i, 0, 0)),
                  pl.BlockSpec((1, 1, g * PAGE), lambda i, p, pt: (i, 0, p)),
                  pl.BlockSpec((1, 1, PAGE), lambda i, p, pt: (i, 0, 0)),
                  own, own]
                 + _page_specs(page_block, layer, g) + _page_specs(page_block, layer, g),
        out_specs=pl.BlockSpec((1, B_HEADS, B_DH), lambda i, p, pt: (i, 0, 0)),
        scratch_shapes=[pltpu.VMEM((B_HEADS, 1), F32), pltpu.VMEM((B_HEADS, 1), F32),
                        pltpu.VMEM((B_HEADS, B_DH), F32)],
    )
    return pl.pallas_call(
        functools.partial(_dsa_step_attend_kernel, pages_per_step=g),
        grid_spec=grid_spec,
        out_shape=jax.ShapeDtypeStruct((b, B_HEADS, B_DH), F32),
        compiler_params=_params(("parallel", "arbitrary")),
        name="dsa_step_attend",
    )(page_table, q, sel, sel_self, k_new, v_new, *([cache_k] * g), *([cache_v] * g))


def _layernorm(x, g, b):
    xc = x - jnp.mean(x, axis=-1, keepdims=True)
    return xc * lax.rsqrt(jnp.mean(xc * xc, axis=-1, keepdims=True) + EPS) * g + b


def _odd_mix_kernel(u_ref, v_ref, a_ref, gt_ref, gv_ref, bv_ref, wsp_ref, bsp_ref, wdc_ref, bdc_ref,
                    gdn_ref, bdn_ref, o_ref, tail_ref, xbuf, slab, *, tt, width, halo):
    j = pl.program_id(1)
    nj = pl.num_programs(1)

    @pl.when(j == 0)
    def _():
        xbuf[0:halo, :] = jnp.zeros((halo, width), F32)
        xbuf[halo + tt:halo + tt + 8, :] = jnp.zeros((8, width), F32)

    vn = _layernorm(v_ref[0], gv_ref[...], bv_ref[...]).astype(BF16)
    r = lax.broadcasted_iota(I32, (tt, tt), 0)
    c = lax.broadcasted_iota(I32, (tt, tt), 1)
    gw = width // C_GROUPS
    for gi in range(C_GROUPS):
        wm = jnp.where(c <= r, wsp_ref[gi], 0.0).astype(BF16)
        sg = _dot(wm, vn[:, gi * gw:(gi + 1) * gw]) + bsp_ref[gi]
        o_ref[0, :, gi * gw:(gi + 1) * gw] = (u_ref[0, :, gi * gw:(gi + 1) * gw] * sg).astype(o_ref.dtype)

    glu = a_ref[0] * _sigmoid(gt_ref[0])
    xbuf[halo:halo + tt, :] = glu
    wdc = wdc_ref[...]
    dc = jnp.zeros((tt, width), F32) + bdc_ref[...]
    base = halo - (D_CONV - 1)
    for b8 in range(8):
        taps = [kk for kk in range(D_CONV) if (base + kk) % 8 == b8]
        src = xbuf
        if b8:
            slab[...] = xbuf[b8:b8 + tt + halo, :]
            src = slab
        for kk in taps:
            a8 = (base + kk) - b8
            dc = dc + wdc[kk:kk + 1, :] * src[a8:a8 + tt, :]
    d = _layernorm(dc, gdn_ref[...], bdn_ref[...])
    o_ref[0, :, width:2 * width] = (d * _sigmoid(d)).astype(o_ref.dtype)

    xbuf[0:halo, :] = glu[tt - halo:tt, :]

    @pl.when(j == nj - 1)
    def _():
        tail_ref[0] = glu[tt - halo:tt, :]


def _odd_mix(p3, g_v, b_v, w_sp, b_sp, w_dc, b_dc, g_dn, b_dn):
    b, t, w4 = p3.shape
    width = w4 // 4
    tt = C_CHUNK
    halo = 32
    vec = lambda x: x.reshape(1, width)
    vspec = pl.BlockSpec((1, width), lambda i, j: (0, 0))
    return pl.pallas_call(
        functools.partial(_odd_mix_kernel, tt=tt, width=width, halo=halo),
        grid=(b, t // tt),
        in_specs=[pl.BlockSpec((1, tt, width), lambda i, j: (i, j, 0)),
                  pl.BlockSpec((1, tt, width), lambda i, j: (i, j, 1)),
                  pl.BlockSpec((1, tt, width), lambda i, j: (i, j, 2)),
                  pl.BlockSpec((1, tt, width), lambda i, j: (i, j, 3)),
                  vspec, vspec,
                  pl.BlockSpec((C_GROUPS, tt, tt), lambda i, j: (0, 0, 0)),
                  pl.BlockSpec((C_GROUPS, tt, 1), lambda i, j: (0, 0, 0)),
                  pl.BlockSpec((D_CONV, width), lambda i, j: (0, 0)),
                  vspec, vspec, vspec],
        out_specs=(pl.BlockSpec((1, tt, 2 * width), lambda i, j: (i, j, 0)),
                   pl.BlockSpec((1, halo, width), lambda i, j: (i, 0, 0))),
        out_shape=(jax.ShapeDtypeStruct((b, t, 2 * width), BF16),
                   jax.ShapeDtypeStruct((b, halo, width), F32)),
        scratch_shapes=[pltpu.VMEM((halo + tt + 8, width), F32), pltpu.VMEM((halo + tt, width), F32)],
        compiler_params=_params(("parallel", "arbitrary")),
        name="odd_mix",
    )(p3, p3, p3, p3, vec(g_v), vec(b_v), w_sp, b_sp.reshape(C_GROUPS, tt, 1), w_dc, vec(b_dc),
      vec(g_dn), vec(b_dn))


def _odd_mix_step_kernel(p_ref, buf_ref, gv_ref, bv_ref, wsp_ref, bsp_ref, wdc_ref, bdc_ref,
                         gdn_ref, bdn_ref, o_ref, vn_ref, glu_ref, *, width):
    u = p_ref[:, 0:width]
    v = p_ref[:, width:2 * width]
    a = p_ref[:, 2 * width:3 * width]
    gt = p_ref[:, 3 * width:4 * width]
    vn = _layernorm(v, gv_ref[...], bv_ref[...])
    vn_ref[...] = vn
    o_ref[:, 0:width] = u * (wsp_ref[...] * vn + bsp_ref[...])
    glu = a * _sigmoid(gt)
    glu_ref[...] = glu
    wdc = wdc_ref[...]
    dc = wdc[D_CONV - 1:D_CONV, :] * glu + bdc_ref[...]
    for kk in range(D_CONV - 1):
        dc = dc + wdc[kk:kk + 1, :] * buf_ref[kk]
    d = _layernorm(dc, gdn_ref[...], bdn_ref[...])
    o_ref[:, width:2 * width] = d * _sigmoid(d)


def _odd_mix_step(p, buf, g_v, b_v, w_sp_vec, b_sp_vec, w_dc, b_dc, g_dn, b_dn):
    b, w4 = p.shape
    width = w4 // 4
    vec = lambda x: x.reshape(1, width)
    return pl.pallas_call(
        functools.partial(_odd_mix_step_kernel, width=width),
        out_shape=(jax.ShapeDtypeStruct((b, 2 * width), F32), jax.ShapeDtypeStruct((b, width), F32),
                   jax.ShapeDtypeStruct((b, width), F32)),
        compiler_params=pltpu.CompilerParams(vmem_limit_bytes=VMEM_LIMIT_BYTES),
        name="odd_mix_step",
    )(p, buf, vec(g_v), vec(b_v), vec(w_sp_vec), vec(b_sp_vec), w_dc, vec(b_dc), vec(g_dn), vec(b_dn))


def _pad_rows(x, rows):
    return jnp.pad(x, ((0, rows - x.shape[0]), (0, 0)))


def _split_even_weights(w_t):
    sizes = (1024, 1024, 2048, 2048, 8, 8, 2048, 512, 512, 1024, 16, 64)
    offs = [0]
    for s in sizes:
        offs.append(offs[-1] + s)
    rows = lambda i: w_t[:, offs[i]:offs[i + 1], :]
    part_b = w_t[:, offs[6]:offs[10], :]
    small = jnp.concatenate([rows(4), rows(5), rows(10), rows(11),
                             jnp.zeros((w_t.shape[0], LANE - 96, w_t.shape[2]), w_t.dtype)], axis=1)
    return part_b, small


def _conv_ffn(h_p, h_s, tp, g, layer, w_up, w_fc, b_fc, w_down_b, buf_s):
    f = w_up.shape[2] // 2
    nb = buf_s.shape[0]
    xn = _rmsnorm(h_p, g, BF16)
    xs = _rmsnorm(h_s, g, BF16)
    act, tail_g, tail_v, hs_g, hs_v = _ffn_up(xn, xs, w_up, w_fc, b_fc.reshape(b_fc.shape[0], 1, 2 * f),
                                              layer, tp)
    h_p_new = _matmul_xres(act, w_down_b, layer, h_p)
    ff_p = jnp.concatenate([tail_g, tail_v], axis=-1)[:, 8 - (FF_CONV - 1):, :]
    act_s = _ffn_act_step(hs_g[:nb], hs_v[:nb], buf_s.transpose(1, 0, 2), w_fc[layer], b_fc[layer].reshape(1, 2 * f))
    h_s_new = _matmul(_pad_rows(act_s.astype(BF16), h_s.shape[0]), w_down_b, layer, res=h_s, tn=1024, tk=f // 2)
    ff_s = jnp.concatenate([buf_s[:, 1:, :], jnp.concatenate([hs_g[:nb], hs_v[:nb]], axis=-1)[:, None, :]], axis=1)
    return h_p_new, h_s_new, ff_p, ff_s


def kernel(x_prompt, x_sample, cache_k, cache_v, cache_kidx, state_mlstm_C, state_mlstm_n, state_mlstm_m,
           state_conv_d, state_ffn_conv, page_table, w_norm_mix, w_norm_ffn, w_in_even, b_gate_even,
           g_head_even, w_out_even, w_in_odd, g_vnorm, b_vnorm, w_spatial, b_spatial, w_dconv, b_dconv,
           g_dnorm, b_dnorm, w_out_odd, w_up, w_fconv, b_fconv, w_down, w_norm_final):
    bp, tp, d = x_prompt.shape
    bs = x_sample.shape[0]
    assert x_sample.shape[1] == 1
    ms = 16
    h_p = x_prompt.reshape(bp * tp, d)
    h_s = _pad_rows(x_sample.reshape(bs, d), ms)
    n_pages = page_table.shape[1]
    past = n_pages * PAGE

    w_even_t = jnp.swapaxes(w_in_even, 1, 2)
    w_b_t, w_small_t = _split_even_weights(w_even_t)
    cache_kidx_t = jnp.swapaxes(cache_kidx, 2, 3)
    w_down_b = w_down.astype(BF16)
    b_i = b_gate_even[0, :A_HEADS]
    b_f = b_gate_even[0, A_HEADS:]
    g_head = g_head_even[0]

    xn = _rmsnorm(h_p, w_norm_mix[0], BF16)
    xs = _rmsnorm(h_s, w_norm_mix[0], BF16)
    pa, pa_s, pa_b = _matmul_ws(xn, xs, w_even_t, 0, 0, _N_A, emit_bf16=True, w_transposed=True)
    pb, pb_s, pb_b = _matmul_ws(xn, xs, w_b_t, 0, 0, _N_B, emit_bf16=True, w_transposed=True)
    small, small_s = _matmul_ws(xn, xs, w_small_t, 0, 0, LANE, w_transposed=True, tn=LANE)
    pa_s = pa_s[:bs]
    pb_s = pb_s[:bs]
    small_s = small_s[:bs]
    pa3 = pa.reshape(bp, tp, _N_A)
    pab3 = pa_b.reshape(bp, tp, _N_A)
    pb3 = pb.reshape(bp, tp, _N_B)
    pbb3 = pb_b.reshape(bp, tp, _N_B)
    small3 = small.reshape(bp, tp, LANE)

    def heads_t(x):
        return x.transpose(0, 2, 1).reshape(bp * A_HEADS, tp)

    a_g, g_g, m_g = _mlstm_gates(heads_t(small3[:, :, _S_AI:_S_AI + A_HEADS]),
                                 heads_t(small3[:, :, _S_AF:_S_AF + A_HEADS]),
                                 jnp.tile(b_i, bp).reshape(-1, 1), jnp.tile(b_f, bp).reshape(-1, 1))
    a4 = a_g.reshape(bp, A_HEADS, 1, tp)
    g4 = g_g.reshape(bp, A_HEADS, 1, tp)
    m4 = m_g.reshape(bp, A_HEADS, 1, tp)
    h_a = _mlstm_prompt(pa3, pab3, a4, g4, m4, g_head.reshape(A_HEADS, 1, A_DV))
    c_p, n_p = _mlstm_prompt_state(pa3, a4, g4)
    m_p = m_g.reshape(bp, A_HEADS, tp)[:, :, -1]
    h_b = _dsa_prompt(pbb3, small3)
    mix = jnp.concatenate([h_a, h_b], axis=-1).reshape(bp * tp, -1)

    k_p = pb3[:, :, _BK:_BK + B_KV_HEADS * B_DH].reshape(bp, tp, B_KV_HEADS, B_DH)
    v_p = pb3[:, :, _BV:_BV + B_KV_HEADS * B_DH].reshape(bp, tp, B_KV_HEADS, B_DH)
    ki_p = small3[:, :, _S_IK:_S_IK + IDX_DIM]

    q_s = pa_s[:, _AQ:_AQ + 1024].reshape(bs, A_HEADS, 1, A_DQK)
    kk_s = pa_s[:, _AK:_AK + 1024].reshape(bs, A_HEADS, 1, A_DQK)
    v_s = pa_s[:, _AV:_AV + 2048].reshape(bs, A_HEADS, A_DV, 1)
    o_s = pa_s[:, _AO:_AO + 2048].reshape(bs, A_HEADS, A_DV, 1)
    ig_s = (small_s[:, _S_AI:_S_AI + A_HEADS] + b_i).reshape(bs, A_HEADS, 1, 1)
    fp_s = (small_s[:, _S_AF:_S_AF + A_HEADS] + b_f).reshape(bs, A_HEADS, 1, 1)
    hcol_s, c_s, n_s, m_s = _mlstm_step(
        q_s, kk_s, v_s, o_s, g_head.reshape(A_HEADS, A_DV, 1), ig_s, fp_s,
        state_mlstm_C[0], state_mlstm_n[0].reshape(bs, A_HEADS, 1, A_DQK),
        state_mlstm_m[0].reshape(bs, A_HEADS, 1, 1))
    h_a_s = hcol_s.reshape(bs, A_HEADS * A_DV)

    bq_s = pb_s[:, _BQ:_BQ + 2048].reshape(bs, B_HEADS, B_DH)
    bk_s = pb_s[:, _BK:_BK + 512]
    bv_s = pb_s[:, _BV:_BV + 512]
    iq_s = pb_s[:, _IQ:_IQ + 1024].reshape(bs, IDX_HEADS, IDX_DIM)
    iw_s = small_s[:, _S_IW:_S_IW + IDX_HEADS]
    ik_s = small_s[:, _S_IK:_S_IK + IDX_DIM]
    sc_past = _dsa_step_scores(page_table, iq_s, iw_s.reshape(bs, IDX_HEADS, 1), cache_kidx_t, 0)
    own_page = lambda x: jnp.pad(x[:, None], ((0, 0), (0, PAGE - 1)) + ((0, 0),) * (x.ndim - 1))
    own_ki_t = jnp.pad(ik_s[:, :, None], ((0, 0), (0, 0), (0, PAGE - 1)))[None]
    sc_self = _dsa_step_scores(jnp.arange(bs, dtype=I32).reshape(bs, 1), iq_s, iw_s.reshape(bs, IDX_HEADS, 1),
                               own_ki_t, 0)[:, 0, 0:1]
    topk_s = min(TOPK_MAX, (past + 1) // 4)
    sel, sel_self = _dsa_step_select(sc_past.reshape(bs, past), sc_self, topk_s)
    h_b_s = _dsa_step_attend(page_table, bq_s, sel.reshape(bs, 1, past),
                             jnp.pad(sel_self, ((0, 0), (0, PAGE - 1))).reshape(bs, 1, PAGE),
                             own_page(bk_s.reshape(bs, B_KV_HEADS, B_DH)), own_page(bv_s.reshape(bs, B_KV_HEADS, B_DH)),
                             cache_k, cache_v, 0)
    mix_s = jnp.concatenate([h_a_s, h_b_s.reshape(bs, B_HEADS * B_DH)], axis=-1).astype(BF16)
    h_p, h_s = _matmul_ws(mix, _pad_rows(mix_s, ms), w_out_even, 0, 0, d, res=h_p, res_s=h_s)

    h_p, h_s, ff_p0, ff_s0 = _conv_ffn(h_p, h_s, tp, w_norm_ffn[0], 0, w_up, w_fconv, b_fconv, w_down_b,
                                       state_ffn_conv[0])

    width = w_in_odd.shape[2] // 4
    xn = _rmsnorm(h_p, w_norm_mix[1], BF16)
    xs = _rmsnorm(h_s, w_norm_mix[1], BF16)
    proj1, proj1_s = _matmul_ws(xn, xs, w_in_odd, 0, 0, 4 * width)
    proj1_s = proj1_s[:bs]
    mix1, tail = _odd_mix(proj1.reshape(bp, tp, 4 * width), g_vnorm[0], b_vnorm[0], w_spatial[0], b_spatial[0],
                          w_dconv[0], b_dconv[0], g_dnorm[0], b_dnorm[0])
    cd_p = tail[:, tail.shape[1] - (D_CONV - 1):, :]
    gw = width // C_GROUPS
    mix1_s, vn_s, glu_s = _odd_mix_step(
        proj1_s, state_conv_d[0].transpose(1, 0, 2), g_vnorm[0], b_vnorm[0],
        jnp.repeat(w_spatial[0, :, 0, 0], gw), jnp.repeat(b_spatial[0, :, 0], gw),
        w_dconv[0], b_dconv[0], g_dnorm[0], b_dnorm[0])
    h_p, h_s = _matmul_ws(mix1.reshape(bp * tp, 2 * width), _pad_rows(mix1_s.astype(BF16), ms), w_out_odd, 0, 0, d,
                          res=h_p, res_s=h_s)
    cd_s = jnp.concatenate([state_conv_d[0][:, 1:, :], glu_s[:, None, :]], axis=1)

    h_p, h_s, ff_p1, ff_s1 = _conv_ffn(h_p, h_s, tp, w_norm_ffn[1], 1, w_up, w_fconv, b_fconv, w_down_b,
                                       state_ffn_conv[1])

    y_p = _rmsnorm(h_p, w_norm_final, F32).reshape(bp, tp, d)
    y_s = _rmsnorm(h_s, w_norm_final, F32)[:bs].reshape(bs, 1, d)

    return (y_p, y_s,
            k_p[None], v_p[None], ki_p[None],
            bk_s.reshape(1, bs, 1, B_KV_HEADS, B_DH), bv_s.reshape(1, bs, 1, B_KV_HEADS, B_DH),
            ik_s.reshape(1, bs, 1, IDX_DIM),
            c_p[None], n_p.reshape(1, bp, A_HEADS, A_DQK), m_p[None],
            c_s[None], n_s.reshape(1, bs, A_HEADS, A_DQK), m_s.reshape(1, bs, A_HEADS),
            vn_s.reshape(1, bs, 1, width),
            cd_p[None], cd_s[None],
            jnp.stack([ff_p0, ff_p1]), jnp.stack([ff_s0, ff_s1]))
```
